```python
import jax
import jax.numpy as jnp
from jax import lax
import numpy as np

D_MODEL = 2048
BATCH = 32
SEQ = 256
DEPTH = 2
DEC_BATCH = 8
DEC_SEQ = 1024
PAST_LEN = 512

GRID_W = 64
HEAD_DIM = 128
ROPE_BASE = 10000.0
QBLOCK = 128
CHUNK = 128
WIN_HEADS = 8
WIN_KV_HEADS = 2
WIN_GROUP = WIN_HEADS // WIN_KV_HEADS
WINDOW = 128
SSM_D_INNER = D_MODEL // 2
SSM_HEAD_DIM = 64
SSM_HEADS = SSM_D_INNER // SSM_HEAD_DIM
SSM_GROUPS = 2
SSM_STATE = 128
SSM_CONV = 7
SSM_CONV_DIM = SSM_D_INNER + 2 * SSM_GROUPS * SSM_STATE
RET_HEADS = 4
RET_QK_DIM = 128
RET_V_DIM = 256
NA_HEADS = 8
NA_ROWS = 8
NA_COLS = 16
N_BRANCH = 4
BRANCH_W = D_MODEL // 2
N_EXPERTS = 64
TOP_K = 8
N_EXPERT_GROUPS = 8
TOPK_GROUPS = 4
EXPERT_FF = 512
SHARED_FF = 512
ROUTED_SCALE = 2.5
MOE_BLOCK = 128
ALPHA = (2.0 * DEPTH) ** 0.25
BETA = (8.0 * DEPTH) ** -0.25
LN_EPS = 1e-5
RMS_EPS = 1e-6
IN_SPLITS = (WIN_HEADS * HEAD_DIM, WIN_KV_HEADS * HEAD_DIM, WIN_KV_HEADS * HEAD_DIM,
             SSM_D_INNER, SSM_CONV_DIM, 2 * SSM_HEADS,
             RET_HEADS * RET_QK_DIM, RET_HEADS * RET_QK_DIM, RET_HEADS * RET_V_DIM, RET_HEADS * RET_V_DIM,
             NA_HEADS * HEAD_DIM, NA_HEADS * HEAD_DIM, NA_HEADS * HEAD_DIM,
             N_BRANCH * D_MODEL)
IN_COLS = sum(IN_SPLITS)

kernel_name = 'hybrid_flow_trunk_step'


def layer_norm(x, g, b):
    xf = x.astype(jnp.float32)
    mu = jnp.mean(xf, -1, keepdims=True)
    var = jnp.mean(jnp.square(xf - mu), -1, keepdims=True)
    y = (xf - mu) * lax.rsqrt(var + LN_EPS) * g.astype(jnp.float32) + b.astype(jnp.float32)
    return y.astype(x.dtype)


def rms_norm(x):
    xf = x.astype(jnp.float32)
    return (xf * lax.rsqrt(jnp.mean(jnp.square(xf), -1, keepdims=True) + RMS_EPS)).astype(x.dtype)


def axial_rope_tables(n_tok, dtype):
    t = jnp.arange(n_tok)
    row = (t // GRID_W).astype(jnp.float32)
    col = (t % GRID_W).astype(jnp.float32)
    half = HEAD_DIM // 2
    inv = ROPE_BASE ** (-jnp.arange(0, half, 2, dtype=jnp.float32) / half)
    ar = row[:, None] * inv
    ac = col[:, None] * inv
    ang = jnp.concatenate([ar, ar, ac, ac], -1)[:, None, :]
    return jnp.cos(ang).astype(dtype), jnp.sin(ang).astype(dtype)


def apply_rope(x, cos, sin):
    a, b, c, d = jnp.split(x, 4, axis=-1)
    return x * cos + jnp.concatenate([-b, a, -d, c], -1) * sin


def softmax_parts(parts, dtype):
    sizes = [p.shape[-1] for p in parts]
    logits = jnp.concatenate([p.astype(jnp.float32) for p in parts], -1)
    probs = jax.nn.softmax(logits, axis=-1).astype(dtype)
    return jnp.split(probs, np.cumsum(sizes)[:-1].tolist(), axis=-1)


def ctx_attention(q, k, v, sink):
    bsz, n_tok, kvh, grp, dh = q.shape
    nb = n_tok // QBLOCK
    scale = dh ** -0.5
    qb = jnp.moveaxis(q.reshape(bsz, nb, QBLOCK, kvh, grp, dh), 1, 0)

    def block(qi):
        s = jnp.einsum('bqkgd,blkd->bkgql', qi, k) * scale
        parts = [s]
        if sink is not None:
            parts.append(jnp.broadcast_to(sink[None, :, :, None, None].astype(jnp.float32), s.shape[:-1] + (1,)))
        p = softmax_parts(parts, v.dtype)[0]
        return jnp.einsum('bkgql,blkd->bqkgd', p, v)

    o = lax.map(block, qb)
    return jnp.moveaxis(o, 0, 1).reshape(bsz, n_tok, kvh * grp * dh)


def window_attention(q, k, v, k_ctx, v_ctx, sink):
    bsz, n_tok, kvh, grp, dh = q.shape
    nb = n_tok // QBLOCK
    scale = dh ** -0.5
    qb = q.reshape(bsz, nb, QBLOCK, kvh, grp, dh)

    def band(t):
        tp = jnp.pad(t, ((0, 0), (QBLOCK, QBLOCK), (0, 0), (0, 0))).reshape(bsz, nb + 2, QBLOCK, kvh, dh)
        return jnp.concatenate([tp[:, :-2], tp[:, 1:-1], tp[:, 2:]], axis=2)

    kw, vw = band(k), band(v)
    blk = jnp.arange(nb)[:, None] * QBLOCK
    qpos = blk + jnp.arange(QBLOCK)[None]
    kpos = blk - QBLOCK + jnp.arange(3 * QBLOCK)[None]
    valid = ((jnp.abs(qpos[:, :, None] - kpos[:, None, :]) <= WINDOW)
             & (kpos[:, None, :] >= 0) & (kpos[:, None, :] < n_tok))
    s_loc = jnp.einsum('bnqkgd,bnjkd->bnkgqj', qb, kw) * scale
    s_loc = jnp.where(valid[None, :, None, None], s_loc.astype(jnp.float32), -jnp.inf)
    s_ctx = jnp.einsum('bnqkgd,blkd->bnkgql', qb, k_ctx) * scale
    s_sink = jnp.broadcast_to(sink[None, None, :, :, None, None].astype(jnp.float32), s_ctx.shape[:-1] + (1,))
    p_ctx, p_loc, _ = softmax_parts([s_ctx, s_loc, s_sink], v.dtype)
    o = (jnp.einsum('bnkgql,blkd->bnqkgd', p_ctx, v_ctx)
         + jnp.einsum('bnkgqj,bnjkd->bnqkgd', p_loc, vw))
    return o.reshape(bsz, n_tok, kvh * grp * dh)


def neighbourhood_attention(q, k, v, k_ctx, v_ctx, rpb):
    bsz, n_tok, nh, dh = q.shape
    rows = n_tok // GRID_W
    kr = min(NA_ROWS, rows)
    scale = dh ** -0.5
    r = jnp.arange(rows)
    row_idx = jnp.clip(r - kr // 2, 0, rows - kr)[:, None] + jnp.arange(kr)[None]
    cidx = jnp.arange(GRID_W)
    col_start = jnp.clip(cidx - NA_COLS // 2, 0, GRID_W - NA_COLS)
    col_ok = (cidx[None] >= col_start[:, None]) & (cidx[None] < col_start[:, None] + NA_COLS)
    dr = row_idx - r[:, None] + (NA_ROWS - 1)
    dc = jnp.clip(cidx[None] - cidx[:, None], 1 - NA_COLS, NA_COLS - 1) + (NA_COLS - 1)
    bias = rpb[:, dr[:, None, :, None], dc[None, :, None, :]]
    grid = lambda t: t.reshape(bsz, rows, GRID_W, nh, dh)
    qg = grid(q)
    kg = grid(k)[:, row_idx]
    vg = grid(v)[:, row_idx]
    s_loc = jnp.einsum('brchd,brkwhd->bhrckw', qg, kg) * scale + bias[None].astype(q.dtype)
    s_loc = jnp.where(col_ok[:, None, :], s_loc.astype(jnp.float32), -jnp.inf)
    s_loc = s_loc.reshape(bsz, nh, rows, GRID_W, kr * GRID_W)
    s_ctx = jnp.einsum('brchd,blhd->bhrcl', qg, k_ctx) * scale
    p_ctx, p_loc = softmax_parts([s_ctx, s_loc], v.dtype)
    p_loc = p_loc.reshape(bsz, nh, rows, GRID_W, kr, GRID_W)
    o = (jnp.einsum('bhrcl,blhd->brchd', p_ctx, v_ctx)
         + jnp.einsum('bhrckw,brkwhd->brchd', p_loc, vg))
    return o.reshape(bsz, n_tok, nh * dh)


def chunk_scan(q, k, v, log_a, s0):
    bsz, n_tok, nh, dk = q.shape
    dv = v.shape[-1]
    nc = n_tok // CHUNK
    dt = v.dtype
    qc = q.reshape(bsz, nc, CHUNK, nh, dk)
    kc = k.reshape(bsz, nc, CHUNK, nh, dk)
    vc = v.reshape(bsz, nc, CHUNK, nh, dv)
    acs = jnp.cumsum(log_a.astype(jnp.float32).reshape(bsz, nc, CHUNK, nh), axis=2)
    at = jnp.swapaxes(acs, 2, 3)
    tri = jnp.tril(jnp.ones((CHUNK, CHUNK), bool))
    decay = jnp.exp(jnp.where(tri, at[..., :, None] - at[..., None, :], -jnp.inf)).astype(dt)
    y_intra = jnp.einsum('bchij,bcjhv->bcihv', jnp.einsum('bcihk,bcjhk->bchij', qc, kc) * decay, vc)
    last = acs[:, :, -1:]
    kw = kc * jnp.exp(last - acs)[..., None].astype(dt)
    chunk_state = jnp.einsum('bcjhk,bcjhv->bchkv', kw, vc)
    chunk_decay = jnp.exp(last[:, :, 0])

    def step(s, inp):
        st, dcy = inp
        return s * dcy[:, :, None, None] + st.astype(jnp.float32), s

    s_fin, s_in = lax.scan(step, s0.astype(jnp.float32),
                           (jnp.moveaxis(chunk_state, 1, 0), jnp.moveaxis(chunk_decay, 1, 0)))
    s_in = jnp.moveaxis(s_in, 0, 1).astype(dt)
    y_inter = jnp.einsum('bcihk,bchkv->bcihv', qc * jnp.exp(acs)[..., None].astype(dt), s_in)
    return (y_intra + y_inter).reshape(bsz, n_tok, nh, dv), s_fin.astype(dt)


def bidir_scan(q, k, v_f, v_b, la_f, la_b, s0):
    y_f, s_f = chunk_scan(q, k, v_f, la_f, s0[:, 0])
    flip = lambda t: jnp.flip(t, axis=1)
    y_b, s_b = chunk_scan(flip(q), flip(k), flip(v_b), flip(la_b), s0[:, 1])
    return y_f + flip(y_b), jnp.stack([s_f, s_b], axis=1)


def centred_depthwise_conv(x, w):
    pad = SSM_CONV // 2
    return lax.conv_general_dilated(x, w[:, None, :].astype(x.dtype), (1,), [(pad, pad)],
                                    dimension_numbers=('NWC', 'WIO', 'NWC'),
                                    feature_group_count=x.shape[-1])


def swiglu(x, w_gu, w_down):
    g, u = jnp.split(x @ w_gu, 2, axis=-1)
    return (jax.nn.silu(g) * u) @ w_down


def moe(h, router_w, router_bias, exp_w_gu, exp_w_down, sh_w_gu, sh_w_down):
    bsz, n_tok, dm = h.shape
    n = bsz * n_tok
    hf = h.reshape(n, dm)
    scores = jax.nn.sigmoid((hf @ router_w).astype(jnp.float32))
    biased = scores + router_bias.astype(jnp.float32)
    per_group = N_EXPERTS // N_EXPERT_GROUPS
    grp_score = jnp.sum(lax.top_k(biased.reshape(n, N_EXPERT_GROUPS, per_group), 2)[0], -1)
    _, grp_idx = lax.top_k(grp_score, TOPK_GROUPS)
    grp_ok = jnp.sum(jax.nn.one_hot(grp_idx, N_EXPERT_GROUPS, dtype=jnp.float32), 1) > 0
    cand = jnp.where(jnp.repeat(grp_ok, per_group, axis=1), biased, -jnp.inf)
    _, top_idx = lax.top_k(cand, TOP_K)
    top_w = jnp.take_along_axis(scores, top_idx, 1)
    top_w = top_w / jnp.sum(top_w, -1, keepdims=True) * ROUTED_SCALE
    n_assign = n * TOP_K
    flat_e = top_idx.reshape(n_assign)
    flat_tok = jnp.arange(n_assign, dtype=jnp.int32) // TOP_K
    order = jnp.argsort(flat_e)
    e_s = flat_e[order]
    tok_s = flat_tok[order]
    w_s = top_w.reshape(n_assign)[order]
    counts = jnp.bincount(flat_e, length=N_EXPERTS)
    padded = (counts + MOE_BLOCK - 1) // MOE_BLOCK * MOE_BLOCK
    pad_end = jnp.cumsum(padded)
    pad_start = pad_end - padded
    start = jnp.cumsum(counts) - counts
    dest = pad_start[e_s] + jnp.arange(n_assign) - start[e_s]
    n_blocks = -(-n_assign // MOE_BLOCK) + N_EXPERTS
    slots = n_blocks * MOE_BLOCK
    buf_tok = jnp.zeros((slots,), jnp.int32).at[dest].set(tok_s)
    buf_w = jnp.zeros((slots,), jnp.float32).at[dest].set(w_s)
    block_e = jnp.minimum(jnp.searchsorted(pad_end, jnp.arange(n_blocks) * MOE_BLOCK, side='right'),
                          N_EXPERTS - 1)

    def expert_block(args):
        toks, e = args
        return swiglu(hf[toks], exp_w_gu[e], exp_w_down[e])

    y = lax.map(expert_block, (buf_tok.reshape(n_blocks, MOE_BLOCK), block_e))
    y = y.reshape(slots, dm) * buf_w[:, None].astype(h.dtype)
    routed = jax.ops.segment_sum(y, buf_tok, num_segments=n)
    return (routed + swiglu(hf, sh_w_gu, sh_w_down)).reshape(bsz, n_tok, dm)


def mixer(h, lw, rope, ctx):
    bsz, n_tok, _ = h.shape
    latent = ctx is not None
    u = h @ lw['w_in']
    (qa, ka, va, z, xbc, dt_raw, qr, kr, vr, gr, qn, kn, vn, gates) = jnp.split(
        u, np.cumsum(IN_SPLITS)[:-1].tolist(), axis=-1)
    shp = lambda t, nh, d: t.reshape(bsz, n_tok, nh, d)
    qa = shp(qa, WIN_HEADS, HEAD_DIM)
    ka = shp(ka, WIN_KV_HEADS, HEAD_DIM)
    va = shp(va, WIN_KV_HEADS, HEAD_DIM)
    sink = lw['win_sink'].reshape(WIN_KV_HEADS, WIN_GROUP)
    if latent:
        qa = apply_rope(qa, *rope)
        ka = apply_rope(ka, *rope)
        ya = window_attention(qa.reshape(bsz, n_tok, WIN_KV_HEADS, WIN_GROUP, HEAD_DIM), ka, va,
                              ctx[0], ctx[1], sink)
    else:
        ya = ctx_attention(qa.reshape(bsz, n_tok, WIN_KV_HEADS, WIN_GROUP, HEAD_DIM), ka, va, sink)
    xbc = jax.nn.silu(centred_depthwise_conv(xbc, lw['conv_w']) + lw['conv_b'])
    xs, bm, cm = jnp.split(xbc, [SSM_D_INNER, SSM_D_INNER + SSM_GROUPS * SSM_STATE], axis=-1)
    xs = shp(xs, SSM_HEADS, SSM_HEAD_DIM)
    rep = SSM_HEADS // SSM_GROUPS
    bm = jnp.repeat(shp(bm, SSM_GROUPS, SSM_STATE), rep, axis=2)
    cm = jnp.repeat(shp(cm, SSM_GROUPS, SSM_STATE), rep, axis=2)
    dt = jax.nn.softplus(shp(dt_raw, 2, SSM_HEADS).astype(jnp.float32) + lw['dt_bias'].astype(jnp.float32))
    log_a = dt * -jnp.exp(lw['a_log'].astype(jnp.float32))
    xdt = xs[:, :, None] * dt[..., None].astype(xs.dtype)
    s0 = ctx[2] if latent else jnp.zeros((bsz, 2, SSM_HEADS, SSM_STATE, SSM_HEAD_DIM), h.dtype)
    ys, ssm_fin = bidir_scan(cm, bm, xdt[:, :, 0], xdt[:, :, 1], log_a[:, :, 0], log_a[:, :, 1], s0)
    ys = (ys + xs * lw['d_skip'][:, None]).reshape(bsz, n_tok, SSM_D_INNER) * jax.nn.silu(z)
    ys = rms_norm(ys.reshape(bsz, n_tok, SSM_GROUPS, -1)).reshape(bsz, n_tok, SSM_D_INNER) * lw['ssm_norm']
    qr = shp(qr, RET_HEADS, RET_QK_DIM)
    kr = shp(kr, RET_HEADS, RET_QK_DIM) * (RET_QK_DIM ** -0.5)
    vr = shp(vr, RET_HEADS, RET_V_DIM)
    if latent:
        qr = apply_rope(qr, *rope)
        kr = apply_rope(kr, *rope)
    lg = jax.nn.log_sigmoid(lw['ret_decay'].astype(jnp.float32))
    la = jnp.broadcast_to(lg[:, None, None, :], (2, bsz, n_tok, RET_HEADS))
    s0 = ctx[3] if latent else jnp.zeros((bsz, 2, RET_HEADS, RET_QK_DIM, RET_V_DIM), h.dtype)
    yr, ret_fin = bidir_scan(qr, kr, vr, vr, la[0], la[1], s0)
    yr = rms_norm(yr).reshape(bsz, n_tok, RET_HEADS * RET_V_DIM) * jax.nn.silu(gr)
    qn = shp(qn, NA_HEADS, HEAD_DIM)
    kn = shp(kn, NA_HEADS, HEAD_DIM)
    vn = shp(vn, NA_HEADS, HEAD_DIM)
    if latent:
        yn = neighbourhood_attention(qn, kn, vn, ctx[4], ctx[5], lw['na_rpb'])
    else:
        yn = ctx_attention(qn[:, :, :, None], kn, vn, None)
    branches = jnp.stack([ya, ys, yr, yn], 0)
    proj = jnp.einsum('nble,ned->nbld', branches, lw['w_branch'])
    g = jax.nn.sigmoid(gates.reshape(bsz, n_tok, N_BRANCH, D_MODEL))
    out = jnp.einsum('nbld,blnd->bld', proj, g) @ lw['w_out']
    new_ctx = None if latent else (ka, va, ssm_fin, ret_fin, kn, vn)
    return out, new_ctx


def trunk_layer(x, cond, lw, rope, ctx):
    mod = (jax.nn.silu(cond) @ lw['w_ada'] + lw['b_ada']).reshape(-1, 1, 6 * D_MODEL)
    sh1, sc1, g1, sh2, sc2, g2 = jnp.split(mod, 6, axis=-1)
    mix, new_ctx = mixer(x * (1 + sc1) + sh1, lw, rope, ctx)
    x = layer_norm(ALPHA * x + g1 * mix, lw['ln1_g'], lw['ln1_b'])
    ff = moe(x * (1 + sc2) + sh2, lw['router_w'], lw['router_bias'], lw['exp_w_gu'], lw['exp_w_down'],
             lw['sh_w_gu'], lw['sh_w_down'])
    x = layer_norm(ALPHA * x + g2 * ff, lw['ln2_g'], lw['ln2_b'])
    return x, new_ctx


def setup_inputs(seed: int = 0) -> dict:
    key = jax.random.key(seed)
    ks = iter(jax.random.split(key, 40))
    f32 = jnp.float32
    nrm = lambda shape, s: jax.random.normal(next(ks), shape, f32) * s
    L = DEPTH
    ret_init = jnp.asarray(np.log(2.0 ** (5.0 + np.arange(RET_HEADS)) - 1.0).astype(np.float32))
    ret_decay = ret_init[None, None, :] + nrm((L, 2, RET_HEADS), 0.01)
    dt_init = jnp.exp(jax.random.uniform(next(ks), (L, 2, SSM_HEADS), f32, float(np.log(1e-3)), float(np.log(1e-1))))
    dt_bias = dt_init + jnp.log(-jnp.expm1(-dt_init))
    a_log = jnp.log(jax.random.uniform(next(ks), (L, 2, SSM_HEADS), f32, 1.0, 16.0))
    return {
        'x_prompt': nrm((BATCH, SEQ, D_MODEL), 1.0),
        'x_sample': nrm((DEC_BATCH, DEC_SEQ, D_MODEL), 1.0),
        'cache_win_k': nrm((DEC_BATCH, DEPTH, PAST_LEN, WIN_KV_HEADS, HEAD_DIM), 1.0),
        'cache_win_v': nrm((DEC_BATCH, DEPTH, PAST_LEN, WIN_KV_HEADS, HEAD_DIM), 1.0),
        'state_ssm': nrm((DEC_BATCH, DEPTH, 2, SSM_HEADS, SSM_STATE, SSM_HEAD_DIM), 0.5),
        'state_ret': nrm((DEC_BATCH, DEPTH, 2, RET_HEADS, RET_QK_DIM, RET_V_DIM), 0.5),
        'cache_na_k': nrm((DEC_BATCH, DEPTH, PAST_LEN, NA_HEADS, HEAD_DIM), 1.0),
        'cache_na_v': nrm((DEC_BATCH, DEPTH, PAST_LEN, NA_HEADS, HEAD_DIM), 1.0),
        'c': nrm((DEC_BATCH, D_MODEL), 1.0),
        'c_ctx': nrm((D_MODEL,), 1.0),
        'w_ada': nrm((L, D_MODEL, 6 * D_MODEL), 0.5 * D_MODEL ** -0.5),
        'b_ada': nrm((L, 6 * D_MODEL), 0.02),
        'w_in': nrm((L, D_MODEL, IN_COLS), D_MODEL ** -0.5),
        'win_sink': nrm((L, WIN_HEADS), 0.5),
        'conv_w': nrm((L, SSM_CONV, SSM_CONV_DIM), SSM_CONV ** -0.5),
        'conv_b': nrm((L, SSM_CONV_DIM), 0.02),
        'dt_bias': dt_bias,
        'a_log': a_log,
        'd_skip': 1.0 + nrm((L, SSM_HEADS), 0.02),
        'ssm_norm': 1.0 + nrm((L, SSM_D_INNER), 0.02),
        'ret_decay': ret_decay,
        'na_rpb': nrm((L, NA_HEADS, 2 * NA_ROWS - 1, 2 * NA_COLS - 1), 0.1),
        'w_branch': nrm((L, N_BRANCH, BRANCH_W, D_MODEL), BRANCH_W ** -0.5),
        'w_out': nrm((L, D_MODEL, D_MODEL), BETA * D_MODEL ** -0.5),
        'ln1_g': 1.0 + nrm((L, D_MODEL), 0.02),
        'ln1_b': nrm((L, D_MODEL), 0.02),
        'router_w': nrm((L, D_MODEL, N_EXPERTS), D_MODEL ** -0.5),
        'router_bias': nrm((L, N_EXPERTS), 0.01),
        'exp_w_gu': nrm((L, N_EXPERTS, D_MODEL, 2 * EXPERT_FF), D_MODEL ** -0.5),
        'exp_w_down': nrm((L, N_EXPERTS, EXPERT_FF, D_MODEL), BETA * EXPERT_FF ** -0.5),
        'sh_w_gu': nrm((L, D_MODEL, 2 * SHARED_FF), D_MODEL ** -0.5),
        'sh_w_down': nrm((L, SHARED_FF, D_MODEL), BETA * SHARED_FF ** -0.5),
        'ln2_g': 1.0 + nrm((L, D_MODEL), 0.02),
        'ln2_b': nrm((L, D_MODEL), 0.02),
    }


def reference(x_prompt, x_sample, cache_win_k, cache_win_v, state_ssm, state_ret, cache_na_k, cache_na_v,
              c, c_ctx, w_ada, b_ada, w_in, win_sink, conv_w, conv_b, dt_bias, a_log, d_skip, ssm_norm,
              ret_decay, na_rpb, w_branch, w_out, ln1_g, ln1_b, router_w, router_bias, exp_w_gu, exp_w_down,
              sh_w_gu, sh_w_down, ln2_g, ln2_b):
    rope = axial_rope_tables(x_sample.shape[1], x_sample.dtype)
    y_prompt, y_sample = x_prompt, x_sample
    new_ctx = []
    for l in range(DEPTH):
        lw = {'w_ada': w_ada[l], 'b_ada': b_ada[l], 'w_in': w_in[l], 'win_sink': win_sink[l],
              'conv_w': conv_w[l], 'conv_b': conv_b[l], 'dt_bias': dt_bias[l], 'a_log': a_log[l],
              'd_skip': d_skip[l], 'ssm_norm': ssm_norm[l], 'ret_decay': ret_decay[l], 'na_rpb': na_rpb[l],
              'w_branch': w_branch[l], 'w_out': w_out[l], 'ln1_g': ln1_g[l], 'ln1_b': ln1_b[l],
              'router_w': router_w[l], 'router_bias': router_bias[l], 'exp_w_gu': exp_w_gu[l],
              'exp_w_down': exp_w_down[l], 'sh_w_gu': sh_w_gu[l], 'sh_w_down': sh_w_down[l],
              'ln2_g': ln2_g[l], 'ln2_b': ln2_b[l]}
        y_prompt, ctx_l = trunk_layer(y_prompt, c_ctx, lw, None, None)
        new_ctx.append(ctx_l)
        cached = (cache_win_k[:, l], cache_win_v[:, l], state_ssm[:, l], state_ret[:, l],
                  cache_na_k[:, l], cache_na_v[:, l])
        y_sample, _ = trunk_layer(y_sample, c, lw, rope, cached)
    stack = lambda i: jnp.stack([t[i] for t in new_ctx], axis=1)
    return (y_prompt, y_sample, stack(0), stack(1), stack(2), stack(3), stack(4), stack(5))
```

```python
import functools

import jax
import jax.numpy as jnp
import numpy as np
from jax import lax
from jax.experimental import pallas as pl
from jax.experimental.pallas import tpu as pltpu

F32 = jnp.float32
BF16 = jnp.bfloat16
I32 = jnp.int32

D_MODEL = 2048
DEPTH = 2
GRID_W = 64
HEAD_DIM = 128
ROPE_BASE = 10000.0
CHUNK = 128
WIN_HEADS = 8
WIN_KV_HEADS = 2
WIN_GROUP = WIN_HEADS // WIN_KV_HEADS
WINDOW = 128
SSM_D_INNER = D_MODEL // 2
SSM_HEAD_DIM = 64
SSM_HEADS = SSM_D_INNER // SSM_HEAD_DIM
SSM_GROUPS = 2
SSM_STATE = 128
SSM_CONV = 7
SSM_CONV_DIM = SSM_D_INNER + 2 * SSM_GROUPS * SSM_STATE
RET_HEADS = 4
RET_QK_DIM = 128
RET_V_DIM = 256
NA_HEADS = 8
NA_ROWS = 8
NA_COLS = 16
N_BRANCH = 4
BRANCH_W = D_MODEL // 2
N_EXPERTS = 64
TOP_K = 8
N_EXPERT_GROUPS = 8
TOPK_GROUPS = 4
EXPERT_FF = 512
SHARED_FF = 512
ROUTED_SCALE = 2.5
ALPHA = (2.0 * DEPTH) ** 0.25
LN_EPS = 1e-5
RMS_EPS = 1e-6

U1_COLS = 4096
DT_OFF = 4096
U2_OFF = DT_OFF + 2 * SSM_HEADS
U2_COLS = 14336

TOK_TILE = 256
MOE_BM = 256
NA_WIN_ROWS = 10
NEG_BIG = -1e30

VMEM_LIMIT = 52 * 1024 * 1024


def _cp(*sem):
    return pltpu.CompilerParams(dimension_semantics=sem, vmem_limit_bytes=VMEM_LIMIT)


def _dot(a, b):
    return jnp.dot(a, b, preferred_element_type=F32)


def _dot_nt(a, b):
    return lax.dot_general(a, b, (((1,), (1,)), ((), ())), preferred_element_type=F32)


def _dot_exact(a, b):
    return jnp.dot(a, b, preferred_element_type=F32, precision=lax.Precision.HIGHEST)


def _silu(x):
    return x * jax.nn.sigmoid(x)


def _softplus(x):
    return jnp.maximum(x, 0.0) + jnp.log1p(jnp.exp(-jnp.abs(x)))


def _rope(x, cos, sin_a, sin_b):
    return x * cos + pltpu.roll(x, 96, 1) * sin_a + pltpu.roll(x, 32, 1) * sin_b


def _layer_norm(y, g, b):
    mu = jnp.mean(y, -1, keepdims=True)
    yc = y - mu
    var = jnp.mean(yc * yc, -1, keepdims=True)
    return yc * lax.rsqrt(var + LN_EPS) * g + b


def _ada_kernel(c_ref, w_ref, b_ref, o_ref):
    a = _silu(c_ref[...]).astype(BF16)
    o_ref[...] = _dot(a, w_ref[...].astype(BF16)) + b_ref[...]


def _ada(cond, w_ada, b_ada):
    nl, _, nout = w_ada.shape
    nr = cond.shape[0]
    tn = 1024
    return pl.pallas_call(
        _ada_kernel,
        grid=(nl, nout // tn),
        in_specs=[pl.BlockSpec((nr, D_MODEL), lambda l, j: (0, 0)),
                  pl.BlockSpec((None, D_MODEL, tn), lambda l, j: (l, 0, j)),
                  pl.BlockSpec((None, 1, tn), lambda l, j: (l, 0, j))],
        out_specs=pl.BlockSpec((None, nr, tn), lambda l, j: (l, 0, j)),
        out_shape=jax.ShapeDtypeStruct((nl, nr, nout), F32),
        compiler_params=_cp("arbitrary", "arbitrary"),
        name="ada",
    )(cond, w_ada, b_ada.reshape(nl, 1, nout))


def _mod_kernel(x_ref, m_ref, o_ref):
    o_ref[...] = (x_ref[...] * (1.0 + m_ref[1:2, :]) + m_ref[0:1, :]).astype(BF16)


def _modulate(x, mod, l, row_fn):
    tp = x.shape[0]
    return pl.pallas_call(
        _mod_kernel,
        grid=(tp // TOK_TILE,),
        in_specs=[pl.BlockSpec((TOK_TILE, D_MODEL), lambda i: (i, 0)),
                  pl.BlockSpec((None, None, 6, D_MODEL), lambda i: (l, row_fn(i), 0, 0))],
        out_specs=pl.BlockSpec((TOK_TILE, D_MODEL), lambda i: (i, 0)),
        out_shape=jax.ShapeDtypeStruct((tp, D_MODEL), BF16),
        compiler_params=_cp("arbitrary"),
        name="modulate",
    )(x, mod)


def _mm_kernel(x_ref, w_ref, o_ref, wbf):
    @pl.when(pl.program_id(1) == 0)
    def _():
        wbf[...] = w_ref[...].astype(BF16)

    o_ref[...] = _dot(x_ref[...], wbf[...])


def _in_proj(h, w, l, ncols):
    tp = h.shape[0]
    tm, tn = 512, 1024
    return pl.pallas_call(
        _mm_kernel,
        grid=(ncols // tn, tp // tm),
        in_specs=[pl.BlockSpec((tm, D_MODEL), lambda j, i: (i, 0)),
                  pl.BlockSpec((None, D_MODEL, tn), lambda j, i: (l, 0, j))],
        out_specs=pl.BlockSpec((tm, tn), lambda j, i: (i, j)),
        out_shape=jax.ShapeDtypeStruct((tp, ncols), F32),
        scratch_shapes=[pltpu.VMEM((D_MODEL, tn), BF16)],
        compiler_params=_cp("arbitrary", "arbitrary"),
        name="in_proj",
    )(h, w)


def _dt_kernel(h_ref, w_ref, wt_ref, b_ref, a_ref, bt_ref, at_ref, dt_ref, la_ref, lat_ref):
    h = h_ref[...]
    dt = _softplus(_dot(h, w_ref[...]) + b_ref[...])
    la = dt * -jnp.exp(a_ref[...])
    dtt = _softplus(_dot_nt(wt_ref[...], h) + bt_ref[...])
    lat = dtt * -jnp.exp(at_ref[...])
    for g in range(SSM_GROUPS):
        dt_ref[g] = dt[:, 16 * g:16 * (g + 1)]
        la_ref[g] = la[:, 16 * g:16 * (g + 1)]
        for q in range(h.shape[0] // CHUNK):
            lat_ref[g, q] = lat[16 * g:16 * (g + 1), q * CHUNK:(q + 1) * CHUNK]


def _ssd_steps(h, w_dt, dt_bias, a_log):
    tp = h.shape[0]
    tm = 512
    nq = tm // CHUNK
    return pl.pallas_call(
        _dt_kernel,
        grid=(tp // tm,),
        in_specs=[pl.BlockSpec((tm, D_MODEL), lambda i: (i, 0)),
                  pl.BlockSpec((D_MODEL, 32), lambda i: (0, 0)),
                  pl.BlockSpec((32, D_MODEL), lambda i: (0, 0)),
                  pl.BlockSpec((1, 32), lambda i: (0, 0)),
                  pl.BlockSpec((1, 32), lambda i: (0, 0)),
                  pl.BlockSpec((32, 1), lambda i: (0, 0)),
                  pl.BlockSpec((32, 1), lambda i: (0, 0))],
        out_specs=[pl.BlockSpec((SSM_GROUPS, tm, 16), lambda i: (0, i, 0)),
                   pl.BlockSpec((SSM_GROUPS, tm, 16), lambda i: (0, i, 0)),
                   pl.BlockSpec((SSM_GROUPS, nq, 16, CHUNK), lambda i: (0, i, 0, 0))],
        out_shape=[jax.ShapeDtypeStruct((SSM_GROUPS, tp, 16), F32),
                   jax.ShapeDtypeStruct((SSM_GROUPS, tp, 16), F32),
                   jax.ShapeDtypeStruct((SSM_GROUPS, tp // CHUNK, 16, CHUNK), F32)],
        compiler_params=_cp("arbitrary"),
        name="ssd_steps",
    )(h, w_dt, w_dt.T, dt_bias.reshape(1, 32), a_log.reshape(1, 32),
      dt_bias.reshape(32, 1), a_log.reshape(32, 1))


def _conv_kernel(x_ref, w_ref, b_ref, o_ref, pad_ref, *, seq):
    ct = x_ref.shape[1]
    pad = SSM_CONV // 2
    pad_ref[0:8, :] = jnp.zeros((8, ct), F32)
    pad_ref[8 + seq:16 + seq, :] = jnp.zeros((8, ct), F32)
    pad_ref[8:8 + seq, :] = x_ref[...]
    for r in range(seq // CHUNK):
        acc = jnp.broadcast_to(b_ref[...], (CHUNK, ct))
        for j in range(SSM_CONV):
            s = 8 - pad + j + r * CHUNK
            acc = acc + w_ref[j:j + 1, :] * pad_ref[s:s + CHUNK, :]
        o_ref[r * CHUNK:(r + 1) * CHUNK, :] = _silu(acc)


def _conv(u1, conv_w, conv_b, seq):
    tp = u1.shape[0]
    ct = 512
    c0 = 2560 // ct
    return pl.pallas_call(
        functools.partial(_conv_kernel, seq=seq),
        grid=(tp // seq, SSM_CONV_DIM // ct),
        in_specs=[pl.BlockSpec((seq, ct), lambda b, j: (b, c0 + j)),
                  pl.BlockSpec((SSM_CONV, ct), lambda b, j: (0, j)),
                  pl.BlockSpec((1, ct), lambda b, j: (0, j))],
        out_specs=pl.BlockSpec((seq, ct), lambda b, j: (b, j)),
        out_shape=jax.ShapeDtypeStruct((tp, SSM_CONV_DIM), F32),
        scratch_shapes=[pltpu.VMEM((seq + 16, ct), F32)],
        compiler_params=_cp("arbitrary", "arbitrary"),
        name="conv",
    )(u1, conv_w, conv_b.reshape(1, -1))


def _expand8(a, lane_lt64):
    tiles = [jnp.where(lane_lt64, a[:, 2 * p:2 * p + 1], a[:, 2 * p + 1:2 * p + 2]) for p in range(4)]
    return jnp.concatenate(tiles, axis=1)


def _ssd_kernel(*refs, seq, has_s0, want_fin):
    (xs_ref, bm_ref, cm_ref, z_ref, dt_ref, la_ref, lat_ref, dsk_ref, nrm_ref) = refs[:9]
    pos = 9
    s0_ref = None
    if has_s0:
        s0_ref = refs[pos]
        pos += 1
    y_ref = refs[pos]
    pos += 1
    fin_ref = None
    if want_fin:
        fin_ref = refs[pos]
        pos += 1
    st_ref = refs[pos]

    nc = seq // CHUNK
    hpg = SSM_HEADS // SSM_GROUPS
    ri = lax.broadcasted_iota(I32, (CHUNK, CHUNK), 0)
    ci = lax.broadcasted_iota(I32, (CHUNK, CHUNK), 1)
    lower = ri >= ci
    upper = ri <= ci
    lt = jnp.where(lower, 1.0, 0.0).astype(F32)
    ut = jnp.where(upper, 1.0, 0.0).astype(F32)
    lane_lt64 = lax.broadcasted_iota(I32, (1, 128), 1) < SSM_HEAD_DIM
    half0 = lax.broadcasted_iota(I32, (CHUNK, 128), 1) < SSM_HEAD_DIM

    for d in range(2):
        if has_s0:
            st_ref[d] = jnp.concatenate([s0_ref[d, hh] for hh in range(hpg)], axis=1)
        else:
            st_ref[d] = jnp.zeros((SSM_STATE, hpg * SSM_HEAD_DIM), F32)

    def chunk(c, d):
        r0 = pl.multiple_of(c * CHUNK, CHUNK)
        la_c = la_ref[pl.ds(r0, CHUNK), :][:, 8 * d:8 * d + 8]
        dt_c = dt_ref[pl.ds(r0, CHUNK), :][:, 8 * d:8 * d + 8]
        lat_c = lat_ref[c][8 * d:8 * d + 8, :]
        if d == 0:
            acs = _dot_exact(lt, la_c)
            acst = _dot_exact(lat_c, ut)
            mask = lower
            edge = acs[CHUNK - 1:CHUNK, :]
        else:
            acs = _dot_exact(ut, la_c)
            acst = _dot_exact(lat_c, lt)
            mask = upper
            edge = acs[0:1, :]
        xs_c = xs_ref[pl.ds(r0, CHUNK), :]
        b_c = bm_ref[pl.ds(r0, CHUNK), :]
        c_c = cm_ref[pl.ds(r0, CHUNK), :].astype(BF16)
        g = _dot_nt(c_c, b_c.astype(BF16))
        xdt = xs_c * _expand8(dt_c, lane_lt64)
        tiles = []
        for p in range(hpg // 2):
            xt = xdt[:, 128 * p:128 * (p + 1)]
            acc = None
            for q in range(2):
                hh = 2 * p + q
                dec = jnp.exp(jnp.where(mask, acs[:, hh:hh + 1] - acst[hh:hh + 1, :], -jnp.inf))
                v = jnp.where(half0 if q == 0 else jnp.logical_not(half0), xt, 0.0).astype(BF16)
                t = _dot((g * dec).astype(BF16), v)
                acc = t if acc is None else acc + t
            tiles.append(acc)
        y = jnp.concatenate(tiles, axis=1)
        st = st_ref[d]
        y = y + _expand8(jnp.exp(acs), lane_lt64) * _dot(c_c, st.astype(BF16))
        if d == 0:
            y_ref[pl.ds(r0, CHUNK), :] = y
        else:
            y_ref[pl.ds(r0, CHUNK), :] += y
        wgt = jnp.exp(edge - acs)
        v = (xdt * _expand8(wgt, lane_lt64)).astype(BF16)
        st_ref[d] = st * _expand8(jnp.exp(edge), lane_lt64) + _dot(b_c.T.astype(BF16), v)

    def fwd(c, carry):
        chunk(c, 0)
        return carry

    def bwd(c, carry):
        chunk(nc - 1 - c, 1)
        return carry

    lax.fori_loop(0, nc, fwd, 0)
    lax.fori_loop(0, nc, bwd, 0)

    for r in range(nc):
        sl = slice(r * CHUNK, (r + 1) * CHUNK)
        y = (y_ref[sl, :] + xs_ref[sl, :] * dsk_ref[...]) * _silu(z_ref[sl, :])
        y = y * lax.rsqrt(jnp.mean(y * y, -1, keepdims=True) + RMS_EPS)
        y_ref[sl, :] = y * nrm_ref[...]

    if want_fin:
        for d in range(2):
            st = st_ref[d]
            for hh in range(hpg):
                fin_ref[d, hh] = st[:, hh * SSM_HEAD_DIM:(hh + 1) * SSM_HEAD_DIM]


def _ssd(xc, u1, dt, la, lat, dsk, nrm, seq, l, s0):
    tp = xc.shape[0]
    nb = tp // seq
    nc = seq // CHUNK
    gw = SSM_D_INNER // SSM_GROUPS
    hpg = SSM_HEADS // SSM_GROUPS
    has_s0 = s0 is not None
    want_fin = not has_s0
    in_specs = [pl.BlockSpec((seq, gw), lambda b, g: (b, g)),
                pl.BlockSpec((seq, SSM_STATE), lambda b, g: (b, SSM_D_INNER // SSM_STATE + g)),
                pl.BlockSpec((seq, SSM_STATE), lambda b, g: (b, SSM_D_INNER // SSM_STATE + SSM_GROUPS + g)),
                pl.BlockSpec((seq, gw), lambda b, g: (b, 1536 // gw + g)),
                pl.BlockSpec((None, seq, 16), lambda b, g: (g, b, 0)),
                pl.BlockSpec((None, seq, 16), lambda b, g: (g, b, 0)),
                pl.BlockSpec((None, nc, 16, CHUNK), lambda b, g: (g, b, 0, 0)),
                pl.BlockSpec((1, gw), lambda b, g: (0, g)),
                pl.BlockSpec((1, gw), lambda b, g: (0, g))]
    args = [xc, xc, xc, u1, dt, la, lat, dsk, nrm]
    if has_s0:
        in_specs.append(pl.BlockSpec((None, None, 2, hpg, SSM_STATE, SSM_HEAD_DIM),
                                     lambda b, g: (b, l, 0, g, 0, 0)))
        args.append(s0)
    out_specs = [pl.BlockSpec((seq, gw), lambda b, g: (b, g))]
    out_shape = [jax.ShapeDtypeStruct((tp, SSM_D_INNER), F32)]
    if want_fin:
        out_specs.append(pl.BlockSpec((None, 2, hpg, SSM_STATE, SSM_HEAD_DIM), lambda b, g: (b, 0, g, 0, 0)))
        out_shape.append(jax.ShapeDtypeStruct((nb, 2, SSM_HEADS, SSM_STATE, SSM_HEAD_DIM), F32))
    res = pl.pallas_call(
        functools.partial(_ssd_kernel, seq=seq, has_s0=has_s0, want_fin=want_fin),
        grid=(nb, SSM_GROUPS),
        in_specs=in_specs,
        out_specs=out_specs,
        out_shape=out_shape,
        scratch_shapes=[pltpu.VMEM((2, SSM_STATE, gw), F32)],
        compiler_params=_cp("arbitrary", "arbitrary"),
        name="ssd_scan",
    )(*args)
    return (res[0], res[1]) if want_fin else (res[0], None)


def _ret_kernel(*refs, seq, l, latent):
    dec_ref, q_ref, k_ref, v_ref, g_ref = refs[:5]
    pos = 5
    if latent:
        cos_ref, sa_ref, sb_ref, s0_ref = refs[pos:pos + 4]
        pos += 4
    y_ref = refs[pos]
    pos += 1
    fin_ref = None
    if not latent:
        fin_ref = refs[pos]
        pos += 1
    qs_ref, ks_ref, st_ref = refs[pos:pos + 3]

    nc = seq // CHUNK
    hd = pl.program_id(1)
    if latent:
        qs_ref[...] = _rope(q_ref[...], cos_ref[...], sa_ref[...], sb_ref[...]).astype(BF16)
        ks_ref[...] = _rope(k_ref[...] * (RET_QK_DIM ** -0.5), cos_ref[...], sa_ref[...], sb_ref[...])
        for d in range(2):
            st_ref[d] = s0_ref[d]
    else:
        qs_ref[...] = q_ref[...].astype(BF16)
        ks_ref[...] = k_ref[...] * (RET_QK_DIM ** -0.5)
        for d in range(2):
            st_ref[d] = jnp.zeros((RET_QK_DIM, RET_V_DIM), F32)

    ri = lax.broadcasted_iota(I32, (CHUNK, CHUNK), 0)
    ci = lax.broadcasted_iota(I32, (CHUNK, CHUNK), 1)
    dist = (ri - ci).astype(F32)
    rowi = lax.broadcasted_iota(I32, (CHUNK, 1), 0).astype(F32)

    def direction(d):
        raw = jnp.full((1, 1), dec_ref[l * 2 * RET_HEADS + d * RET_HEADS + hd], F32)
        lg = -_softplus(-raw)
        if d == 0:
            dec = jnp.exp(jnp.where(ri >= ci, dist * lg, -jnp.inf))
            e_in = jnp.exp((rowi + 1.0) * lg)
            wgt = jnp.exp((CHUNK - 1.0 - rowi) * lg)
        else:
            dec = jnp.exp(jnp.where(ri <= ci, -dist * lg, -jnp.inf))
            e_in = jnp.exp((CHUNK - rowi) * lg)
            wgt = jnp.exp(rowi * lg)
        full = jnp.exp(CHUNK * lg)

        def body(c, carry):
            cc = c if d == 0 else nc - 1 - c
            r0 = pl.multiple_of(cc * CHUNK, CHUNK)
            q = qs_ref[pl.ds(r0, CHUNK), :]
            k = ks_ref[pl.ds(r0, CHUNK), :]
            v = v_ref[pl.ds(r0, CHUNK), :].astype(BF16)
            g = _dot_nt(q, k.astype(BF16))
            st = st_ref[d]
            y = _dot((g * dec).astype(BF16), v) + e_in * _dot(q, st.astype(BF16))
            if d == 0:
                y_ref[pl.ds(r0, CHUNK), :] = y
            else:
                y_ref[pl.ds(r0, CHUNK), :] += y
            st_ref[d] = st * full + _dot((k * wgt).T.astype(BF16), v)
            return carry

        lax.fori_loop(0, nc, body, 0)

    direction(0)
    direction(1)

    for r in range(nc):
        sl = slice(r * CHUNK, (r + 1) * CHUNK)
        y = y_ref[sl, :]
        y = y * lax.rsqrt(jnp.mean(y * y, -1, keepdims=True) + RMS_EPS)
        y_ref[sl, :] = y * _silu(g_ref[sl, :])

    if not latent:
        for d in range(2):
            fin_ref[d] = st_ref[d]


def _retention(u2, ret_decay, seq, l, rope, s0):
    tp = u2.shape[0]
    nb = tp // seq
    latent = s0 is not None
    in_specs = [pl.BlockSpec((seq, RET_QK_DIM), lambda b, h, *_: (b, h)),
                pl.BlockSpec((seq, RET_QK_DIM), lambda b, h, *_: (b, RET_HEADS + h)),
                pl.BlockSpec((seq, RET_V_DIM), lambda b, h, *_: (b, 1024 // RET_V_DIM + h)),
                pl.BlockSpec((seq, RET_V_DIM), lambda b, h, *_: (b, 2048 // RET_V_DIM + h))]
    args = [u2, u2, u2, u2]
    if latent:
        for t in rope:
            in_specs.append(pl.BlockSpec((seq, HEAD_DIM), lambda b, h, *_: (0, 0)))
            args.append(t)
        in_specs.append(pl.BlockSpec((None, None, 2, None, RET_QK_DIM, RET_V_DIM),
                                     lambda b, h, *_: (b, l, 0, h, 0, 0)))
        args.append(s0)
    out_specs = [pl.BlockSpec((seq, RET_V_DIM), lambda b, h, *_: (b, h))]
    out_shape = [jax.ShapeDtypeStruct((tp, RET_HEADS * RET_V_DIM), F32)]
    if not latent:
        out_specs.append(pl.BlockSpec((None, 2, None, RET_QK_DIM, RET_V_DIM), lambda b, h, *_: (b, 0, h, 0, 0)))
        out_shape.append(jax.ShapeDtypeStruct((nb, 2, RET_HEADS, RET_QK_DIM, RET_V_DIM), F32))
    res = pl.pallas_call(
        functools.partial(_ret_kernel, seq=seq, l=l, latent=latent),
        grid_spec=pltpu.PrefetchScalarGridSpec(
            num_scalar_prefetch=1,
            grid=(nb, RET_HEADS),
            in_specs=in_specs,
            out_specs=out_specs,
            scratch_shapes=[pltpu.VMEM((seq, RET_QK_DIM), BF16),
                            pltpu.VMEM((seq, RET_QK_DIM), F32),
                            pltpu.VMEM((2, RET_QK_DIM, RET_V_DIM), F32)]),
        out_shape=out_shape,
        compiler_params=_cp("arbitrary", "arbitrary"),
        name="retention",
    )(ret_decay.reshape(-1), *args)
    return (res[0], None) if latent else (res[0], res[1])


def _softmax_pv(s, v, sink):
    m = jnp.max(s, -1, keepdims=True)
    if sink is not None:
        m = jnp.maximum(m, sink)
    e = jnp.exp(s - m)
    den = jnp.sum(e, -1, keepdims=True)
    if sink is not None:
        den = den + jnp.exp(sink - m)
    return _dot(e.astype(BF16), v) / den


def _ctx_attn_kernel(sink_ref, qa_ref, ka_ref, va_ref, qn_ref, kn_ref, vn_ref, ya_ref, yn_ref, *, l):
    scale = HEAD_DIM ** -0.5
    for h in range(WIN_HEADS):
        kv = h // WIN_GROUP
        hs = slice(h * HEAD_DIM, (h + 1) * HEAD_DIM)
        ks = slice(kv * HEAD_DIM, (kv + 1) * HEAD_DIM)
        s = _dot_nt(qa_ref[:, hs].astype(BF16), ka_ref[:, ks].astype(BF16)) * scale
        ya_ref[:, hs] = _softmax_pv(s, va_ref[:, ks].astype(BF16), sink_ref[l * WIN_HEADS + h])
    for h in range(NA_HEADS):
        hs = slice(h * HEAD_DIM, (h + 1) * HEAD_DIM)
        s = _dot_nt(qn_ref[:, hs].astype(BF16), kn_ref[:, hs].astype(BF16)) * scale
        yn_ref[:, hs] = _softmax_pv(s, vn_ref[:, hs].astype(BF16), None)


def _ctx_attention(u1, u2, win_sink, seq, l):
    tp = u1.shape[0]
    nb = tp // seq
    kvw = WIN_KV_HEADS * HEAD_DIM
    hw = NA_HEADS * HEAD_DIM
    return pl.pallas_call(
        functools.partial(_ctx_attn_kernel, l=l),
        grid_spec=pltpu.PrefetchScalarGridSpec(
            num_scalar_prefetch=1,
            grid=(nb,),
            in_specs=[pl.BlockSpec((seq, hw), lambda b, *_: (b, 0)),
                      pl.BlockSpec((seq, kvw), lambda b, *_: (b, 1024 // kvw)),
                      pl.BlockSpec((seq, kvw), lambda b, *_: (b, 1280 // kvw)),
                      pl.BlockSpec((seq, hw), lambda b, *_: (b, 3072 // hw)),
                      pl.BlockSpec((seq, hw), lambda b, *_: (b, 4096 // hw)),
                      pl.BlockSpec((seq, hw), lambda b, *_: (b, 5120 // hw))],
            out_specs=[pl.BlockSpec((seq, hw), lambda b, *_: (b, 0)),
                       pl.BlockSpec((seq, hw), lambda b, *_: (b, 0))]),
        out_shape=[jax.ShapeDtypeStruct((tp, hw), F32), jax.ShapeDtypeStruct((tp, hw), F32)],
        compiler_params=_cp("arbitrary"),
        name="ctx_attention",
    )(win_sink.reshape(-1), u1, u1, u1, u2, u2, u2)


def _win_kernel(sink_ref, q_ref, k_ref, v_ref, kc_ref, vc_ref, cos_ref, sa_ref, sb_ref, o_ref,
                kr_ref, vb_ref, *, seq, l):
    scale = HEAD_DIM ** -0.5
    kvh = pl.program_id(1)
    nq = seq // CHUNK
    band = 3 * CHUNK
    rows = WIN_GROUP * CHUNK
    kr_ref[...] = _rope(k_ref[...], cos_ref[...], sa_ref[...], sb_ref[...]).astype(BF16)
    vb_ref[...] = v_ref[...].astype(BF16)
    kc = kc_ref[...].astype(BF16)
    vc = vc_ref[...].astype(BF16)
    rg = lax.broadcasted_iota(I32, (rows, 1), 0) // CHUNK
    sink = jnp.zeros((rows, 1), F32)
    for g in range(WIN_GROUP):
        sink = jnp.where(rg == g, sink_ref[l * WIN_HEADS + kvh * WIN_GROUP + g], sink)
    qoff = lax.broadcasted_iota(I32, (rows, band), 0) % CHUNK
    koff = lax.broadcasted_iota(I32, (rows, band), 1)

    def body(n, carry):
        r0 = pl.multiple_of(n * CHUNK, CHUNK)
        cos = cos_ref[pl.ds(r0, CHUNK), :]
        sa = sa_ref[pl.ds(r0, CHUNK), :]
        sb = sb_ref[pl.ds(r0, CHUNK), :]
        qs = jnp.concatenate(
            [_rope(q_ref[pl.ds(r0, CHUNK), g * HEAD_DIM:(g + 1) * HEAD_DIM], cos, sa, sb)
             for g in range(WIN_GROUP)], axis=0).astype(BF16)
        start = pl.multiple_of(jnp.clip((n - 1) * CHUNK, 0, seq - band), CHUNK)
        kb = kr_ref[pl.ds(start, band), :]
        vb = vb_ref[pl.ds(start, band), :]
        s_loc = _dot_nt(qs, kb) * scale
        ok = jnp.abs(r0 + qoff - (start + koff)) <= WINDOW
        s_loc = jnp.where(ok, s_loc, -jnp.inf)
        s_ctx = _dot_nt(qs, kc) * scale
        m = jnp.maximum(jnp.maximum(jnp.max(s_loc, -1, keepdims=True), jnp.max(s_ctx, -1, keepdims=True)), sink)
        e_loc = jnp.exp(s_loc - m)
        e_ctx = jnp.exp(s_ctx - m)
        den = jnp.sum(e_loc, -1, keepdims=True) + jnp.sum(e_ctx, -1, keepdims=True) + jnp.exp(sink - m)
        o = (_dot(e_ctx.astype(BF16), vc) + _dot(e_loc.astype(BF16), vb)) / den
        for g in range(WIN_GROUP):
            o_ref[pl.ds(r0, CHUNK), g * HEAD_DIM:(g + 1) * HEAD_DIM] = o[g * CHUNK:(g + 1) * CHUNK, :]
        return carry

    lax.fori_loop(0, nq, body, 0)


def _win_attention(u1, cache_k, cache_v, win_sink, rope, seq, l):
    tp = u1.shape[0]
    nb = tp // seq
    past = cache_k.shape[2]
    gw = WIN_GROUP * HEAD_DIM
    ck = cache_k.reshape(nb, DEPTH, past, WIN_KV_HEADS * HEAD_DIM)
    cv = cache_v.reshape(nb, DEPTH, past, WIN_KV_HEADS * HEAD_DIM)
    rope_spec = pl.BlockSpec((seq, HEAD_DIM), lambda b, h, *_: (0, 0))
    return pl.pallas_call(
        functools.partial(_win_kernel, seq=seq, l=l),
        grid_spec=pltpu.PrefetchScalarGridSpec(
            num_scalar_prefetch=1,
            grid=(nb, WIN_KV_HEADS),
            in_specs=[pl.BlockSpec((seq, gw), lambda b, h, *_: (b, h)),
                      pl.BlockSpec((seq, HEAD_DIM), lambda b, h, *_: (b, 1024 // HEAD_DIM + h)),
                      pl.BlockSpec((seq, HEAD_DIM), lambda b, h, *_: (b, 1280 // HEAD_DIM + h)),
                      pl.BlockSpec((None, None, past, HEAD_DIM), lambda b, h, *_: (b, l, 0, h)),
                      pl.BlockSpec((None, None, past, HEAD_DIM), lambda b, h, *_: (b, l, 0, h)),
                      rope_spec, rope_spec, rope_spec],
            out_specs=pl.BlockSpec((seq, gw), lambda b, h, *_: (b, h)),
            scratch_shapes=[pltpu.VMEM((seq, HEAD_DIM), BF16), pltpu.VMEM((seq, HEAD_DIM), BF16)]),
        out_shape=jax.ShapeDtypeStruct((tp, WIN_HEADS * HEAD_DIM), F32),
        compiler_params=_cp("arbitrary", "arbitrary"),
        name="win_attention",
    )(win_sink.reshape(-1), u1, u1, u1, ck, cv, *rope)


def _na_window_start(j, rows):
    kr = min(NA_ROWS, rows)
    rs = jnp.clip(2 * j - kr // 2, 0, rows - kr)
    return jnp.minimum(rs, rows - NA_WIN_ROWS)


def _na_kernel(q_ref, k_ref, v_ref, kc_ref, vc_ref, bias_ref, o_ref, kb_ref, vb_ref, *, seq):
    scale = HEAD_DIM ** -0.5
    rows = seq // GRID_W
    nq = seq // CHUNK
    win = NA_WIN_ROWS * GRID_W
    kb_ref[...] = k_ref[...].astype(BF16)
    vb_ref[...] = v_ref[...].astype(BF16)
    kc = kc_ref[...].astype(BF16)
    vc = vc_ref[...].astype(BF16)

    def body(j, carry):
        r0 = pl.multiple_of(j * CHUNK, CHUNK)
        q = q_ref[pl.ds(r0, CHUNK), :].astype(BF16)
        start = pl.multiple_of(_na_window_start(j, rows) * GRID_W, GRID_W)
        kw = kb_ref[pl.ds(start, win), :]
        vw = vb_ref[pl.ds(start, win), :]
        s_loc = _dot_nt(q, kw) * scale + bias_ref[j]
        s_ctx = _dot_nt(q, kc) * scale
        m = jnp.maximum(jnp.max(s_loc, -1, keepdims=True), jnp.max(s_ctx, -1, keepdims=True))
        e_loc = jnp.exp(s_loc - m)
        e_ctx = jnp.exp(s_ctx - m)
        den = jnp.sum(e_loc, -1, keepdims=True) + jnp.sum(e_ctx, -1, keepdims=True)
        o_ref[pl.ds(r0, CHUNK), :] = (_dot(e_ctx.astype(BF16), vc) + _dot(e_loc.astype(BF16), vw)) / den
        return carry

    lax.fori_loop(0, nq, body, 0)


def _na_bias_table(rpb, seq):
    rows = seq // GRID_W
    kr = min(NA_ROWS, rows)
    nq = seq // CHUNK
    win = NA_WIN_ROWS * GRID_W
    j = np.arange(nq)[:, None, None]
    i = np.arange(CHUNK)[None, :, None]
    m = np.arange(win)[None, None, :]
    r = 2 * j + i // GRID_W
    qc = i % GRID_W
    ws = np.minimum(np.clip(2 * j - kr // 2, 0, rows - kr), rows - NA_WIN_ROWS)
    krow = ws + m // GRID_W
    kc = m % GRID_W
    rs = np.clip(r - kr // 2, 0, rows - kr)
    cs = np.clip(qc - NA_COLS // 2, 0, GRID_W - NA_COLS)
    valid = (krow >= rs) & (krow < rs + kr) & (kc >= cs) & (kc < cs + NA_COLS)
    dr = np.clip(krow - r + (NA_ROWS - 1), 0, 2 * NA_ROWS - 2)
    dc = np.clip(kc - qc, 1 - NA_COLS, NA_COLS - 1) + (NA_COLS - 1)
    dr, dc, valid = np.broadcast_arrays(dr, dc, valid)
    bias = rpb[:, dr, dc]
    return jnp.where(valid[None], bias, NEG_BIG)


def _na_attention(u2, cache_k, cache_v, bias, seq, l):
    tp = u2.shape[0]
    nb = tp // seq
    past = cache_k.shape[2]
    nq = seq // CHUNK
    win = NA_WIN_ROWS * GRID_W
    ck = cache_k.reshape(nb, DEPTH, past, NA_HEADS * HEAD_DIM)
    cv = cache_v.reshape(nb, DEPTH, past, NA_HEADS * HEAD_DIM)
    return pl.pallas_call(
        functools.partial(_na_kernel, seq=seq),
        grid=(nb, NA_HEADS),
        in_specs=[pl.BlockSpec((seq, HEAD_DIM), lambda b, h: (b, 3072 // HEAD_DIM + h)),
                  pl.BlockSpec((seq, HEAD_DIM), lambda b, h: (b, 4096 // HEAD_DIM + h)),
                  pl.BlockSpec((seq, HEAD_DIM), lambda b, h: (b, 5120 // HEAD_DIM + h)),
                  pl.BlockSpec((None, None, past, HEAD_DIM), lambda b, h: (b, l, 0, h)),
                  pl.BlockSpec((None, None, past, HEAD_DIM), lambda b, h: (b, l, 0, h)),
                  pl.BlockSpec((None, nq, CHUNK, win), lambda b, h: (h, 0, 0, 0))],
        out_specs=pl.BlockSpec((seq, HEAD_DIM), lambda b, h: (b, h)),
        out_shape=jax.ShapeDtypeStruct((tp, NA_HEADS * HEAD_DIM), F32),
        scratch_shapes=[pltpu.VMEM((seq, HEAD_DIM), BF16), pltpu.VMEM((seq, HEAD_DIM), BF16)],
        compiler_params=_cp("arbitrary", "arbitrary"),
        name="na_attention",
    )(u2, u2, u2, ck, cv, bias)


def _merge_kernel(ya_ref, ys_ref, yr_ref, yn_ref, g0_ref, g1_ref, g2_ref, g3_ref, w_ref, o_ref, wbf):
    @pl.when(pl.program_id(1) == 0)
    def _():
        wbf[...] = w_ref[...].astype(BF16)

    acc = None
    for n, (b_ref, g_ref) in enumerate(((ya_ref, g0_ref), (ys_ref, g1_ref), (yr_ref, g2_ref), (yn_ref, g3_ref))):
        t = jax.nn.sigmoid(g_ref[...]) * _dot(b_ref[...].astype(BF16), wbf[n])
        acc = t if acc is None else acc + t
    o_ref[...] = acc.astype(BF16)


def _merge(ya, ys, yr, yn, u2, w_branch, l):
    tp = ya.shape[0]
    tm, tn = 256, 512
    g_off = 6144 // tn
    per = D_MODEL // tn
    br = pl.BlockSpec((tm, BRANCH_W), lambda j, i: (i, 0))
    gate = lambda n: pl.BlockSpec((tm, tn), lambda j, i: (i, g_off + n * per + j))
    return pl.pallas_call(
        _merge_kernel,
        grid=(per, tp // tm),
        in_specs=[br, br, br, br, gate(0), gate(1), gate(2), gate(3),
                  pl.BlockSpec((None, N_BRANCH, BRANCH_W, tn), lambda j, i: (l, 0, 0, j))],
        out_specs=pl.BlockSpec((tm, tn), lambda j, i: (i, j)),
        out_shape=jax.ShapeDtypeStruct((tp, D_MODEL), BF16),
        scratch_shapes=[pltpu.VMEM((N_BRANCH, BRANCH_W, tn), BF16)],
        compiler_params=_cp("arbitrary", "arbitrary"),
        name="merge",
    )(ya, ys, yr, yn, u2, u2, u2, u2, w_branch)


def _out_kernel(m_ref, w_ref, x_ref, mod_ref, g_ref, b_ref, o_ref):
    mix = _dot(m_ref[...], w_ref[...])
    y = ALPHA * x_ref[...] + mod_ref[2:3, :] * mix
    o_ref[...] = _layer_norm(y, g_ref[...], b_ref[...])


def _out_proj(merged, w_out_bf, x, mod, ln_g, ln_b, l, row_fn):
    tp = x.shape[0]
    tm = TOK_TILE
    row = pl.BlockSpec((tm, D_MODEL), lambda i: (i, 0))
    vec = pl.BlockSpec((None, 1, D_MODEL), lambda i: (l, 0, 0))
    return pl.pallas_call(
        _out_kernel,
        grid=(tp // tm,),
        in_specs=[row,
                  pl.BlockSpec((None, D_MODEL, D_MODEL), lambda i: (l, 0, 0)),
                  row,
                  pl.BlockSpec((None, None, 6, D_MODEL), lambda i: (l, row_fn(i), 0, 0)),
                  vec, vec],
        out_specs=row,
        out_shape=jax.ShapeDtypeStruct((tp, D_MODEL), F32),
        compiler_params=_cp("arbitrary"),
        name="out_proj_ln1",
    )(merged, w_out_bf, x, mod, ln_g.reshape(DEPTH, 1, D_MODEL), ln_b.reshape(DEPTH, 1, D_MODEL))


def _router_kernel(xc_ref, xl_ref, mod_ref, rw_ref, rb_ref, xm_ref, idx_ref, wt_ref, *, n_ctx_tiles):
    i = pl.program_id(0)
    x = jnp.where(i < n_ctx_tiles, xc_ref[...], xl_ref[...])
    xm = x * (1.0 + mod_ref[4:5, :]) + mod_ref[3:4, :]
    xm_ref[...] = xm
    tm = xm.shape[0]
    scores = jax.nn.sigmoid(_dot_nt(rw_ref[...], xm.astype(BF16)))
    biased = scores + rb_ref[...]
    per = N_EXPERTS // N_EXPERT_GROUPS
    sub = lax.broadcasted_iota(I32, (per, tm), 0)
    bg = [biased[per * g:per * (g + 1), :] for g in range(N_EXPERT_GROUPS)]
    sg = [scores[per * g:per * (g + 1), :] for g in range(N_EXPERT_GROUPS)]
    gs = []
    for g in range(N_EXPERT_GROUPS):
        m1 = jnp.max(bg[g], 0, keepdims=True)
        i1 = jnp.min(jnp.where(bg[g] == m1, sub, per), 0, keepdims=True)
        m2 = jnp.max(jnp.where(sub == i1, -jnp.inf, bg[g]), 0, keepdims=True)
        gs.append(m1 + m2)
    cand = []
    for g in range(N_EXPERT_GROUPS):
        rank = jnp.zeros((1, tm), I32)
        for g2 in range(N_EXPERT_GROUPS):
            if g2 == g:
                continue
            ahead = (gs[g2] > gs[g]) | ((gs[g2] == gs[g]) & (g2 < g))
            rank = rank + ahead.astype(I32)
        cand.append(jnp.where(rank < TOPK_GROUPS, bg[g], -jnp.inf))
    eidx = [sub + per * g for g in range(N_EXPERT_GROUPS)]
    idx_rows, w_rows = [], []
    for _ in range(TOP_K):
        m = None
        for g in range(N_EXPERT_GROUPS):
            t = jnp.max(cand[g], 0, keepdims=True)
            m = t if m is None else jnp.maximum(m, t)
        ik = None
        for g in range(N_EXPERT_GROUPS):
            t = jnp.min(jnp.where(cand[g] == m, eidx[g], N_EXPERTS), 0, keepdims=True)
            ik = t if ik is None else jnp.minimum(ik, t)
        wk = jnp.zeros((1, tm), F32)
        for g in range(N_EXPERT_GROUPS):
            hit = eidx[g] == ik
            wk = wk + jnp.sum(jnp.where(hit, sg[g], 0.0), 0, keepdims=True)
            cand[g] = jnp.where(hit, -jnp.inf, cand[g])
        idx_rows.append(ik)
        w_rows.append(wk)
    tot = w_rows[0]
    for wk in w_rows[1:]:
        tot = tot + wk
    idx_ref[...] = jnp.concatenate(idx_rows, axis=0)
    wt_ref[...] = jnp.concatenate([wk / tot * ROUTED_SCALE for wk in w_rows], axis=0)


def _router(x1c, x1l, mod, rwt_bf, rbias, l, n_lat_tiles_per_seq):
    tc, tl = x1c.shape[0], x1l.shape[0]
    tm = TOK_TILE
    nct, nlt = tc // tm, tl // tm
    t = tc + tl
    row_fn = lambda i: jnp.where(i < nct, 0, 1 + (i - nct) // n_lat_tiles_per_seq)
    return pl.pallas_call(
        functools.partial(_router_kernel, n_ctx_tiles=nct),
        grid=(nct + nlt,),
        in_specs=[pl.BlockSpec((tm, D_MODEL), lambda i: (jnp.minimum(i, nct - 1), 0)),
                  pl.BlockSpec((tm, D_MODEL), lambda i: (jnp.maximum(i - nct, 0), 0)),
                  pl.BlockSpec((None, None, 6, D_MODEL), lambda i: (l, row_fn(i), 0, 0)),
                  pl.BlockSpec((N_EXPERTS, D_MODEL), lambda i: (0, 0)),
                  pl.BlockSpec((N_EXPERTS, 1), lambda i: (0, 0))],
        out_specs=[pl.BlockSpec((tm, D_MODEL), lambda i: (i, 0)),
                   pl.BlockSpec((TOP_K, tm), lambda i: (0, i)),
                   pl.BlockSpec((TOP_K, tm), lambda i: (0, i))],
        out_shape=[jax.ShapeDtypeStruct((t, D_MODEL), F32),
                   jax.ShapeDtypeStruct((TOP_K, t), I32),
                   jax.ShapeDtypeStruct((TOP_K, t), F32)],
        compiler_params=_cp("arbitrary"),
        name="router",
    )(x1c, x1l, mod, rwt_bf, rbias.reshape(N_EXPERTS, 1))


def _row_copy(src_hbm, src_row, dst, dst_row, sem):
    return pltpu.make_async_copy(src_hbm.at[pl.ds(src_row, 1), :], dst.at[pl.ds(dst_row, 1), :], sem)


def _experts_kernel(be_ref, nu_ref, tok_ref, xm_hbm, wgu_ref, wdn_ref, y_ref, xbuf, wgu_bf, wdn_bf, sem):
    i = pl.program_id(0)

    @pl.when(i < nu_ref[0])
    def _():
        def issue(r, carry):
            _row_copy(xm_hbm, tok_ref[0, r], xbuf, r, sem).start()
            return carry

        lax.fori_loop(0, MOE_BM, issue, 0)

        prev = be_ref[jnp.maximum(i - 1, 0)]

        @pl.when((i == 0) | (be_ref[i] != prev))
        def _():
            wgu_bf[...] = wgu_ref[...].astype(BF16)
            wdn_bf[...] = wdn_ref[...].astype(BF16)

        def drain(r, carry):
            _row_copy(xm_hbm, 0, xbuf, r, sem).wait()
            return carry

        lax.fori_loop(0, MOE_BM, drain, 0)

        hgu = _dot(xbuf[...].astype(BF16), wgu_bf[...])
        act = (_silu(hgu[:, :EXPERT_FF]) * hgu[:, EXPERT_FF:]).astype(BF16)
        y_ref[...] = _dot(act, wdn_bf[...])

    @pl.when(i >= nu_ref[0])
    def _():
        y_ref[...] = jnp.zeros(y_ref.shape, F32)


def _experts(xm, slot_tok, blk_e, n_used, w_gu, w_dn, l):
    nblk = blk_e.shape[0]
    last = lambda i, be, nu: jnp.minimum(i, nu[0] - 1)
    return pl.pallas_call(
        _experts_kernel,
        grid_spec=pltpu.PrefetchScalarGridSpec(
            num_scalar_prefetch=2,
            grid=(nblk,),
            in_specs=[pl.BlockSpec((None, 1, MOE_BM), lambda i, be, nu: (last(i, be, nu), 0, 0),
                                   memory_space=pltpu.SMEM),
                      pl.BlockSpec(memory_space=pl.ANY),
                      pl.BlockSpec((None, None, D_MODEL, 2 * EXPERT_FF),
                                   lambda i, be, nu: (l, be[last(i, be, nu)], 0, 0)),
                      pl.BlockSpec((None, None, EXPERT_FF, D_MODEL),
                                   lambda i, be, nu: (l, be[last(i, be, nu)], 0, 0))],
            out_specs=pl.BlockSpec((MOE_BM, D_MODEL), lambda i, be, nu: (i, 0)),
            scratch_shapes=[pltpu.VMEM((MOE_BM, D_MODEL), F32),
                            pltpu.VMEM((D_MODEL, 2 * EXPERT_FF), BF16),
                            pltpu.VMEM((EXPERT_FF, D_MODEL), BF16),
                            pltpu.SemaphoreType.DMA(())]),
        out_shape=jax.ShapeDtypeStruct((nblk * MOE_BM, D_MODEL), F32),
        compiler_params=_cp("arbitrary"),
        name="experts",
    )(blk_e, n_used, slot_tok.reshape(nblk, 1, MOE_BM), xm, w_gu, w_dn)


FIN_TM = 128


def _final_kernel(dst_ref, ys_hbm, x1_ref, xm_ref, wk_ref, mod_ref, g_ref, b_ref, sgu_ref, sdn_ref, o_ref,
                  gbuf, sem):
    n = TOP_K * FIN_TM

    def issue(s, carry):
        k = s // FIN_TM
        r = s - k * FIN_TM
        pltpu.make_async_copy(ys_hbm.at[pl.ds(dst_ref[0, s], 1), :], gbuf.at[k, pl.ds(r, 1), :], sem).start()
        return carry

    lax.fori_loop(0, n, issue, 0)

    xb = xm_ref[...].astype(BF16)
    hgu = _dot(xb, sgu_ref[...])
    act = (_silu(hgu[:, :SHARED_FF]) * hgu[:, SHARED_FF:]).astype(BF16)
    shared = _dot(act, sdn_ref[...])

    def drain(s, carry):
        k = s // FIN_TM
        r = s - k * FIN_TM
        pltpu.make_async_copy(ys_hbm.at[pl.ds(0, 1), :], gbuf.at[k, pl.ds(r, 1), :], sem).wait()
        return carry

    lax.fori_loop(0, n, drain, 0)

    routed = wk_ref[:, 0:1] * gbuf[0]
    for k in range(1, TOP_K):
        routed = routed + wk_ref[:, k:k + 1] * gbuf[k]
    y = ALPHA * x1_ref[...] + mod_ref[5:6, :] * (routed + shared)
    o_ref[...] = _layer_norm(y, g_ref[...], b_ref[...])


def _final(dest, ys, x1, xm, wk, mod, ln_g, ln_b, sgu_bf, sdn_bf, l, row_fn, tok_off):
    tp = x1.shape[0]
    tm = FIN_TM
    off = tok_off // tm
    row = pl.BlockSpec((tm, D_MODEL), lambda i: (i, 0))
    vec = pl.BlockSpec((None, 1, D_MODEL), lambda i: (l, 0, 0))
    return pl.pallas_call(
        _final_kernel,
        grid=(tp // tm,),
        in_specs=[pl.BlockSpec((None, 1, TOP_K * tm), lambda i: (off + i, 0, 0), memory_space=pltpu.SMEM),
                  pl.BlockSpec(memory_space=pl.ANY),
                  row,
                  pl.BlockSpec((tm, D_MODEL), lambda i: (off + i, 0)),
                  pl.BlockSpec((tm, TOP_K), lambda i: (off + i, 0)),
                  pl.BlockSpec((None, None, 6, D_MODEL), lambda i: (l, row_fn(i * tm // TOK_TILE), 0, 0)),
                  vec, vec,
                  pl.BlockSpec((None, D_MODEL, 2 * SHARED_FF), lambda i: (l, 0, 0)),
                  pl.BlockSpec((None, SHARED_FF, D_MODEL), lambda i: (l, 0, 0))],
        out_specs=row,
        out_shape=jax.ShapeDtypeStruct((tp, D_MODEL), F32),
        scratch_shapes=[pltpu.VMEM((TOP_K, tm, D_MODEL), F32), pltpu.SemaphoreType.DMA(())],
        compiler_params=_cp("arbitrary"),
        name="combine_ln2",
    )(dest, ys, x1, xm, wk, mod, ln_g.reshape(DEPTH, 1, D_MODEL), ln_b.reshape(DEPTH, 1, D_MODEL),
      sgu_bf, sdn_bf)


def _dispatch_plan(idx_t):
    t = idx_t.shape[1]
    idx = idx_t.T
    member = jnp.sum((idx[:, :, None] == jnp.arange(N_EXPERTS, dtype=I32)).astype(I32), axis=1)
    counts = jnp.sum(member, axis=0)
    before = jnp.cumsum(member, axis=0) - member
    pos = jnp.take_along_axis(before, idx, axis=1)
    padded = (counts + MOE_BM - 1) // MOE_BM * MOE_BM
    pad_end = jnp.cumsum(padded)
    pad_start = pad_end - padded
    dest = pad_start[idx] + pos
    nblk = t * TOP_K // MOE_BM + N_EXPERTS
    tok = jnp.broadcast_to(jnp.arange(t, dtype=I32)[:, None], (t, TOP_K))
    slot_tok = jnp.zeros((nblk * MOE_BM,), I32).at[dest.reshape(-1)].set(tok.reshape(-1))
    blk_e = jnp.minimum(jnp.searchsorted(pad_end, jnp.arange(nblk, dtype=I32) * MOE_BM, side='right'),
                        N_EXPERTS - 1).astype(I32)
    n_used = (pad_end[-1] // MOE_BM).astype(I32).reshape(1)
    dest_tiles = dest.reshape(t // FIN_TM, FIN_TM, TOP_K).transpose(0, 2, 1).reshape(t // FIN_TM, 1, TOP_K * FIN_TM)
    return slot_tok, blk_e, n_used, dest_tiles.astype(I32)


def _rope_tables(n_tok):
    t = jnp.arange(n_tok)
    row = (t // GRID_W).astype(F32)
    col = (t % GRID_W).astype(F32)
    half = HEAD_DIM // 2
    inv = ROPE_BASE ** (-jnp.arange(0, half, 2, dtype=F32) / half)
    ar = row[:, None] * inv
    ac = col[:, None] * inv
    ang = jnp.concatenate([ar, ar, ac, ac], -1)
    cos, sin = jnp.cos(ang), jnp.sin(ang)
    quarter = (jnp.arange(HEAD_DIM) // (HEAD_DIM // 4)) % 2
    sin_a = jnp.where(quarter == 0, -sin, 0.0)
    sin_b = jnp.where(quarter == 1, sin, 0.0)
    return cos, sin_a, sin_b


def kernel(x_prompt, x_sample, cache_win_k, cache_win_v, state_ssm, state_ret, cache_na_k, cache_na_v, c, c_ctx, w_ada, b_ada, w_in, win_sink, conv_w, conv_b, dt_bias, a_log, d_skip, ssm_norm, ret_decay, na_rpb, w_branch, w_out, ln1_g, ln1_b, router_w, router_bias, exp_w_gu, exp_w_down, sh_w_gu, sh_w_down, ln2_g, ln2_b):
    bc, lc, _ = x_prompt.shape
    bl, ll, _ = x_sample.shape
    assert lc == TOK_TILE and ll % TOK_TILE == 0
    tc, tl = bc * lc, bl * ll
    lat_tiles = ll // TOK_TILE

    nr = -(-(1 + bl) // 8) * 8
    cond = jnp.zeros((nr, D_MODEL), F32).at[0].set(c_ctx).at[1:1 + bl].set(c)
    mod = _ada(cond, w_ada, b_ada).reshape(DEPTH, nr, 6, D_MODEL)
    row_ctx = lambda i: 0
    row_lat = lambda i: 1 + i // lat_tiles

    rope = _rope_tables(ll)
    perm = np.array([d * SSM_HEADS + g * 8 + hh for g in range(SSM_GROUPS) for d in range(2) for hh in range(8)])
    dsk_full = jnp.repeat(d_skip, SSM_HEAD_DIM, axis=1)
    w_in2 = w_in[:, :, U2_OFF:].astype(BF16)
    w_dt = w_in[:, :, DT_OFF:U2_OFF][:, :, perm].astype(BF16)
    w_out_bf = w_out.astype(BF16)
    rwt_bf = jnp.swapaxes(router_w, 1, 2).astype(BF16)
    sgu_bf = sh_w_gu.astype(BF16)
    sdn_bf = sh_w_down.astype(BF16)

    xs = {"ctx": x_prompt.reshape(tc, D_MODEL), "lat": x_sample.reshape(tl, D_MODEL)}
    new_ctx = []
    for l in range(DEPTH):
        x1 = {}
        bias_tab = _na_bias_table(na_rpb[l], ll)
        for path in ("ctx", "lat"):
            latent = path == "lat"
            x = xs[path]
            seq = ll if latent else lc
            row_fn = row_lat if latent else row_ctx
            h = _modulate(x, mod, l, row_fn)
            u1 = _in_proj(h, w_in, l, U1_COLS)
            u2 = _in_proj(h, w_in2, l, U2_COLS)
            dt, la, lat_t = _ssd_steps(h, w_dt[l], dt_bias[l].reshape(-1)[perm], a_log[l].reshape(-1)[perm])
            xc = _conv(u1, conv_w[l], conv_b[l], seq)
            ys, ssm_fin = _ssd(xc, u1, dt, la, lat_t, dsk_full[l:l + 1], ssm_norm[l:l + 1], seq, l,
                               state_ssm if latent else None)
            yr, ret_fin = _retention(u2, ret_decay, seq, l, rope if latent else None,
                                     state_ret if latent else None)
            if latent:
                ya = _win_attention(u1, cache_win_k, cache_win_v, win_sink, rope, seq, l)
                yn = _na_attention(u2, cache_na_k, cache_na_v, bias_tab, seq, l)
            else:
                ya, yn = _ctx_attention(u1, u2, win_sink, seq, l)
                kv = lambda t, nh: t.reshape(bc, lc, nh, HEAD_DIM)
                new_ctx.append((kv(u1[:, 1024:1280], WIN_KV_HEADS), kv(u1[:, 1280:1536], WIN_KV_HEADS),
                                ssm_fin, ret_fin,
                                kv(u2[:, 4096:5120], NA_HEADS), kv(u2[:, 5120:6144], NA_HEADS)))
            merged = _merge(ya, ys, yr, yn, u2, w_branch, l)
            x1[path] = _out_proj(merged, w_out_bf, x, mod, ln1_g, ln1_b, l, row_fn)

        xm, idx_t, w_t = _router(x1["ctx"], x1["lat"], mod, rwt_bf[l], router_bias[l], l, lat_tiles)
        slot_tok, blk_e, n_used, dest_tiles = _dispatch_plan(idx_t)
        y_sorted = _experts(xm, slot_tok, blk_e, n_used, exp_w_gu, exp_w_down, l)
        wk = w_t.T
        for path, off, row_fn in (("ctx", 0, row_ctx), ("lat", tc, row_lat)):
            xs[path] = _final(dest_tiles, y_sorted, x1[path], xm, wk, mod, ln2_g, ln2_b, sgu_bf, sdn_bf,
                              l, row_fn, off)

    stack = lambda i: jnp.stack([t[i] for t in new_ctx], axis=1)
    return (xs["ctx"].reshape(bc, lc, D_MODEL), xs["lat"].reshape(bl, ll, D_MODEL),
            stack(0), stack(1), stack(2), stack(3), stack(4), stack(5))
```

```python
import functools

import jax
import jax.numpy as jnp
import numpy as np
from jax import lax
from jax.experimental import pallas as pl
from jax.experimental.pallas import tpu as pltpu

F32 = jnp.float32
BF16 = jnp.bfloat16
I32 = jnp.int32

D_MODEL = 2048
DEPTH = 2
GRID_W = 64
HEAD_DIM = 128
ROPE_BASE = 10000.0
CHUNK = 128
WIN_HEADS = 8
WIN_KV_HEADS = 2
WIN_GROUP = WIN_HEADS // WIN_KV_HEADS
WINDOW = 128
SSM_D_INNER = D_MODEL // 2
SSM_HEAD_DIM = 64
SSM_HEADS = SSM_D_INNER // SSM_HEAD_DIM
SSM_GROUPS = 2
SSM_STATE = 128
SSM_CONV = 7
SSM_CONV_DIM = SSM_D_INNER + 2 * SSM_GROUPS * SSM_STATE
RET_HEADS = 4
RET_QK_DIM = 128
RET_V_DIM = 256
NA_HEADS = 8
NA_ROWS = 8
NA_COLS = 16
N_BRANCH = 4
BRANCH_W = D_MODEL // 2
N_EXPERTS = 64
TOP_K = 8
N_EXPERT_GROUPS = 8
TOPK_GROUPS = 4
EXPERT_FF = 512
SHARED_FF = 512
ROUTED_SCALE = 2.5
ALPHA = (2.0 * DEPTH) ** 0.25
LN_EPS = 1e-5
RMS_EPS = 1e-6

U1_COLS = 4096
DT_OFF = 4096
U2_OFF = DT_OFF + 2 * SSM_HEADS
U2_COLS = 14336

TOK_TILE = 256
MOE_BM = 256
NA_WIN_ROWS = 10
NEG_BIG = -1e30

VMEM_LIMIT = 52 * 1024 * 1024


def _cp(*sem):
    return pltpu.CompilerParams(dimension_semantics=sem, vmem_limit_bytes=VMEM_LIMIT)


def _dot(a, b):
    return jnp.dot(a, b, preferred_element_type=F32)


def _dot_nt(a, b):
    return lax.dot_general(a, b, (((1,), (1,)), ((), ())), preferred_element_type=F32)


def _dot_exact(a, b):
    return jnp.dot(a, b, preferred_element_type=F32, precision=lax.Precision.HIGHEST)


def _silu(x):
    return x * jax.nn.sigmoid(x)


def _softplus(x):
    return jnp.maximum(x, 0.0) + jnp.log1p(jnp.exp(-jnp.abs(x)))


def _rope(x, cos, sin_a, sin_b):
    return x * cos + pltpu.roll(x, 96, 1) * sin_a + pltpu.roll(x, 32, 1) * sin_b


def _layer_norm(y, g, b):
    mu = jnp.mean(y, -1, keepdims=True)
    yc = y - mu
    var = jnp.mean(yc * yc, -1, keepdims=True)
    return yc * lax.rsqrt(var + LN_EPS) * g + b


def _ada_kernel(c_ref, w_ref, b_ref, o_ref):
    a = _silu(c_ref[...]).astype(BF16)
    o_ref[...] = _dot(a, w_ref[...].astype(BF16)) + b_ref[...]


def _ada(cond, w_ada, b_ada):
    nl, _, nout = w_ada.shape
    nr = cond.shape[0]
    tn = 1024
    return pl.pallas_call(
        _ada_kernel,
        grid=(nl, nout // tn),
        in_specs=[pl.BlockSpec((nr, D_MODEL), lambda l, j: (0, 0)),
                  pl.BlockSpec((None, D_MODEL, tn), lambda l, j: (l, 0, j)),
                  pl.BlockSpec((None, 1, tn), lambda l, j: (l, 0, j))],
        out_specs=pl.BlockSpec((None, nr, tn), lambda l, j: (l, 0, j)),
        out_shape=jax.ShapeDtypeStruct((nl, nr, nout), F32),
        compiler_params=_cp("arbitrary", "arbitrary"),
        name="ada",
    )(cond, w_ada, b_ada.reshape(nl, 1, nout))


def _mod_kernel(x_ref, m_ref, o_ref):
    o_ref[...] = (x_ref[...] * (1.0 + m_ref[1:2, :]) + m_ref[0:1, :]).astype(BF16)


def _modulate(x, mod, l, row_fn):
    tp = x.shape[0]
    return pl.pallas_call(
        _mod_kernel,
        grid=(tp // TOK_TILE,),
        in_specs=[pl.BlockSpec((TOK_TILE, D_MODEL), lambda i: (i, 0)),
                  pl.BlockSpec((None, None, 6, D_MODEL), lambda i: (l, row_fn(i), 0, 0))],
        out_specs=pl.BlockSpec((TOK_TILE, D_MODEL), lambda i: (i, 0)),
        out_shape=jax.ShapeDtypeStruct((tp, D_MODEL), BF16),
        compiler_params=_cp("arbitrary"),
        name="modulate",
    )(x, mod)


def _mm_kernel(x_ref, w_ref, o_ref, wbf):
    @pl.when(pl.program_id(1) == 0)
    def _():
        wbf[...] = w_ref[...].astype(BF16)

    o_ref[...] = _dot(x_ref[...], wbf[...])


def _in_proj(h, w, l, ncols):
    tp = h.shape[0]
    tm, tn = 512, 1024
    return pl.pallas_call(
        _mm_kernel,
        grid=(ncols // tn, tp // tm),
        in_specs=[pl.BlockSpec((tm, D_MODEL), lambda j, i: (i, 0)),
                  pl.BlockSpec((None, D_MODEL, tn), lambda j, i: (l, 0, j))],
        out_specs=pl.BlockSpec((tm, tn), lambda j, i: (i, j)),
        out_shape=jax.ShapeDtypeStruct((tp, ncols), F32),
        scratch_shapes=[pltpu.VMEM((D_MODEL, tn), BF16)],
        compiler_params=_cp("arbitrary", "arbitrary"),
        name="in_proj",
    )(h, w)


def _dt_kernel(h_ref, w_ref, wt_ref, b_ref, a_ref, bt_ref, at_ref, dt_ref, la_ref, lat_ref):
    h = h_ref[...]
    dt = _softplus(_dot(h, w_ref[...]) + b_ref[...])
    la = dt * -jnp.exp(a_ref[...])
    dtt = _softplus(_dot_nt(wt_ref[...], h) + bt_ref[...])
    lat = dtt * -jnp.exp(at_ref[...])
    for g in range(SSM_GROUPS):
        dt_ref[g] = dt[:, 16 * g:16 * (g + 1)]
        la_ref[g] = la[:, 16 * g:16 * (g + 1)]
        for q in range(h.shape[0] // CHUNK):
            lat_ref[g, q] = lat[16 * g:16 * (g + 1), q * CHUNK:(q + 1) * CHUNK]


def _ssd_steps(h, w_dt, dt_bias, a_log):
    tp = h.shape[0]
    tm = 512
    nq = tm // CHUNK
    return pl.pallas_call(
        _dt_kernel,
        grid=(tp // tm,),
        in_specs=[pl.BlockSpec((tm, D_MODEL), lambda i: (i, 0)),
                  pl.BlockSpec((D_MODEL, 32), lambda i: (0, 0)),
                  pl.BlockSpec((32, D_MODEL), lambda i: (0, 0)),
                  pl.BlockSpec((1, 32), lambda i: (0, 0)),
                  pl.BlockSpec((1, 32), lambda i: (0, 0)),
                  pl.BlockSpec((32, 1), lambda i: (0, 0)),
                  pl.BlockSpec((32, 1), lambda i: (0, 0))],
        out_specs=[pl.BlockSpec((SSM_GROUPS, tm, 16), lambda i: (0, i, 0)),
                   pl.BlockSpec((SSM_GROUPS, tm, 16), lambda i: (0, i, 0)),
                   pl.BlockSpec((SSM_GROUPS, nq, 16, CHUNK), lambda i: (0, i, 0, 0))],
        out_shape=[jax.ShapeDtypeStruct((SSM_GROUPS, tp, 16), F32),
                   jax.ShapeDtypeStruct((SSM_GROUPS, tp, 16), F32),
                   jax.ShapeDtypeStruct((SSM_GROUPS, tp // CHUNK, 16, CHUNK), F32)],
        compiler_params=_cp("arbitrary"),
        name="ssd_steps",
    )(h, w_dt, w_dt.T, dt_bias.reshape(1, 32), a_log.reshape(1, 32),
      dt_bias.reshape(32, 1), a_log.reshape(32, 1))


def _conv_kernel(x_ref, w_ref, b_ref, o_ref, pad_ref, *, seq):
    ct = x_ref.shape[1]
    pad = SSM_CONV // 2
    pad_ref[0:8, :] = jnp.zeros((8, ct), F32)
    pad_ref[8 + seq:16 + seq, :] = jnp.zeros((8, ct), F32)
    pad_ref[8:8 + seq, :] = x_ref[...]
    for r in range(seq // CHUNK):
        acc = jnp.broadcast_to(b_ref[...], (CHUNK, ct))
        for j in range(SSM_CONV):
            s = 8 - pad + j + r * CHUNK
            acc = acc + w_ref[j:j + 1, :] * pad_ref[s:s + CHUNK, :]
        o_ref[r * CHUNK:(r + 1) * CHUNK, :] = _silu(acc)


def _conv(u1, conv_w, conv_b, seq):
    tp = u1.shape[0]
    ct = 512
    c0 = 2560 // ct
    return pl.pallas_call(
        functools.partial(_conv_kernel, seq=seq),
        grid=(tp // seq, SSM_CONV_DIM // ct),
        in_specs=[pl.BlockSpec((seq, ct), lambda b, j: (b, c0 + j)),
                  pl.BlockSpec((SSM_CONV, ct), lambda b, j: (0, j)),
                  pl.BlockSpec((1, ct), lambda b, j: (0, j))],
        out_specs=pl.BlockSpec((seq, ct), lambda b, j: (b, j)),
        out_shape=jax.ShapeDtypeStruct((tp, SSM_CONV_DIM), F32),
        scratch_shapes=[pltpu.VMEM((seq + 16, ct), F32)],
        compiler_params=_cp("arbitrary", "arbitrary"),
        name="conv",
    )(u1, conv_w, conv_b.reshape(1, -1))


def _expand8(a, lane_lt64):
    tiles = [jnp.where(lane_lt64, a[:, 2 * p:2 * p + 1], a[:, 2 * p + 1:2 * p + 2]) for p in range(4)]
    return jnp.concatenate(tiles, axis=1)


def _ssd_kernel(*refs, seq, has_s0, want_fin):
    (xs_ref, bm_ref, cm_ref, z_ref, dt_ref, la_ref, lat_ref, dsk_ref, nrm_ref) = refs[:9]
    pos = 9
    s0_ref = None
    if has_s0:
        s0_ref = refs[pos]
        pos += 1
    y_ref = refs[pos]
    pos += 1
    fin_ref = None
    if want_fin:
        fin_ref = refs[pos]
        pos += 1
    st_ref = refs[pos]

    nc = seq // CHUNK
    hpg = SSM_HEADS // SSM_GROUPS
    ri = lax.broadcasted_iota(I32, (CHUNK, CHUNK), 0)
    ci = lax.broadcasted_iota(I32, (CHUNK, CHUNK), 1)
    lower = ri >= ci
    upper = ri <= ci
    lt = jnp.where(lower, 1.0, 0.0).astype(F32)
    ut = jnp.where(upper, 1.0, 0.0).astype(F32)
    lane_lt64 = lax.broadcasted_iota(I32, (1, 128), 1) < SSM_HEAD_DIM
    half0 = lax.broadcasted_iota(I32, (CHUNK, 128), 1) < SSM_HEAD_DIM

    for d in range(2):
        if has_s0:
            st_ref[d] = jnp.concatenate([s0_ref[d, hh] for hh in range(hpg)], axis=1)
        else:
            st_ref[d] = jnp.zeros((SSM_STATE, hpg * SSM_HEAD_DIM), F32)

    def chunk(c, d):
        r0 = pl.multiple_of(c * CHUNK, CHUNK)
        la_c = la_ref[pl.ds(r0, CHUNK), :][:, 8 * d:8 * d + 8]
        dt_c = dt_ref[pl.ds(r0, CHUNK), :][:, 8 * d:8 * d + 8]
        lat_c = lat_ref[c][8 * d:8 * d + 8, :]
        if d == 0:
            acs = _dot_exact(lt, la_c)
            acst = _dot_exact(lat_c, ut)
            mask = lower
            edge = acs[CHUNK - 1:CHUNK, :]
        else:
            acs = _dot_exact(ut, la_c)
            acst = _dot_exact(lat_c, lt)
            mask = upper
            edge = acs[0:1, :]
        xs_c = xs_ref[pl.ds(r0, CHUNK), :]
        b_c = bm_ref[pl.ds(r0, CHUNK), :]
        c_c = cm_ref[pl.ds(r0, CHUNK), :].astype(BF16)
        g = _dot_nt(c_c, b_c.astype(BF16))
        xdt = xs_c * _expand8(dt_c, lane_lt64)
        tiles = []
        for p in range(hpg // 2):
            xt = xdt[:, 128 * p:128 * (p + 1)]
            acc = None
            for q in range(2):
                hh = 2 * p + q
                dec = jnp.exp(jnp.where(mask, acs[:, hh:hh + 1] - acst[hh:hh + 1, :], -jnp.inf))
                v = jnp.where(half0 if q == 0 else jnp.logical_not(half0), xt, 0.0).astype(BF16)
                t = _dot((g * dec).astype(BF16), v)
                acc = t if acc is None else acc + t
            tiles.append(acc)
        y = jnp.concatenate(tiles, axis=1)
        st = st_ref[d]
        y = y + _expand8(jnp.exp(acs), lane_lt64) * _dot(c_c, st.astype(BF16))
        if d == 0:
            y_ref[pl.ds(r0, CHUNK), :] = y
        else:
            y_ref[pl.ds(r0, CHUNK), :] += y
        wgt = jnp.exp(edge - acs)
        v = (xdt * _expand8(wgt, lane_lt64)).astype(BF16)
        st_ref[d] = st * _expand8(jnp.exp(edge), lane_lt64) + _dot(b_c.T.astype(BF16), v)

    def fwd(c, carry):
        chunk(c, 0)
        return carry

    def bwd(c, carry):
        chunk(nc - 1 - c, 1)
        return carry

    lax.fori_loop(0, nc, fwd, 0)
    lax.fori_loop(0, nc, bwd, 0)

    for r in range(nc):
        sl = slice(r * CHUNK, (r + 1) * CHUNK)
        y = (y_ref[sl, :] + xs_ref[sl, :] * dsk_ref[...]) * _silu(z_ref[sl, :])
        y = y * lax.rsqrt(jnp.mean(y * y, -1, keepdims=True) + RMS_EPS)
        y_ref[sl, :] = y * nrm_ref[...]

    if want_fin:
        for d in range(2):
            st = st_ref[d]
            for hh in range(hpg):
                fin_ref[d, hh] = st[:, hh * SSM_HEAD_DIM:(hh + 1) * SSM_HEAD_DIM]


def _ssd(xc, u1, dt, la, lat, dsk, nrm, seq, l, s0):
    tp = xc.shape[0]
    nb = tp // seq
    nc = seq // CHUNK
    gw = SSM_D_INNER // SSM_GROUPS
    hpg = SSM_HEADS // SSM_GROUPS
    has_s0 = s0 is not None
    want_fin = not has_s0
    in_specs = [pl.BlockSpec((seq, gw), lambda b, g: (b, g)),
                pl.BlockSpec((seq, SSM_STATE), lambda b, g: (b, SSM_D_INNER // SSM_STATE + g)),
                pl.BlockSpec((seq, SSM_STATE), lambda b, g: (b, SSM_D_INNER // SSM_STATE + SSM_GROUPS + g)),
                pl.BlockSpec((seq, gw), lambda b, g: (b, 1536 // gw + g)),
                pl.BlockSpec((None, seq, 16), lambda b, g: (g, b, 0)),
                pl.BlockSpec((None, seq, 16), lambda b, g: (g, b, 0)),
                pl.BlockSpec((None, nc, 16, CHUNK), lambda b, g: (g, b, 0, 0)),
                pl.BlockSpec((1, gw), lambda b, g: (0, g)),
                pl.BlockSpec((1, gw), lambda b, g: (0, g))]
    args = [xc, xc, xc, u1, dt, la, lat, dsk, nrm]
    if has_s0:
        in_specs.append(pl.BlockSpec((None, None, 2, hpg, SSM_STATE, SSM_HEAD_DIM),
                                     lambda b, g: (b, l, 0, g, 0, 0)))
        args.append(s0)
    out_specs = [pl.BlockSpec((seq, gw), lambda b, g: (b, g))]
    out_shape = [jax.ShapeDtypeStruct((tp, SSM_D_INNER), F32)]
    if want_fin:
        out_specs.append(pl.BlockSpec((None, 2, hpg, SSM_STATE, SSM_HEAD_DIM), lambda b, g: (b, 0, g, 0, 0)))
        out_shape.append(jax.ShapeDtypeStruct((nb, 2, SSM_HEADS, SSM_STATE, SSM_HEAD_DIM), F32))
    res = pl.pallas_call(
        functools.partial(_ssd_kernel, seq=seq, has_s0=has_s0, want_fin=want_fin),
        grid=(nb, SSM_GROUPS),
        in_specs=in_specs,
        out_specs=out_specs,
        out_shape=out_shape,
        scratch_shapes=[pltpu.VMEM((2, SSM_STATE, gw), F32)],
        compiler_params=_cp("arbitrary", "arbitrary"),
        name="ssd_scan",
    )(*args)
    return (res[0], res[1]) if want_fin else (res[0], None)


def _ret_kernel(*refs, seq, l, latent):
    dec_ref, q_ref, k_ref, v_ref, g_ref = refs[:5]
    pos = 5
    if latent:
        cos_ref, sa_ref, sb_ref, s0_ref = refs[pos:pos + 4]
        pos += 4
    y_ref = refs[pos]
    pos += 1
    fin_ref = None
    if not latent:
        fin_ref = refs[pos]
        pos += 1
    qs_ref, ks_ref, st_ref = refs[pos:pos + 3]

    nc = seq // CHUNK
    hd = pl.program_id(1)
    if latent:
        qs_ref[...] = _rope(q_ref[...], cos_ref[...], sa_ref[...], sb_ref[...]).astype(BF16)
        ks_ref[...] = _rope(k_ref[...] * (RET_QK_DIM ** -0.5), cos_ref[...], sa_ref[...], sb_ref[...])
        for d in range(2):
            st_ref[d] = s0_ref[d]
    else:
        qs_ref[...] = q_ref[...].astype(BF16)
        ks_ref[...] = k_ref[...] * (RET_QK_DIM ** -0.5)
        for d in range(2):
            st_ref[d] = jnp.zeros((RET_QK_DIM, RET_V_DIM), F32)

    ri = lax.broadcasted_iota(I32, (CHUNK, CHUNK), 0)
    ci = lax.broadcasted_iota(I32, (CHUNK, CHUNK), 1)
    dist = (ri - ci).astype(F32)
    rowi = lax.broadcasted_iota(I32, (CHUNK, 1), 0).astype(F32)

    def direction(d):
        raw = jnp.full((1, 1), dec_ref[l * 2 * RET_HEADS + d * RET_HEADS + hd], F32)
        lg = -_softplus(-raw)
        if d == 0:
            dec = jnp.exp(jnp.where(ri >= ci, dist * lg, -jnp.inf))
            e_in = jnp.exp((rowi + 1.0) * lg)
            wgt = jnp.exp((CHUNK - 1.0 - rowi) * lg)
        else:
            dec = jnp.exp(jnp.where(ri <= ci, -dist * lg, -jnp.inf))
            e_in = jnp.exp((CHUNK - rowi) * lg)
            wgt = jnp.exp(rowi * lg)
        full = jnp.exp(CHUNK * lg)

        def body(c, carry):
            cc = c if d == 0 else nc - 1 - c
            r0 = pl.multiple_of(cc * CHUNK, CHUNK)
            q = qs_ref[pl.ds(r0, CHUNK), :]
            k = ks_ref[pl.ds(r0, CHUNK), :]
            v = v_ref[pl.ds(r0, CHUNK), :].astype(BF16)
            g = _dot_nt(q, k.astype(BF16))
            st = st_ref[d]
            y = _dot((g * dec).astype(BF16), v) + e_in * _dot(q, st.astype(BF16))
            if d == 0:
                y_ref[pl.ds(r0, CHUNK), :] = y
            else:
                y_ref[pl.ds(r0, CHUNK), :] += y
            st_ref[d] = st * full + _dot((k * wgt).T.astype(BF16), v)
            return carry

        lax.fori_loop(0, nc, body, 0)

    direction(0)
    direction(1)

    for r in range(nc):
        sl = slice(r * CHUNK, (r + 1) * CHUNK)
        y = y_ref[sl, :]
        y = y * lax.rsqrt(jnp.mean(y * y, -1, keepdims=True) + RMS_EPS)
        y_ref[sl, :] = y * _silu(g_ref[sl, :])

    if not latent:
        for d in range(2):
            fin_ref[d] = st_ref[d]


def _retention(u2, ret_decay, seq, l, rope, s0):
    tp = u2.shape[0]
    nb = tp // seq
    latent = s0 is not None
    in_specs = [pl.BlockSpec((seq, RET_QK_DIM), lambda b, h, *_: (b, h)),
                pl.BlockSpec((seq, RET_QK_DIM), lambda b, h, *_: (b, RET_HEADS + h)),
                pl.BlockSpec((seq, RET_V_DIM), lambda b, h, *_: (b, 1024 // RET_V_DIM + h)),
                pl.BlockSpec((seq, RET_V_DIM), lambda b, h, *_: (b, 2048 // RET_V_DIM + h))]
    args = [u2, u2, u2, u2]
    if latent:
        for t in rope:
            in_specs.append(pl.BlockSpec((seq, HEAD_DIM), lambda b, h, *_: (0, 0)))
            args.append(t)
        in_specs.append(pl.BlockSpec((None, None, 2, None, RET_QK_DIM, RET_V_DIM),
                                     lambda b, h, *_: (b, l, 0, h, 0, 0)))
        args.append(s0)
    out_specs = [pl.BlockSpec((seq, RET_V_DIM), lambda b, h, *_: (b, h))]
    out_shape = [jax.ShapeDtypeStruct((tp, RET_HEADS * RET_V_DIM), F32)]
    if not latent:
        out_specs.append(pl.BlockSpec((None, 2, None, RET_QK_DIM, RET_V_DIM), lambda b, h, *_: (b, 0, h, 0, 0)))
        out_shape.append(jax.ShapeDtypeStruct((nb, 2, RET_HEADS, RET_QK_DIM, RET_V_DIM), F32))
    res = pl.pallas_call(
        functools.partial(_ret_kernel, seq=seq, l=l, latent=latent),
        grid_spec=pltpu.PrefetchScalarGridSpec(
            num_scalar_prefetch=1,
            grid=(nb, RET_HEADS),
            in_specs=in_specs,
            out_specs=out_specs,
            scratch_shapes=[pltpu.VMEM((seq, RET_QK_DIM), BF16),
                            pltpu.VMEM((seq, RET_QK_DIM), F32),
                            pltpu.VMEM((2, RET_QK_DIM, RET_V_DIM), F32)]),
        out_shape=out_shape,
        compiler_params=_cp("arbitrary", "arbitrary"),
        name="retention",
    )(ret_decay.reshape(-1), *args)
    return (res[0], None) if latent else (res[0], res[1])


def _softmax_pv(s, v, sink):
    m = jnp.max(s, -1, keepdims=True)
    if sink is not None:
        m = jnp.maximum(m, sink)
    e = jnp.exp(s - m)
    den = jnp.sum(e, -1, keepdims=True)
    if sink is not None:
        den = den + jnp.exp(sink - m)
    return _dot(e.astype(BF16), v) / den


def _ctx_attn_kernel(sink_ref, qa_ref, ka_ref, va_ref, qn_ref, kn_ref, vn_ref, ya_ref, yn_ref, *, l):
    scale = HEAD_DIM ** -0.5
    for h in range(WIN_HEADS):
        kv = h // WIN_GROUP
        hs = slice(h * HEAD_DIM, (h + 1) * HEAD_DIM)
        ks = slice(kv * HEAD_DIM, (kv + 1) * HEAD_DIM)
        s = _dot_nt(qa_ref[:, hs].astype(BF16), ka_ref[:, ks].astype(BF16)) * scale
        ya_ref[:, hs] = _softmax_pv(s, va_ref[:, ks].astype(BF16), sink_ref[l * WIN_HEADS + h])
    for h in range(NA_HEADS):
        hs = slice(h * HEAD_DIM, (h + 1) * HEAD_DIM)
        s = _dot_nt(qn_ref[:, hs].astype(BF16), kn_ref[:, hs].astype(BF16)) * scale
        yn_ref[:, hs] = _softmax_pv(s, vn_ref[:, hs].astype(BF16), None)


def _ctx_attention(u1, u2, win_sink, seq, l):
    tp = u1.shape[0]
    nb = tp // seq
    kvw = WIN_KV_HEADS * HEAD_DIM
    hw = NA_HEADS * HEAD_DIM
    return pl.pallas_call(
        functools.partial(_ctx_attn_kernel, l=l),
        grid_spec=pltpu.PrefetchScalarGridSpec(
            num_scalar_prefetch=1,
            grid=(nb,),
            in_specs=[pl.BlockSpec((seq, hw), lambda b, *_: (b, 0)),
                      pl.BlockSpec((seq, kvw), lambda b, *_: (b, 1024 // kvw)),
                      pl.BlockSpec((seq, kvw), lambda b, *_: (b, 1280 // kvw)),
                      pl.BlockSpec((seq, hw), lambda b, *_: (b, 3072 // hw)),
                      pl.BlockSpec((seq, hw), lambda b, *_: (b, 4096 // hw)),
                      pl.BlockSpec((seq, hw), lambda b, *_: (b, 5120 // hw))],
            out_specs=[pl.BlockSpec((seq, hw), lambda b, *_: (b, 0)),
                       pl.BlockSpec((seq, hw), lambda b, *_: (b, 0))]),
        out_shape=[jax.ShapeDtypeStruct((tp, hw), F32), jax.ShapeDtypeStruct((tp, hw), F32)],
        compiler_params=_cp("arbitrary"),
        name="ctx_attention",
    )(win_sink.reshape(-1), u1, u1, u1, u2, u2, u2)


def _win_kernel(sink_ref, q_ref, k_ref, v_ref, kc_ref, vc_ref, cos_ref, sa_ref, sb_ref, o_ref,
                kr_ref, vb_ref, *, seq, l):
    scale = HEAD_DIM ** -0.5
    kvh = pl.program_id(1)
    nq = seq // CHUNK
    band = 3 * CHUNK
    rows = WIN_GROUP * CHUNK
    kr_ref[...] = _rope(k_ref[...], cos_ref[...], sa_ref[...], sb_ref[...]).astype(BF16)
    vb_ref[...] = v_ref[...].astype(BF16)
    kc = kc_ref[...].astype(BF16)
    vc = vc_ref[...].astype(BF16)
    rg = lax.broadcasted_iota(I32, (rows, 1), 0) // CHUNK
    sink = jnp.zeros((rows, 1), F32)
    for g in range(WIN_GROUP):
        sink = jnp.where(rg == g, sink_ref[l * WIN_HEADS + kvh * WIN_GROUP + g], sink)
    qoff = lax.broadcasted_iota(I32, (rows, band), 0) % CHUNK
    koff = lax.broadcasted_iota(I32, (rows, band), 1)

    def body(n, carry):
        r0 = pl.multiple_of(n * CHUNK, CHUNK)
        cos = cos_ref[pl.ds(r0, CHUNK), :]
        sa = sa_ref[pl.ds(r0, CHUNK), :]
        sb = sb_ref[pl.ds(r0, CHUNK), :]
        qs = jnp.concatenate(
            [_rope(q_ref[pl.ds(r0, CHUNK), g * HEAD_DIM:(g + 1) * HEAD_DIM], cos, sa, sb)
             for g in range(WIN_GROUP)], axis=0).astype(BF16)
        start = pl.multiple_of(jnp.clip((n - 1) * CHUNK, 0, seq - band), CHUNK)
        kb = kr_ref[pl.ds(start, band), :]
        vb = vb_ref[pl.ds(start, band), :]
        s_loc = _dot_nt(qs, kb) * scale
        ok = jnp.abs(r0 + qoff - (start + koff)) <= WINDOW
        s_loc = jnp.where(ok, s_loc, -jnp.inf)
        s_ctx = _dot_nt(qs, kc) * scale
        m = jnp.maximum(jnp.maximum(jnp.max(s_loc, -1, keepdims=True), jnp.max(s_ctx, -1, keepdims=True)), sink)
        e_loc = jnp.exp(s_loc - m)
        e_ctx = jnp.exp(s_ctx - m)
        den = jnp.sum(e_loc, -1, keepdims=True) + jnp.sum(e_ctx, -1, keepdims=True) + jnp.exp(sink - m)
        o = (_dot(e_ctx.astype(BF16), vc) + _dot(e_loc.astype(BF16), vb)) / den
        for g in range(WIN_GROUP):
            o_ref[pl.ds(r0, CHUNK), g * HEAD_DIM:(g + 1) * HEAD_DIM] = o[g * CHUNK:(g + 1) * CHUNK, :]
        return carry

    lax.fori_loop(0, nq, body, 0)


def _win_attention(u1, cache_k, cache_v, win_sink, rope, seq, l):
    tp = u1.shape[0]
    nb = tp // seq
    past = cache_k.shape[2]
    gw = WIN_GROUP * HEAD_DIM
    ck = cache_k.reshape(nb, DEPTH, past, WIN_KV_HEADS * HEAD_DIM)
    cv = cache_v.reshape(nb, DEPTH, past, WIN_KV_HEADS * HEAD_DIM)
    rope_spec = pl.BlockSpec((seq, HEAD_DIM), lambda b, h, *_: (0, 0))
    return pl.pallas_call(
        functools.partial(_win_kernel, seq=seq, l=l),
        grid_spec=pltpu.PrefetchScalarGridSpec(
            num_scalar_prefetch=1,
            grid=(nb, WIN_KV_HEADS),
            in_specs=[pl.BlockSpec((seq, gw), lambda b, h, *_: (b, h)),
                      pl.BlockSpec((seq, HEAD_DIM), lambda b, h, *_: (b, 1024 // HEAD_DIM + h)),
                      pl.BlockSpec((seq, HEAD_DIM), lambda b, h, *_: (b, 1280 // HEAD_DIM + h)),
                      pl.BlockSpec((None, None, past, HEAD_DIM), lambda b, h, *_: (b, l, 0, h)),
                      pl.BlockSpec((None, None, past, HEAD_DIM), lambda b, h, *_: (b, l, 0, h)),
                      rope_spec, rope_spec, rope_spec],
            out_specs=pl.BlockSpec((seq, gw), lambda b, h, *_: (b, h)),
            scratch_shapes=[pltpu.VMEM((seq, HEAD_DIM), BF16), pltpu.VMEM((seq, HEAD_DIM), BF16)]),
        out_shape=jax.ShapeDtypeStruct((tp, WIN_HEADS * HEAD_DIM), F32),
        compiler_params=_cp("arbitrary", "arbitrary"),
        name="win_attention",
    )(win_sink.reshape(-1), u1, u1, u1, ck, cv, *rope)


def _na_window_start(j, rows):
    kr = min(NA_ROWS, rows)
    rs = jnp.clip(2 * j - kr // 2, 0, rows - kr)
    return jnp.minimum(rs, rows - NA_WIN_ROWS)


def _na_kernel(q_ref, k_ref, v_ref, kc_ref, vc_ref, bias_ref, o_ref, kb_ref, vb_ref, *, seq):
    scale = HEAD_DIM ** -0.5
    rows = seq // GRID_W
    nq = seq // CHUNK
    win = NA_WIN_ROWS * GRID_W
    kb_ref[...] = k_ref[...].astype(BF16)
    vb_ref[...] = v_ref[...].astype(BF16)
    kc = kc_ref[...].astype(BF16)
    vc = vc_ref[...].astype(BF16)

    def body(j, carry):
        r0 = pl.multiple_of(j * CHUNK, CHUNK)
        q = q_ref[pl.ds(r0, CHUNK), :].astype(BF16)
        start = pl.multiple_of(_na_window_start(j, rows) * GRID_W, GRID_W)
        kw = kb_ref[pl.ds(start, win), :]
        vw = vb_ref[pl.ds(start, win), :]
        s_loc = _dot_nt(q, kw) * scale + bias_ref[j]
        s_ctx = _dot_nt(q, kc) * scale
        m = jnp.maximum(jnp.max(s_loc, -1, keepdims=True), jnp.max(s_ctx, -1, keepdims=True))
        e_loc = jnp.exp(s_loc - m)
        e_ctx = jnp.exp(s_ctx - m)
        den = jnp.sum(e_loc, -1, keepdims=True) + jnp.sum(e_ctx, -1, keepdims=True)
        o_ref[pl.ds(r0, CHUNK), :] = (_dot(e_ctx.astype(BF16), vc) + _dot(e_loc.astype(BF16), vw)) / den
        return carry

    lax.fori_loop(0, nq, body, 0)


def _na_bias_table(rpb, seq):
    nh = rpb.shape[0]
    rows = seq // GRID_W
    kr = min(NA_ROWS, rows)
    nq = seq // CHUNK
    ndr, ndc = 2 * NA_ROWS - 1, 2 * NA_COLS - 1
    halves = CHUNK // GRID_W
    qc = np.arange(GRID_W)[:, None]
    kc = np.arange(GRID_W)[None, :]
    dc = (np.clip(kc - qc, 1 - NA_COLS, NA_COLS - 1) + (NA_COLS - 1)).reshape(-1)
    cs = np.clip(qc - NA_COLS // 2, 0, GRID_W - NA_COLS)
    col_ok = ((kc >= cs) & (kc < cs + NA_COLS)).reshape(-1)
    onehot = (dc[None, :] == np.arange(ndc)[:, None]).astype(np.float32)
    blocks = jnp.einsum('hrd,dq->hrq', rpb, onehot, precision=lax.Precision.HIGHEST)
    blocks = jnp.where(col_ok[None, None, :], blocks, NEG_BIG)
    blocks = jnp.concatenate([blocks, jnp.full((nh, 1, GRID_W * GRID_W), NEG_BIG, F32)], axis=1)
    blocks = blocks.reshape(nh, ndr + 1, GRID_W, GRID_W)
    j = np.arange(nq)[:, None, None]
    r = 2 * j + np.arange(halves)[None, :, None]
    ws = np.minimum(np.clip(2 * j - kr // 2, 0, rows - kr), rows - NA_WIN_ROWS)
    krow = ws + np.arange(NA_WIN_ROWS)[None, None, :]
    rs = np.clip(r - kr // 2, 0, rows - kr)
    blk = np.where((krow >= rs) & (krow < rs + kr), krow - r + (NA_ROWS - 1), ndr)
    t = jnp.take(blocks, jnp.asarray(blk.reshape(-1), I32), axis=1)
    t = t.reshape(nh, nq, halves, NA_WIN_ROWS, GRID_W, GRID_W).transpose(0, 1, 2, 4, 3, 5)
    return t.reshape(nh, nq, CHUNK, NA_WIN_ROWS * GRID_W)


def _na_attention(u2, cache_k, cache_v, bias, seq, l):
    tp = u2.shape[0]
    nb = tp // seq
    past = cache_k.shape[2]
    nq = seq // CHUNK
    win = NA_WIN_ROWS * GRID_W
    ck = cache_k.reshape(nb, DEPTH, past, NA_HEADS * HEAD_DIM)
    cv = cache_v.reshape(nb, DEPTH, past, NA_HEADS * HEAD_DIM)
    return pl.pallas_call(
        functools.partial(_na_kernel, seq=seq),
        grid=(nb, NA_HEADS),
        in_specs=[pl.BlockSpec((seq, HEAD_DIM), lambda b, h: (b, 3072 // HEAD_DIM + h)),
                  pl.BlockSpec((seq, HEAD_DIM), lambda b, h: (b, 4096 // HEAD_DIM + h)),
                  pl.BlockSpec((seq, HEAD_DIM), lambda b, h: (b, 5120 // HEAD_DIM + h)),
                  pl.BlockSpec((None, None, past, HEAD_DIM), lambda b, h: (b, l, 0, h)),
                  pl.BlockSpec((None, None, past, HEAD_DIM), lambda b, h: (b, l, 0, h)),
                  pl.BlockSpec((None, nq, CHUNK, win), lambda b, h: (h, 0, 0, 0))],
        out_specs=pl.BlockSpec((seq, HEAD_DIM), lambda b, h: (b, h)),
        out_shape=jax.ShapeDtypeStruct((tp, NA_HEADS * HEAD_DIM), F32),
        scratch_shapes=[pltpu.VMEM((seq, HEAD_DIM), BF16), pltpu.VMEM((seq, HEAD_DIM), BF16)],
        compiler_params=_cp("arbitrary", "arbitrary"),
        name="na_attention",
    )(u2, u2, u2, ck, cv, bias)


def _merge_kernel(ya_ref, ys_ref, yr_ref, yn_ref, g0_ref, g1_ref, g2_ref, g3_ref, w_ref, o_ref, wbf):
    @pl.when(pl.program_id(1) == 0)
    def _():
        wbf[...] = w_ref[...].astype(BF16)

    acc = None
    for n, (b_ref, g_ref) in enumerate(((ya_ref, g0_ref), (ys_ref, g1_ref), (yr_ref, g2_ref), (yn_ref, g3_ref))):
        t = jax.nn.sigmoid(g_ref[...]) * _dot(b_ref[...].astype(BF16), wbf[n])
        acc = t if acc is None else acc + t
    o_ref[...] = acc.astype(BF16)


def _merge(ya, ys, yr, yn, u2, w_branch, l):
    tp = ya.shape[0]
    tm, tn = 256, 512
    g_off = 6144 // tn
    per = D_MODEL // tn
    br = pl.BlockSpec((tm, BRANCH_W), lambda j, i: (i, 0))
    gate = lambda n: pl.BlockSpec((tm, tn), lambda j, i: (i, g_off + n * per + j))
    return pl.pallas_call(
        _merge_kernel,
        grid=(per, tp // tm),
        in_specs=[br, br, br, br, gate(0), gate(1), gate(2), gate(3),
                  pl.BlockSpec((None, N_BRANCH, BRANCH_W, tn), lambda j, i: (l, 0, 0, j))],
        out_specs=pl.BlockSpec((tm, tn), lambda j, i: (i, j)),
        out_shape=jax.ShapeDtypeStruct((tp, D_MODEL), BF16),
        scratch_shapes=[pltpu.VMEM((N_BRANCH, BRANCH_W, tn), BF16)],
        compiler_params=_cp("arbitrary", "arbitrary"),
        name="merge",
    )(ya, ys, yr, yn, u2, u2, u2, u2, w_branch)


def _out_kernel(m_ref, w_ref, x_ref, mod_ref, g_ref, b_ref, o_ref):
    mix = _dot(m_ref[...], w_ref[...])
    y = ALPHA * x_ref[...] + mod_ref[2:3, :] * mix
    o_ref[...] = _layer_norm(y, g_ref[...], b_ref[...])


def _out_proj(merged, w_out_bf, x, mod, ln_g, ln_b, l, row_fn):
    tp = x.shape[0]
    tm = TOK_TILE
    row = pl.BlockSpec((tm, D_MODEL), lambda i: (i, 0))
    vec = pl.BlockSpec((None, 1, D_MODEL), lambda i: (l, 0, 0))
    return pl.pallas_call(
        _out_kernel,
        grid=(tp // tm,),
        in_specs=[row,
                  pl.BlockSpec((None, D_MODEL, D_MODEL), lambda i: (l, 0, 0)),
                  row,
                  pl.BlockSpec((None, None, 6, D_MODEL), lambda i: (l, row_fn(i), 0, 0)),
                  vec, vec],
        out_specs=row,
        out_shape=jax.ShapeDtypeStruct((tp, D_MODEL), F32),
        compiler_params=_cp("arbitrary"),
        name="out_proj_ln1",
    )(merged, w_out_bf, x, mod, ln_g.reshape(DEPTH, 1, D_MODEL), ln_b.reshape(DEPTH, 1, D_MODEL))


def _pack_pair(a, b):
    ab = lax.bitcast_convert_type(a.astype(BF16).astype(F32), jnp.uint32)
    bb = lax.bitcast_convert_type(b.astype(BF16).astype(F32), jnp.uint32)
    return (ab & jnp.uint32(0xFFFF0000)) | (bb >> 16)


def _unpack_pair(p):
    a = lax.bitcast_convert_type(p & jnp.uint32(0xFFFF0000), F32)
    b = lax.bitcast_convert_type(p << 16, F32)
    return a, b


HALF = D_MODEL // 2


def _router_kernel(xc_ref, xl_ref, mod_ref, rw_ref, rb_ref, xp_ref, idx_ref, wt_ref, pos_ref, cnt_ref, carry,
                   *, n_ctx_tiles):
    i = pl.program_id(0)

    @pl.when(i == 0)
    def _():
        carry[...] = jnp.zeros(carry.shape, F32)

    x = jnp.where(i < n_ctx_tiles, xc_ref[...], xl_ref[...])
    xm = x * (1.0 + mod_ref[4:5, :]) + mod_ref[3:4, :]
    xp_ref[...] = _pack_pair(xm[:, :HALF], xm[:, HALF:])
    tm = xm.shape[0]
    scores = jax.nn.sigmoid(_dot_nt(rw_ref[...], xm.astype(BF16)))
    biased = scores + rb_ref[...]
    per = N_EXPERTS // N_EXPERT_GROUPS
    sub = lax.broadcasted_iota(I32, (per, tm), 0)
    bg = [biased[per * g:per * (g + 1), :] for g in range(N_EXPERT_GROUPS)]
    sg = [scores[per * g:per * (g + 1), :] for g in range(N_EXPERT_GROUPS)]
    gs = []
    for g in range(N_EXPERT_GROUPS):
        m1 = jnp.max(bg[g], 0, keepdims=True)
        i1 = jnp.min(jnp.where(bg[g] == m1, sub, per), 0, keepdims=True)
        m2 = jnp.max(jnp.where(sub == i1, -jnp.inf, bg[g]), 0, keepdims=True)
        gs.append(m1 + m2)
    cand = []
    for g in range(N_EXPERT_GROUPS):
        rank = jnp.zeros((1, tm), I32)
        for g2 in range(N_EXPERT_GROUPS):
            if g2 == g:
                continue
            ahead = (gs[g2] > gs[g]) | ((gs[g2] == gs[g]) & (g2 < g))
            rank = rank + ahead.astype(I32)
        cand.append(jnp.where(rank < TOPK_GROUPS, bg[g], -jnp.inf))
    eidx = [sub + per * g for g in range(N_EXPERT_GROUPS)]
    idx_rows, w_rows = [], []
    for _ in range(TOP_K):
        m = None
        for g in range(N_EXPERT_GROUPS):
            t = jnp.max(cand[g], 0, keepdims=True)
            m = t if m is None else jnp.maximum(m, t)
        ik = None
        for g in range(N_EXPERT_GROUPS):
            t = jnp.min(jnp.where(cand[g] == m, eidx[g], N_EXPERTS), 0, keepdims=True)
            ik = t if ik is None else jnp.minimum(ik, t)
        wk = jnp.zeros((1, tm), F32)
        for g in range(N_EXPERT_GROUPS):
            hit = eidx[g] == ik
            wk = wk + jnp.sum(jnp.where(hit, sg[g], 0.0), 0, keepdims=True)
            cand[g] = jnp.where(hit, -jnp.inf, cand[g])
        idx_rows.append(ik)
        w_rows.append(wk)
    tot = w_rows[0]
    for wk in w_rows[1:]:
        tot = tot + wk
    idx_ref[...] = jnp.concatenate(idx_rows, axis=0)
    wt_ref[...] = jnp.concatenate([wk / tot * ROUTED_SCALE for wk in w_rows], axis=0)

    member = []
    for g in range(N_EXPERT_GROUPS):
        sel = eidx[g] == idx_rows[0]
        for k in range(1, TOP_K):
            sel = sel | (eidx[g] == idx_rows[k])
        member.append(jnp.where(sel, 1.0, 0.0))
    member = jnp.concatenate(member, axis=0)
    earlier = (lax.broadcasted_iota(I32, (tm, tm), 0) < lax.broadcasted_iota(I32, (tm, tm), 1))
    before = carry[...] + _dot(member.astype(BF16), jnp.where(earlier, 1.0, 0.0).astype(BF16))
    pos_rows = []
    for k in range(TOP_K):
        pk = jnp.zeros((1, tm), F32)
        for g in range(N_EXPERT_GROUPS):
            pk = pk + jnp.sum(jnp.where(eidx[g] == idx_rows[k], before[per * g:per * (g + 1), :], 0.0),
                              0, keepdims=True)
        pos_rows.append(pk)
    pos_ref[...] = jnp.concatenate(pos_rows, axis=0).astype(I32)
    carry[...] = carry[...] + jnp.sum(member, axis=1, keepdims=True)
    cnt_ref[...] = carry[...]


def _router(x1c, x1l, mod, rwt_bf, rbias, l, n_lat_tiles_per_seq):
    tc, tl = x1c.shape[0], x1l.shape[0]
    tm = TOK_TILE
    nct, nlt = tc // tm, tl // tm
    t = tc + tl
    row_fn = lambda i: jnp.where(i < nct, 0, 1 + (i - nct) // n_lat_tiles_per_seq)
    return pl.pallas_call(
        functools.partial(_router_kernel, n_ctx_tiles=nct),
        grid=(nct + nlt,),
        in_specs=[pl.BlockSpec((tm, D_MODEL), lambda i: (jnp.minimum(i, nct - 1), 0)),
                  pl.BlockSpec((tm, D_MODEL), lambda i: (jnp.maximum(i - nct, 0), 0)),
                  pl.BlockSpec((None, None, 6, D_MODEL), lambda i: (l, row_fn(i), 0, 0)),
                  pl.BlockSpec((N_EXPERTS, D_MODEL), lambda i: (0, 0)),
                  pl.BlockSpec((N_EXPERTS, 1), lambda i: (0, 0))],
        out_specs=[pl.BlockSpec((tm, HALF), lambda i: (i, 0)),
                   pl.BlockSpec((TOP_K, tm), lambda i: (0, i)),
                   pl.BlockSpec((TOP_K, tm), lambda i: (0, i)),
                   pl.BlockSpec((TOP_K, tm), lambda i: (0, i)),
                   pl.BlockSpec((N_EXPERTS, 1), lambda i: (0, 0))],
        out_shape=[jax.ShapeDtypeStruct((t, HALF), jnp.uint32),
                   jax.ShapeDtypeStruct((TOP_K, t), I32),
                   jax.ShapeDtypeStruct((TOP_K, t), F32),
                   jax.ShapeDtypeStruct((TOP_K, t), I32),
                   jax.ShapeDtypeStruct((N_EXPERTS, 1), F32)],
        scratch_shapes=[pltpu.VMEM((N_EXPERTS, 1), F32)],
        compiler_params=_cp("arbitrary"),
        name="router",
    )(x1c, x1l, mod, rwt_bf, rbias.reshape(N_EXPERTS, 1))


DMA_UNROLL = 8
N_PAD_SLOTS = N_EXPERTS * MOE_BM


def _dispatch_kernel(pad_ref, dst_ref, xp_ref, xs_hbm, zrow, sem, zsem):
    tm = xp_ref.shape[0]
    n = TOP_K * tm

    @pl.when(pl.program_id(0) == 0)
    def _():
        zrow[...] = jnp.zeros(zrow.shape, jnp.uint32)

        def z_issue(s, carry):
            pltpu.make_async_copy(zrow.at[pl.ds(0, 1), :], xs_hbm.at[pl.ds(pad_ref[s], 1), :], zsem).start()
            return carry

        def z_drain(s, carry):
            pltpu.make_async_copy(zrow.at[pl.ds(0, 1), :], xs_hbm.at[pl.ds(0, 1), :], zsem).wait()
            return carry

        lax.fori_loop(0, N_PAD_SLOTS, z_issue, 0, unroll=DMA_UNROLL)
        lax.fori_loop(0, N_PAD_SLOTS, z_drain, 0, unroll=DMA_UNROLL)

    def issue(s, carry):
        r = s % tm
        pltpu.make_async_copy(xp_ref.at[pl.ds(r, 1), :], xs_hbm.at[pl.ds(dst_ref[0, s], 1), :], sem).start()
        return carry

    def drain(s, carry):
        pltpu.make_async_copy(xp_ref.at[pl.ds(0, 1), :], xs_hbm.at[pl.ds(0, 1), :], sem).wait()
        return carry

    lax.fori_loop(0, n, issue, 0, unroll=DMA_UNROLL)
    lax.fori_loop(0, n, drain, 0, unroll=DMA_UNROLL)


def _dispatch(xp, dest_tiles, pad_dst, n_rows):
    t = xp.shape[0]
    tm = TOK_TILE
    return pl.pallas_call(
        _dispatch_kernel,
        grid_spec=pltpu.PrefetchScalarGridSpec(
            num_scalar_prefetch=1,
            grid=(t // tm,),
            in_specs=[pl.BlockSpec((None, 1, TOP_K * tm), lambda i, pad: (i, 0, 0), memory_space=pltpu.SMEM),
                      pl.BlockSpec((tm, HALF), lambda i, pad: (i, 0))],
            out_specs=pl.BlockSpec(memory_space=pl.ANY),
            scratch_shapes=[pltpu.VMEM((8, HALF), jnp.uint32),
                            pltpu.SemaphoreType.DMA(()), pltpu.SemaphoreType.DMA(())]),
        out_shape=jax.ShapeDtypeStruct((n_rows, HALF), jnp.uint32),
        compiler_params=_cp("arbitrary"),
        name="dispatch",
    )(pad_dst, dest_tiles, xp)


def _experts_kernel(be_ref, nu_ref, x_ref, wgu_ref, wdn_ref, y_ref, wgu_bf, wdn_bf):
    i = pl.program_id(0)

    @pl.when(i < nu_ref[0])
    def _():
        prev = be_ref[jnp.maximum(i - 1, 0)]

        @pl.when((i == 0) | (be_ref[i] != prev))
        def _():
            wgu_bf[...] = wgu_ref[...].astype(BF16)
            wdn_bf[...] = wdn_ref[...].astype(BF16)

        xa, xb = _unpack_pair(x_ref[...])
        hgu = _dot(xa.astype(BF16), wgu_bf[:HALF, :]) + _dot(xb.astype(BF16), wgu_bf[HALF:, :])
        act = (_silu(hgu[:, :EXPERT_FF]) * hgu[:, EXPERT_FF:]).astype(BF16)
        y = _dot(act, wdn_bf[...])
        y_ref[...] = _pack_pair(y[:, :HALF], y[:, HALF:])

    @pl.when(i >= nu_ref[0])
    def _():
        y_ref[...] = jnp.zeros(y_ref.shape, jnp.uint32)


def _experts(xs, blk_e, n_used, w_gu, w_dn, l):
    nblk = blk_e.shape[0]
    last = lambda i, be, nu: jnp.minimum(i, nu[0] - 1)
    return pl.pallas_call(
        _experts_kernel,
        grid_spec=pltpu.PrefetchScalarGridSpec(
            num_scalar_prefetch=2,
            grid=(nblk,),
            in_specs=[pl.BlockSpec((MOE_BM, HALF), lambda i, be, nu: (last(i, be, nu), 0)),
                      pl.BlockSpec((None, None, D_MODEL, 2 * EXPERT_FF),
                                   lambda i, be, nu: (l, be[last(i, be, nu)], 0, 0)),
                      pl.BlockSpec((None, None, EXPERT_FF, D_MODEL),
                                   lambda i, be, nu: (l, be[last(i, be, nu)], 0, 0))],
            out_specs=pl.BlockSpec((MOE_BM, HALF), lambda i, be, nu: (i, 0)),
            scratch_shapes=[pltpu.VMEM((D_MODEL, 2 * EXPERT_FF), BF16),
                            pltpu.VMEM((EXPERT_FF, D_MODEL), BF16)]),
        out_shape=jax.ShapeDtypeStruct((nblk * MOE_BM, HALF), jnp.uint32),
        compiler_params=_cp("arbitrary"),
        name="experts",
    )(blk_e, n_used, xs, w_gu, w_dn)


FIN_TM = 128


def _final_kernel(dcur_ref, dnxt_ref, ys_hbm, x1_ref, xp_ref, wk_ref, mod_ref, g_ref, b_ref, sgu_ref, sdn_ref,
                  o_ref, gbuf, sem):
    n = TOP_K * FIN_TM
    i = pl.program_id(0)
    slot = i % 2

    def fetch(d_ref, sl):
        def issue(s, carry):
            k = s // FIN_TM
            r = s - k * FIN_TM
            pltpu.make_async_copy(ys_hbm.at[pl.ds(d_ref[0, s], 1), :], gbuf.at[sl, k, pl.ds(r, 1), :],
                                  sem.at[sl]).start()
            return carry

        lax.fori_loop(0, n, issue, 0, unroll=DMA_UNROLL)

    @pl.when(i == 0)
    def _():
        fetch(dcur_ref, 0)

    @pl.when(i + 1 < pl.num_programs(0))
    def _():
        fetch(dnxt_ref, 1 - slot)

    xa, xb = _unpack_pair(xp_ref[...])
    hgu = _dot(xa.astype(BF16), sgu_ref[:HALF, :]) + _dot(xb.astype(BF16), sgu_ref[HALF:, :])
    act = (_silu(hgu[:, :SHARED_FF]) * hgu[:, SHARED_FF:]).astype(BF16)
    shared = _dot(act, sdn_ref[...])

    def drain(s, carry):
        k = s // FIN_TM
        r = s - k * FIN_TM
        pltpu.make_async_copy(ys_hbm.at[pl.ds(0, 1), :], gbuf.at[slot, k, pl.ds(r, 1), :], sem.at[slot]).wait()
        return carry

    lax.fori_loop(0, n, drain, 0, unroll=DMA_UNROLL)

    ra = rb = None
    for k in range(TOP_K):
        ya, yb = _unpack_pair(gbuf[slot, k])
        w = wk_ref[:, k:k + 1]
        ra = w * ya if ra is None else ra + w * ya
        rb = w * yb if rb is None else rb + w * yb
    gate = mod_ref[5:6, :]
    x1 = x1_ref[...]
    ya = ALPHA * x1[:, :HALF] + gate[:, :HALF] * (ra + shared[:, :HALF])
    yb = ALPHA * x1[:, HALF:] + gate[:, HALF:] * (rb + shared[:, HALF:])
    mu = (jnp.sum(ya, -1, keepdims=True) + jnp.sum(yb, -1, keepdims=True)) / D_MODEL
    ya = ya - mu
    yb = yb - mu
    var = (jnp.sum(ya * ya, -1, keepdims=True) + jnp.sum(yb * yb, -1, keepdims=True)) / D_MODEL
    inv = lax.rsqrt(var + LN_EPS)
    o_ref[:, :HALF] = ya * inv * g_ref[:, :HALF] + b_ref[:, :HALF]
    o_ref[:, HALF:] = yb * inv * g_ref[:, HALF:] + b_ref[:, HALF:]


def _final(dest, ys, x1, xp, wk, mod, ln_g, ln_b, sgu_bf, sdn_bf, l, row_fn, tok_off):
    tp = x1.shape[0]
    tm = FIN_TM
    off = tok_off // tm
    nt = tp // tm
    row = pl.BlockSpec((tm, D_MODEL), lambda i: (i, 0))
    vec = pl.BlockSpec((None, 1, D_MODEL), lambda i: (l, 0, 0))
    return pl.pallas_call(
        _final_kernel,
        grid=(nt,),
        in_specs=[pl.BlockSpec((None, 1, TOP_K * tm), lambda i: (off + i, 0, 0), memory_space=pltpu.SMEM),
                  pl.BlockSpec((None, 1, TOP_K * tm), lambda i: (off + jnp.minimum(i + 1, nt - 1), 0, 0),
                               memory_space=pltpu.SMEM),
                  pl.BlockSpec(memory_space=pl.ANY),
                  row,
                  pl.BlockSpec((tm, HALF), lambda i: (off + i, 0)),
                  pl.BlockSpec((tm, TOP_K), lambda i: (off + i, 0)),
                  pl.BlockSpec((None, None, 6, D_MODEL), lambda i: (l, row_fn(i * tm // TOK_TILE), 0, 0)),
                  vec, vec,
                  pl.BlockSpec((None, D_MODEL, 2 * SHARED_FF), lambda i: (l, 0, 0)),
                  pl.BlockSpec((None, SHARED_FF, D_MODEL), lambda i: (l, 0, 0))],
        out_specs=row,
        out_shape=jax.ShapeDtypeStruct((tp, D_MODEL), F32),
        scratch_shapes=[pltpu.VMEM((2, TOP_K, tm, HALF), jnp.uint32), pltpu.SemaphoreType.DMA((2,))],
        compiler_params=_cp("arbitrary"),
        name="combine_ln2",
    )(dest, dest, ys, x1, xp, wk, mod, ln_g.reshape(DEPTH, 1, D_MODEL), ln_b.reshape(DEPTH, 1, D_MODEL),
      sgu_bf, sdn_bf)


def _dispatch_plan(idx_t, pos_t, cnt):
    t = idx_t.shape[1]
    nblk = t * TOP_K // MOE_BM + N_EXPERTS
    n_slots = nblk * MOE_BM
    e_ids = jnp.arange(N_EXPERTS, dtype=I32)
    counts = cnt[:, 0].astype(I32)
    padded = (counts + MOE_BM - 1) // MOE_BM * MOE_BM
    incl = e_ids[None, :] <= e_ids[:, None]
    pad_end = jnp.sum(jnp.where(incl, padded[None, :], 0), axis=1)
    pad_start = pad_end - padded
    start_of = jnp.sum(jnp.where(idx_t[:, :, None] == e_ids, pad_start, 0), axis=-1)
    dest = start_of + pos_t

    def tiles(tm):
        return dest.reshape(TOP_K, t // tm, tm).transpose(1, 0, 2).reshape(t // tm, 1, TOP_K * tm)

    blk_e = jnp.minimum(jnp.sum((pad_end[None, :] <= (jnp.arange(nblk, dtype=I32) * MOE_BM)[:, None]).astype(I32),
                                axis=1), N_EXPERTS - 1)
    n_used = (pad_end[-1] // MOE_BM).reshape(1)
    free = padded - counts
    free_end = jnp.sum(jnp.where(incl, free[None, :], 0), axis=1)
    free_start = free_end - free
    j = jnp.arange(n_slots - t * TOP_K, dtype=I32)
    owner = jnp.sum((free_end[None, :] <= j[:, None]).astype(I32), axis=1)
    own = owner[:, None] == e_ids
    in_pad = jnp.sum(jnp.where(own, (pad_start + counts - free_start)[None, :], 0), axis=1) + j
    in_tail = pad_end[-1] + j - free_end[-1]
    pad_dst = jnp.where(owner < N_EXPERTS, in_pad, in_tail)
    return tiles(TOK_TILE), tiles(FIN_TM), pad_dst, blk_e, n_used


def _rope_tables(n_tok):
    t = jnp.arange(n_tok)
    row = (t // GRID_W).astype(F32)
    col = (t % GRID_W).astype(F32)
    half = HEAD_DIM // 2
    inv = ROPE_BASE ** (-jnp.arange(0, half, 2, dtype=F32) / half)
    ar = row[:, None] * inv
    ac = col[:, None] * inv
    ang = jnp.concatenate([ar, ar, ac, ac], -1)
    cos, sin = jnp.cos(ang), jnp.sin(ang)
    quarter = (jnp.arange(HEAD_DIM) // (HEAD_DIM // 4)) % 2
    sin_a = jnp.where(quarter == 0, -sin, 0.0)
    sin_b = jnp.where(quarter == 1, sin, 0.0)
    return cos, sin_a, sin_b


def kernel(x_prompt, x_sample, cache_win_k, cache_win_v, state_ssm, state_ret, cache_na_k, cache_na_v, c, c_ctx, w_ada, b_ada, w_in, win_sink, conv_w, conv_b, dt_bias, a_log, d_skip, ssm_norm, ret_decay, na_rpb, w_branch, w_out, ln1_g, ln1_b, router_w, router_bias, exp_w_gu, exp_w_down, sh_w_gu, sh_w_down, ln2_g, ln2_b):
    bc, lc, _ = x_prompt.shape
    bl, ll, _ = x_sample.shape
    assert lc == TOK_TILE and ll % TOK_TILE == 0
    tc, tl = bc * lc, bl * ll
    lat_tiles = ll // TOK_TILE

    nr = -(-(1 + bl) // 8) * 8
    cond = jnp.zeros((nr, D_MODEL), F32).at[0].set(c_ctx).at[1:1 + bl].set(c)
    mod = _ada(cond, w_ada, b_ada).reshape(DEPTH, nr, 6, D_MODEL)
    row_ctx = lambda i: 0
    row_lat = lambda i: 1 + i // lat_tiles

    rope = _rope_tables(ll)
    perm = np.array([d * SSM_HEADS + g * 8 + hh for g in range(SSM_GROUPS) for d in range(2) for hh in range(8)])
    dsk_full = jnp.repeat(d_skip, SSM_HEAD_DIM, axis=1)
    w_in2 = w_in[:, :, U2_OFF:].astype(BF16)
    w_dt = w_in[:, :, DT_OFF:U2_OFF][:, :, perm].astype(BF16)
    w_out_bf = w_out.astype(BF16)
    rwt_bf = jnp.swapaxes(router_w, 1, 2).astype(BF16)
    sgu_bf = sh_w_gu.astype(BF16)
    sdn_bf = sh_w_down.astype(BF16)

    xs = {"ctx": x_prompt.reshape(tc, D_MODEL), "lat": x_sample.reshape(tl, D_MODEL)}
    new_ctx = []
    for l in range(DEPTH):
        x1 = {}
        bias_tab = _na_bias_table(na_rpb[l], ll)
        for path in ("ctx", "lat"):
            latent = path == "lat"
            x = xs[path]
            seq = ll if latent else lc
            row_fn = row_lat if latent else row_ctx
            h = _modulate(x, mod, l, row_fn)
            u1 = _in_proj(h, w_in, l, U1_COLS)
            u2 = _in_proj(h, w_in2, l, U2_COLS)
            dt, la, lat_t = _ssd_steps(h, w_dt[l], dt_bias[l].reshape(-1)[perm], a_log[l].reshape(-1)[perm])
            xc = _conv(u1, conv_w[l], conv_b[l], seq)
            ys, ssm_fin = _ssd(xc, u1, dt, la, lat_t, dsk_full[l:l + 1], ssm_norm[l:l + 1], seq, l,
                               state_ssm if latent else None)
            yr, ret_fin = _retention(u2, ret_decay, seq, l, rope if latent else None,
                                     state_ret if latent else None)
            if latent:
                ya = _win_attention(u1, cache_win_k, cache_win_v, win_sink, rope, seq, l)
                yn = _na_attention(u2, cache_na_k, cache_na_v, bias_tab, seq, l)
            else:
                ya, yn = _ctx_attention(u1, u2, win_sink, seq, l)
                kv = lambda t, nh: t.reshape(bc, lc, nh, HEAD_DIM)
                new_ctx.append((kv(u1[:, 1024:1280], WIN_KV_HEADS), kv(u1[:, 1280:1536], WIN_KV_HEADS),
                                ssm_fin, ret_fin,
                                kv(u2[:, 4096:5120], NA_HEADS), kv(u2[:, 5120:6144], NA_HEADS)))
            merged = _merge(ya, ys, yr, yn, u2, w_branch, l)
            x1[path] = _out_proj(merged, w_out_bf, x, mod, ln1_g, ln1_b, l, row_fn)

        xp, idx_t, w_t, pos_t, cnt = _router(x1["ctx"], x1["lat"], mod, rwt_bf[l], router_bias[l], l, lat_tiles)
        dest_disp, dest_fin, pad_dst, blk_e, n_used = _dispatch_plan(idx_t, pos_t, cnt)
        x_sorted = _dispatch(xp, dest_disp, pad_dst, blk_e.shape[0] * MOE_BM)
        y_sorted = _experts(x_sorted, blk_e, n_used, exp_w_gu, exp_w_down, l)
        wk = w_t.T
        for path, off, row_fn in (("ctx", 0, row_ctx), ("lat", tc, row_lat)):
            xs[path] = _final(dest_fin, y_sorted, x1[path], xp, wk, mod, ln2_g, ln2_b, sgu_bf, sdn_bf,
                              l, row_fn, off)

    stack = lambda i: jnp.stack([t[i] for t in new_ctx], axis=1)
    return (xs["ctx"].reshape(bc, lc, D_MODEL), xs["lat"].reshape(bl, ll, D_MODEL),
            stack(0), stack(1), stack(2), stack(3), stack(4), stack(5))
```

```python
import functools

import jax
import jax.numpy as jnp
import numpy as np
from jax import lax
from jax.experimental import pallas as pl
from jax.experimental.pallas import tpu as pltpu

F32 = jnp.float32
BF16 = jnp.bfloat16
I32 = jnp.int32

D_MODEL = 2048
DEPTH = 2
GRID_W = 64
HEAD_DIM = 128
ROPE_BASE = 10000.0
CHUNK = 128
WIN_HEADS = 8
WIN_KV_HEADS = 2
WIN_GROUP = WIN_HEADS // WIN_KV_HEADS
WINDOW = 128
SSM_D_INNER = D_MODEL // 2
SSM_HEAD_DIM = 64
SSM_HEADS = SSM_D_INNER // SSM_HEAD_DIM
SSM_GROUPS = 2
SSM_STATE = 128
SSM_CONV = 7
SSM_CONV_DIM = SSM_D_INNER + 2 * SSM_GROUPS * SSM_STATE
RET_HEADS = 4
RET_QK_DIM = 128
RET_V_DIM = 256
NA_HEADS = 8
NA_ROWS = 8
NA_COLS = 16
N_BRANCH = 4
BRANCH_W = D_MODEL // 2
N_EXPERTS = 64
TOP_K = 8
N_EXPERT_GROUPS = 8
TOPK_GROUPS = 4
EXPERT_FF = 512
SHARED_FF = 512
ROUTED_SCALE = 2.5
ALPHA = (2.0 * DEPTH) ** 0.25
LN_EPS = 1e-5
RMS_EPS = 1e-6

U1_COLS = 4096
DT_OFF = 4096
U2_OFF = DT_OFF + 2 * SSM_HEADS
U2_COLS = 14336

TOK_TILE = 256
MOE_BM = 256
NA_WIN_ROWS = 10
NEG_BIG = -1e30

VMEM_LIMIT = 52 * 1024 * 1024


def _cp(*sem):
    return pltpu.CompilerParams(dimension_semantics=sem, vmem_limit_bytes=VMEM_LIMIT)


def _dot(a, b):
    return jnp.dot(a, b, preferred_element_type=F32)


def _dot_nt(a, b):
    return lax.dot_general(a, b, (((1,), (1,)), ((), ())), preferred_element_type=F32)


def _dot_exact(a, b):
    return jnp.dot(a, b, preferred_element_type=F32, precision=lax.Precision.HIGHEST)


def _silu(x):
    return x * jax.nn.sigmoid(x)


def _softplus(x):
    return jnp.maximum(x, 0.0) + jnp.log1p(jnp.exp(-jnp.abs(x)))


def _rope(x, cos, sin_a, sin_b):
    return x * cos + pltpu.roll(x, 96, 1) * sin_a + pltpu.roll(x, 32, 1) * sin_b


def _layer_norm(y, g, b):
    mu = jnp.mean(y, -1, keepdims=True)
    yc = y - mu
    var = jnp.mean(yc * yc, -1, keepdims=True)
    return yc * lax.rsqrt(var + LN_EPS) * g + b


def _ada_kernel(c_ref, w_ref, b_ref, o_ref):
    a = _silu(c_ref[...]).astype(BF16)
    o_ref[...] = _dot(a, w_ref[...].astype(BF16)) + b_ref[...]


def _ada(cond, w_ada, b_ada):
    nl, _, nout = w_ada.shape
    nr = cond.shape[0]
    tn = 1024
    return pl.pallas_call(
        _ada_kernel,
        grid=(nl, nout // tn),
        in_specs=[pl.BlockSpec((nr, D_MODEL), lambda l, j: (0, 0)),
                  pl.BlockSpec((None, D_MODEL, tn), lambda l, j: (l, 0, j)),
                  pl.BlockSpec((None, 1, tn), lambda l, j: (l, 0, j))],
        out_specs=pl.BlockSpec((None, nr, tn), lambda l, j: (l, 0, j)),
        out_shape=jax.ShapeDtypeStruct((nl, nr, nout), F32),
        compiler_params=_cp("arbitrary", "arbitrary"),
        name="ada",
    )(cond, w_ada, b_ada.reshape(nl, 1, nout))


def _mod_kernel(x_ref, m_ref, o_ref):
    o_ref[...] = (x_ref[...] * (1.0 + m_ref[1:2, :]) + m_ref[0:1, :]).astype(BF16)


def _modulate(x, mod, l, row_fn):
    tp = x.shape[0]
    return pl.pallas_call(
        _mod_kernel,
        grid=(tp // TOK_TILE,),
        in_specs=[pl.BlockSpec((TOK_TILE, D_MODEL), lambda i: (i, 0)),
                  pl.BlockSpec((None, None, 6, D_MODEL), lambda i: (l, row_fn(i), 0, 0))],
        out_specs=pl.BlockSpec((TOK_TILE, D_MODEL), lambda i: (i, 0)),
        out_shape=jax.ShapeDtypeStruct((tp, D_MODEL), BF16),
        compiler_params=_cp("arbitrary"),
        name="modulate",
    )(x, mod)


def _mm_kernel(x_ref, w_ref, o_ref, wbf):
    @pl.when(pl.program_id(1) == 0)
    def _():
        wbf[...] = w_ref[...].astype(BF16)

    o_ref[...] = _dot(x_ref[...], wbf[...])


def _in_proj(h, w, l, ncols):
    tp = h.shape[0]
    tm, tn = 512, 1024
    return pl.pallas_call(
        _mm_kernel,
        grid=(ncols // tn, tp // tm),
        in_specs=[pl.BlockSpec((tm, D_MODEL), lambda j, i: (i, 0)),
                  pl.BlockSpec((None, D_MODEL, tn), lambda j, i: (l, 0, j))],
        out_specs=pl.BlockSpec((tm, tn), lambda j, i: (i, j)),
        out_shape=jax.ShapeDtypeStruct((tp, ncols), F32),
        scratch_shapes=[pltpu.VMEM((D_MODEL, tn), BF16)],
        compiler_params=_cp("arbitrary", "arbitrary"),
        name="in_proj",
    )(h, w)


def _dt_kernel(h_ref, w_ref, wt_ref, b_ref, a_ref, bt_ref, at_ref, dt_ref, la_ref, lat_ref):
    h = h_ref[...]
    dt = _softplus(_dot(h, w_ref[...]) + b_ref[...])
    la = dt * -jnp.exp(a_ref[...])
    dtt = _softplus(_dot_nt(wt_ref[...], h) + bt_ref[...])
    lat = dtt * -jnp.exp(at_ref[...])
    for g in range(SSM_GROUPS):
        dt_ref[g] = dt[:, 16 * g:16 * (g + 1)]
        la_ref[g] = la[:, 16 * g:16 * (g + 1)]
        for q in range(h.shape[0] // CHUNK):
            lat_ref[g, q] = lat[16 * g:16 * (g + 1), q * CHUNK:(q + 1) * CHUNK]


def _ssd_steps(h, w_dt, dt_bias, a_log):
    tp = h.shape[0]
    tm = 512
    nq = tm // CHUNK
    return pl.pallas_call(
        _dt_kernel,
        grid=(tp // tm,),
        in_specs=[pl.BlockSpec((tm, D_MODEL), lambda i: (i, 0)),
                  pl.BlockSpec((D_MODEL, 32), lambda i: (0, 0)),
                  pl.BlockSpec((32, D_MODEL), lambda i: (0, 0)),
                  pl.BlockSpec((1, 32), lambda i: (0, 0)),
                  pl.BlockSpec((1, 32), lambda i: (0, 0)),
                  pl.BlockSpec((32, 1), lambda i: (0, 0)),
                  pl.BlockSpec((32, 1), lambda i: (0, 0))],
        out_specs=[pl.BlockSpec((SSM_GROUPS, tm, 16), lambda i: (0, i, 0)),
                   pl.BlockSpec((SSM_GROUPS, tm, 16), lambda i: (0, i, 0)),
                   pl.BlockSpec((SSM_GROUPS, nq, 16, CHUNK), lambda i: (0, i, 0, 0))],
        out_shape=[jax.ShapeDtypeStruct((SSM_GROUPS, tp, 16), F32),
                   jax.ShapeDtypeStruct((SSM_GROUPS, tp, 16), F32),
                   jax.ShapeDtypeStruct((SSM_GROUPS, tp // CHUNK, 16, CHUNK), F32)],
        compiler_params=_cp("arbitrary"),
        name="ssd_steps",
    )(h, w_dt, w_dt.T, dt_bias.reshape(1, 32), a_log.reshape(1, 32),
      dt_bias.reshape(32, 1), a_log.reshape(32, 1))


def _conv_kernel(x_ref, w_ref, b_ref, o_ref, pad_ref, *, seq):
    ct = x_ref.shape[1]
    pad = SSM_CONV // 2
    pad_ref[0:8, :] = jnp.zeros((8, ct), F32)
    pad_ref[8 + seq:16 + seq, :] = jnp.zeros((8, ct), F32)
    pad_ref[8:8 + seq, :] = x_ref[...]
    for r in range(seq // CHUNK):
        acc = jnp.broadcast_to(b_ref[...], (CHUNK, ct))
        for j in range(SSM_CONV):
            s = 8 - pad + j + r * CHUNK
            acc = acc + w_ref[j:j + 1, :] * pad_ref[s:s + CHUNK, :]
        o_ref[r * CHUNK:(r + 1) * CHUNK, :] = _silu(acc)


def _conv(u1, conv_w, conv_b, seq):
    tp = u1.shape[0]
    ct = 512
    c0 = 2560 // ct
    return pl.pallas_call(
        functools.partial(_conv_kernel, seq=seq),
        grid=(tp // seq, SSM_CONV_DIM // ct),
        in_specs=[pl.BlockSpec((seq, ct), lambda b, j: (b, c0 + j)),
                  pl.BlockSpec((SSM_CONV, ct), lambda b, j: (0, j)),
                  pl.BlockSpec((1, ct), lambda b, j: (0, j))],
        out_specs=pl.BlockSpec((seq, ct), lambda b, j: (b, j)),
        out_shape=jax.ShapeDtypeStruct((tp, SSM_CONV_DIM), F32),
        scratch_shapes=[pltpu.VMEM((seq + 16, ct), F32)],
        compiler_params=_cp("arbitrary", "arbitrary"),
        name="conv",
    )(u1, conv_w, conv_b.reshape(1, -1))


def _expand8(a, lane_lt64):
    tiles = [jnp.where(lane_lt64, a[:, 2 * p:2 * p + 1], a[:, 2 * p + 1:2 * p + 2]) for p in range(4)]
    return jnp.concatenate(tiles, axis=1)


def _ssd_kernel(*refs, seq, has_s0, want_fin):
    (xs_ref, bm_ref, cm_ref, z_ref, dt_ref, la_ref, lat_ref, dsk_ref, nrm_ref) = refs[:9]
    pos = 9
    s0_ref = None
    if has_s0:
        s0_ref = refs[pos]
        pos += 1
    o_ref = refs[pos]
    pos += 1
    fin_ref = None
    if want_fin:
        fin_ref = refs[pos]
        pos += 1
    st_ref = refs[pos]
    y_ref = refs[pos + 1]

    nc = seq // CHUNK
    hpg = SSM_HEADS // SSM_GROUPS
    ri = lax.broadcasted_iota(I32, (CHUNK, CHUNK), 0)
    ci = lax.broadcasted_iota(I32, (CHUNK, CHUNK), 1)
    lower = ri >= ci
    upper = ri <= ci
    lt = jnp.where(lower, 1.0, 0.0).astype(F32)
    ut = jnp.where(upper, 1.0, 0.0).astype(F32)
    lane_lt64 = lax.broadcasted_iota(I32, (1, 128), 1) < SSM_HEAD_DIM
    half0 = lax.broadcasted_iota(I32, (CHUNK, 128), 1) < SSM_HEAD_DIM

    for d in range(2):
        if has_s0:
            st_ref[d] = jnp.concatenate([s0_ref[d, hh] for hh in range(hpg)], axis=1)
        else:
            st_ref[d] = jnp.zeros((SSM_STATE, hpg * SSM_HEAD_DIM), F32)

    def chunk(c, d):
        r0 = pl.multiple_of(c * CHUNK, CHUNK)
        la_c = la_ref[pl.ds(r0, CHUNK), :][:, 8 * d:8 * d + 8]
        dt_c = dt_ref[pl.ds(r0, CHUNK), :][:, 8 * d:8 * d + 8]
        lat_c = lat_ref[c][8 * d:8 * d + 8, :]
        if d == 0:
            acs = _dot_exact(lt, la_c)
            acst = _dot_exact(lat_c, ut)
            mask = lower
            edge = acs[CHUNK - 1:CHUNK, :]
        else:
            acs = _dot_exact(ut, la_c)
            acst = _dot_exact(lat_c, lt)
            mask = upper
            edge = acs[0:1, :]
        xs_c = xs_ref[pl.ds(r0, CHUNK), :]
        b_c = bm_ref[pl.ds(r0, CHUNK), :]
        c_c = cm_ref[pl.ds(r0, CHUNK), :].astype(BF16)
        g = _dot_nt(c_c, b_c.astype(BF16))
        xdt = xs_c * _expand8(dt_c, lane_lt64)
        tiles = []
        for p in range(hpg // 2):
            xt = xdt[:, 128 * p:128 * (p + 1)]
            acc = None
            for q in range(2):
                hh = 2 * p + q
                dec = jnp.exp(jnp.where(mask, acs[:, hh:hh + 1] - acst[hh:hh + 1, :], -jnp.inf))
                v = jnp.where(half0 if q == 0 else jnp.logical_not(half0), xt, 0.0).astype(BF16)
                t = _dot((g * dec).astype(BF16), v)
                acc = t if acc is None else acc + t
            tiles.append(acc)
        y = jnp.concatenate(tiles, axis=1)
        st = st_ref[d]
        y = y + _expand8(jnp.exp(acs), lane_lt64) * _dot(c_c, st.astype(BF16))
        if d == 0:
            y_ref[pl.ds(r0, CHUNK), :] = y
        else:
            y_ref[pl.ds(r0, CHUNK), :] += y
        wgt = jnp.exp(edge - acs)
        v = (xdt * _expand8(wgt, lane_lt64)).astype(BF16)
        st_ref[d] = st * _expand8(jnp.exp(edge), lane_lt64) + _dot(b_c.T.astype(BF16), v)

    def fwd(c, carry):
        chunk(c, 0)
        return carry

    def bwd(c, carry):
        chunk(nc - 1 - c, 1)
        return carry

    lax.fori_loop(0, nc, fwd, 0)
    lax.fori_loop(0, nc, bwd, 0)

    for r in range(nc):
        sl = slice(r * CHUNK, (r + 1) * CHUNK)
        y = (y_ref[sl, :] + xs_ref[sl, :] * dsk_ref[...]) * _silu(z_ref[sl, :])
        y = y * lax.rsqrt(jnp.mean(y * y, -1, keepdims=True) + RMS_EPS)
        o_ref[sl, :] = (y * nrm_ref[...]).astype(BF16)

    if want_fin:
        for d in range(2):
            st = st_ref[d]
            for hh in range(hpg):
                fin_ref[d, hh] = st[:, hh * SSM_HEAD_DIM:(hh + 1) * SSM_HEAD_DIM]


def _ssd(xc, u1, dt, la, lat, dsk, nrm, seq, l, s0):
    tp = xc.shape[0]
    nb = tp // seq
    nc = seq // CHUNK
    gw = SSM_D_INNER // SSM_GROUPS
    hpg = SSM_HEADS // SSM_GROUPS
    has_s0 = s0 is not None
    want_fin = not has_s0
    in_specs = [pl.BlockSpec((seq, gw), lambda b, g: (b, g)),
                pl.BlockSpec((seq, SSM_STATE), lambda b, g: (b, SSM_D_INNER // SSM_STATE + g)),
                pl.BlockSpec((seq, SSM_STATE), lambda b, g: (b, SSM_D_INNER // SSM_STATE + SSM_GROUPS + g)),
                pl.BlockSpec((seq, gw), lambda b, g: (b, 1536 // gw + g)),
                pl.BlockSpec((None, seq, 16), lambda b, g: (g, b, 0)),
                pl.BlockSpec((None, seq, 16), lambda b, g: (g, b, 0)),
                pl.BlockSpec((None, nc, 16, CHUNK), lambda b, g: (g, b, 0, 0)),
                pl.BlockSpec((1, gw), lambda b, g: (0, g)),
                pl.BlockSpec((1, gw), lambda b, g: (0, g))]
    args = [xc, xc, xc, u1, dt, la, lat, dsk, nrm]
    if has_s0:
        in_specs.append(pl.BlockSpec((None, None, 2, hpg, SSM_STATE, SSM_HEAD_DIM),
                                     lambda b, g: (b, l, 0, g, 0, 0)))
        args.append(s0)
    out_specs = [pl.BlockSpec((seq, gw), lambda b, g: (b, g))]
    out_shape = [jax.ShapeDtypeStruct((tp, SSM_D_INNER), BF16)]
    if want_fin:
        out_specs.append(pl.BlockSpec((None, 2, hpg, SSM_STATE, SSM_HEAD_DIM), lambda b, g: (b, 0, g, 0, 0)))
        out_shape.append(jax.ShapeDtypeStruct((nb, 2, SSM_HEADS, SSM_STATE, SSM_HEAD_DIM), F32))
    res = pl.pallas_call(
        functools.partial(_ssd_kernel, seq=seq, has_s0=has_s0, want_fin=want_fin),
        grid=(nb, SSM_GROUPS),
        in_specs=in_specs,
        out_specs=out_specs,
        out_shape=out_shape,
        scratch_shapes=[pltpu.VMEM((2, SSM_STATE, gw), F32), pltpu.VMEM((seq, gw), F32)],
        compiler_params=_cp("arbitrary", "arbitrary"),
        name="ssd_scan",
    )(*args)
    return (res[0], res[1]) if want_fin else (res[0], None)


def _ret_kernel(*refs, seq, l, latent):
    dec_ref, q_ref, k_ref, v_ref, g_ref = refs[:5]
    pos = 5
    if latent:
        cos_ref, sa_ref, sb_ref, s0_ref = refs[pos:pos + 4]
        pos += 4
    o_ref = refs[pos]
    pos += 1
    fin_ref = None
    if not latent:
        fin_ref = refs[pos]
        pos += 1
    qs_ref, ks_ref, st_ref, y_ref = refs[pos:pos + 4]

    nc = seq // CHUNK
    hd = pl.program_id(1)
    if latent:
        qs_ref[...] = _rope(q_ref[...], cos_ref[...], sa_ref[...], sb_ref[...]).astype(BF16)
        ks_ref[...] = _rope(k_ref[...] * (RET_QK_DIM ** -0.5), cos_ref[...], sa_ref[...], sb_ref[...])
        for d in range(2):
            st_ref[d] = s0_ref[d]
    else:
        qs_ref[...] = q_ref[...].astype(BF16)
        ks_ref[...] = k_ref[...] * (RET_QK_DIM ** -0.5)
        for d in range(2):
            st_ref[d] = jnp.zeros((RET_QK_DIM, RET_V_DIM), F32)

    ri = lax.broadcasted_iota(I32, (CHUNK, CHUNK), 0)
    ci = lax.broadcasted_iota(I32, (CHUNK, CHUNK), 1)
    dist = (ri - ci).astype(F32)
    rowi = lax.broadcasted_iota(I32, (CHUNK, 1), 0).astype(F32)

    def direction(d):
        raw = jnp.full((1, 1), dec_ref[l * 2 * RET_HEADS + d * RET_HEADS + hd], F32)
        lg = -_softplus(-raw)
        if d == 0:
            dec = jnp.exp(jnp.where(ri >= ci, dist * lg, -jnp.inf))
            e_in = jnp.exp((rowi + 1.0) * lg)
            wgt = jnp.exp((CHUNK - 1.0 - rowi) * lg)
        else:
            dec = jnp.exp(jnp.where(ri <= ci, -dist * lg, -jnp.inf))
            e_in = jnp.exp((CHUNK - rowi) * lg)
            wgt = jnp.exp(rowi * lg)
        full = jnp.exp(CHUNK * lg)

        def body(c, carry):
            cc = c if d == 0 else nc - 1 - c
            r0 = pl.multiple_of(cc * CHUNK, CHUNK)
            q = qs_ref[pl.ds(r0, CHUNK), :]
            k = ks_ref[pl.ds(r0, CHUNK), :]
            v = v_ref[pl.ds(r0, CHUNK), :].astype(BF16)
            g = _dot_nt(q, k.astype(BF16))
            st = st_ref[d]
            y = _dot((g * dec).astype(BF16), v) + e_in * _dot(q, st.astype(BF16))
            if d == 0:
                y_ref[pl.ds(r0, CHUNK), :] = y
            else:
                y_ref[pl.ds(r0, CHUNK), :] += y
            st_ref[d] = st * full + _dot((k * wgt).T.astype(BF16), v)
            return carry

        lax.fori_loop(0, nc, body, 0)

    direction(0)
    direction(1)

    for r in range(nc):
        sl = slice(r * CHUNK, (r + 1) * CHUNK)
        y = y_ref[sl, :]
        y = y * lax.rsqrt(jnp.mean(y * y, -1, keepdims=True) + RMS_EPS)
        o_ref[sl, :] = (y * _silu(g_ref[sl, :])).astype(BF16)

    if not latent:
        for d in range(2):
            fin_ref[d] = st_ref[d]


def _retention(u2, ret_decay, seq, l, rope, s0):
    tp = u2.shape[0]
    nb = tp // seq
    latent = s0 is not None
    in_specs = [pl.BlockSpec((seq, RET_QK_DIM), lambda b, h, *_: (b, h)),
                pl.BlockSpec((seq, RET_QK_DIM), lambda b, h, *_: (b, RET_HEADS + h)),
                pl.BlockSpec((seq, RET_V_DIM), lambda b, h, *_: (b, 1024 // RET_V_DIM + h)),
                pl.BlockSpec((seq, RET_V_DIM), lambda b, h, *_: (b, 2048 // RET_V_DIM + h))]
    args = [u2, u2, u2, u2]
    if latent:
        for t in rope:
            in_specs.append(pl.BlockSpec((seq, HEAD_DIM), lambda b, h, *_: (0, 0)))
            args.append(t)
        in_specs.append(pl.BlockSpec((None, None, 2, None, RET_QK_DIM, RET_V_DIM),
                                     lambda b, h, *_: (b, l, 0, h, 0, 0)))
        args.append(s0)
    out_specs = [pl.BlockSpec((seq, RET_V_DIM), lambda b, h, *_: (b, h))]
    out_shape = [jax.ShapeDtypeStruct((tp, RET_HEADS * RET_V_DIM), BF16)]
    if not latent:
        out_specs.append(pl.BlockSpec((None, 2, None, RET_QK_DIM, RET_V_DIM), lambda b, h, *_: (b, 0, h, 0, 0)))
        out_shape.append(jax.ShapeDtypeStruct((nb, 2, RET_HEADS, RET_QK_DIM, RET_V_DIM), F32))
    res = pl.pallas_call(
        functools.partial(_ret_kernel, seq=seq, l=l, latent=latent),
        grid_spec=pltpu.PrefetchScalarGridSpec(
            num_scalar_prefetch=1,
            grid=(nb, RET_HEADS),
            in_specs=in_specs,
            out_specs=out_specs,
            scratch_shapes=[pltpu.VMEM((seq, RET_QK_DIM), BF16),
                            pltpu.VMEM((seq, RET_QK_DIM), F32),
                            pltpu.VMEM((2, RET_QK_DIM, RET_V_DIM), F32),
                            pltpu.VMEM((seq, RET_V_DIM), F32)]),
        out_shape=out_shape,
        compiler_params=_cp("arbitrary", "arbitrary"),
        name="retention",
    )(ret_decay.reshape(-1), *args)
    return (res[0], None) if latent else (res[0], res[1])


def _softmax_pv(s, v, sink):
    m = jnp.max(s, -1, keepdims=True)
    if sink is not None:
        m = jnp.maximum(m, sink)
    e = jnp.exp(s - m)
    den = jnp.sum(e, -1, keepdims=True)
    if sink is not None:
        den = den + jnp.exp(sink - m)
    return _dot(e.astype(BF16), v) / den


def _ctx_attn_kernel(sink_ref, qa_ref, ka_ref, va_ref, qn_ref, kn_ref, vn_ref, ya_ref, yn_ref, *, l):
    scale = HEAD_DIM ** -0.5
    for h in range(WIN_HEADS):
        kv = h // WIN_GROUP
        hs = slice(h * HEAD_DIM, (h + 1) * HEAD_DIM)
        ks = slice(kv * HEAD_DIM, (kv + 1) * HEAD_DIM)
        s = _dot_nt(qa_ref[:, hs].astype(BF16), ka_ref[:, ks].astype(BF16)) * scale
        ya_ref[:, hs] = _softmax_pv(s, va_ref[:, ks].astype(BF16), sink_ref[l * WIN_HEADS + h]).astype(BF16)
    for h in range(NA_HEADS):
        hs = slice(h * HEAD_DIM, (h + 1) * HEAD_DIM)
        s = _dot_nt(qn_ref[:, hs].astype(BF16), kn_ref[:, hs].astype(BF16)) * scale
        yn_ref[:, hs] = _softmax_pv(s, vn_ref[:, hs].astype(BF16), None).astype(BF16)


def _ctx_attention(u1, u2, win_sink, seq, l):
    tp = u1.shape[0]
    nb = tp // seq
    kvw = WIN_KV_HEADS * HEAD_DIM
    hw = NA_HEADS * HEAD_DIM
    return pl.pallas_call(
        functools.partial(_ctx_attn_kernel, l=l),
        grid_spec=pltpu.PrefetchScalarGridSpec(
            num_scalar_prefetch=1,
            grid=(nb,),
            in_specs=[pl.BlockSpec((seq, hw), lambda b, *_: (b, 0)),
                      pl.BlockSpec((seq, kvw), lambda b, *_: (b, 1024 // kvw)),
                      pl.BlockSpec((seq, kvw), lambda b, *_: (b, 1280 // kvw)),
                      pl.BlockSpec((seq, hw), lambda b, *_: (b, 3072 // hw)),
                      pl.BlockSpec((seq, hw), lambda b, *_: (b, 4096 // hw)),
                      pl.BlockSpec((seq, hw), lambda b, *_: (b, 5120 // hw))],
            out_specs=[pl.BlockSpec((seq, hw), lambda b, *_: (b, 0)),
                       pl.BlockSpec((seq, hw), lambda b, *_: (b, 0))]),
        out_shape=[jax.ShapeDtypeStruct((tp, hw), BF16), jax.ShapeDtypeStruct((tp, hw), BF16)],
        compiler_params=_cp("arbitrary"),
        name="ctx_attention",
    )(win_sink.reshape(-1), u1, u1, u1, u2, u2, u2)


def _win_kernel(sink_ref, q_ref, k_ref, v_ref, kc_ref, vc_ref, cos_ref, sa_ref, sb_ref, o_ref,
                kr_ref, vb_ref, *, seq, l):
    scale = HEAD_DIM ** -0.5
    kvh = pl.program_id(1)
    nq = seq // CHUNK
    band = 3 * CHUNK
    rows = WIN_GROUP * CHUNK
    kr_ref[...] = _rope(k_ref[...], cos_ref[...], sa_ref[...], sb_ref[...]).astype(BF16)
    vb_ref[...] = v_ref[...].astype(BF16)
    kc = kc_ref[...].astype(BF16)
    vc = vc_ref[...].astype(BF16)
    rg = lax.broadcasted_iota(I32, (rows, 1), 0) // CHUNK
    sink = jnp.zeros((rows, 1), F32)
    for g in range(WIN_GROUP):
        sink = jnp.where(rg == g, sink_ref[l * WIN_HEADS + kvh * WIN_GROUP + g], sink)
    qoff = lax.broadcasted_iota(I32, (rows, band), 0) % CHUNK
    koff = lax.broadcasted_iota(I32, (rows, band), 1)

    def body(n, carry):
        r0 = pl.multiple_of(n * CHUNK, CHUNK)
        cos = cos_ref[pl.ds(r0, CHUNK), :]
        sa = sa_ref[pl.ds(r0, CHUNK), :]
        sb = sb_ref[pl.ds(r0, CHUNK), :]
        qs = jnp.concatenate(
            [_rope(q_ref[pl.ds(r0, CHUNK), g * HEAD_DIM:(g + 1) * HEAD_DIM], cos, sa, sb)
             for g in range(WIN_GROUP)], axis=0).astype(BF16)
        start = pl.multiple_of(jnp.clip((n - 1) * CHUNK, 0, seq - band), CHUNK)
        kb = kr_ref[pl.ds(start, band), :]
        vb = vb_ref[pl.ds(start, band), :]
        s_loc = _dot_nt(qs, kb) * scale
        ok = jnp.abs(r0 + qoff - (start + koff)) <= WINDOW
        s_loc = jnp.where(ok, s_loc, -jnp.inf)
        s_ctx = _dot_nt(qs, kc) * scale
        m = jnp.maximum(jnp.maximum(jnp.max(s_loc, -1, keepdims=True), jnp.max(s_ctx, -1, keepdims=True)), sink)
        e_loc = jnp.exp(s_loc - m)
        e_ctx = jnp.exp(s_ctx - m)
        den = jnp.sum(e_loc, -1, keepdims=True) + jnp.sum(e_ctx, -1, keepdims=True) + jnp.exp(sink - m)
        o = (_dot(e_ctx.astype(BF16), vc) + _dot(e_loc.astype(BF16), vb)) / den
        for g in range(WIN_GROUP):
            o_ref[pl.ds(r0, CHUNK), g * HEAD_DIM:(g + 1) * HEAD_DIM] = o[g * CHUNK:(g + 1) * CHUNK, :].astype(BF16)
        return carry

    lax.fori_loop(0, nq, body, 0)


def _win_attention(u1, cache_k, cache_v, win_sink, rope, seq, l):
    tp = u1.shape[0]
    nb = tp // seq
    past = cache_k.shape[2]
    gw = WIN_GROUP * HEAD_DIM
    ck = cache_k.reshape(nb, DEPTH, past, WIN_KV_HEADS * HEAD_DIM)
    cv = cache_v.reshape(nb, DEPTH, past, WIN_KV_HEADS * HEAD_DIM)
    rope_spec = pl.BlockSpec((seq, HEAD_DIM), lambda b, h, *_: (0, 0))
    return pl.pallas_call(
        functools.partial(_win_kernel, seq=seq, l=l),
        grid_spec=pltpu.PrefetchScalarGridSpec(
            num_scalar_prefetch=1,
            grid=(nb, WIN_KV_HEADS),
            in_specs=[pl.BlockSpec((seq, gw), lambda b, h, *_: (b, h)),
                      pl.BlockSpec((seq, HEAD_DIM), lambda b, h, *_: (b, 1024 // HEAD_DIM + h)),
                      pl.BlockSpec((seq, HEAD_DIM), lambda b, h, *_: (b, 1280 // HEAD_DIM + h)),
                      pl.BlockSpec((None, None, past, HEAD_DIM), lambda b, h, *_: (b, l, 0, h)),
                      pl.BlockSpec((None, None, past, HEAD_DIM), lambda b, h, *_: (b, l, 0, h)),
                      rope_spec, rope_spec, rope_spec],
            out_specs=pl.BlockSpec((seq, gw), lambda b, h, *_: (b, h)),
            scratch_shapes=[pltpu.VMEM((seq, HEAD_DIM), BF16), pltpu.VMEM((seq, HEAD_DIM), BF16)]),
        out_shape=jax.ShapeDtypeStruct((tp, WIN_HEADS * HEAD_DIM), BF16),
        compiler_params=_cp("arbitrary", "arbitrary"),
        name="win_attention",
    )(win_sink.reshape(-1), u1, u1, u1, ck, cv, *rope)


def _na_window_start(j, rows):
    kr = min(NA_ROWS, rows)
    rs = jnp.clip(2 * j - kr // 2, 0, rows - kr)
    return jnp.minimum(rs, rows - NA_WIN_ROWS)


def _na_kernel(q_ref, k_ref, v_ref, kc_ref, vc_ref, bias_ref, o_ref, kb_ref, vb_ref, *, seq):
    scale = HEAD_DIM ** -0.5
    rows = seq // GRID_W
    nq = seq // CHUNK
    win = NA_WIN_ROWS * GRID_W
    kb_ref[...] = k_ref[...].astype(BF16)
    vb_ref[...] = v_ref[...].astype(BF16)
    kc = kc_ref[...].astype(BF16)
    vc = vc_ref[...].astype(BF16)

    def body(j, carry):
        r0 = pl.multiple_of(j * CHUNK, CHUNK)
        q = q_ref[pl.ds(r0, CHUNK), :].astype(BF16)
        start = pl.multiple_of(_na_window_start(j, rows) * GRID_W, GRID_W)
        kw = kb_ref[pl.ds(start, win), :]
        vw = vb_ref[pl.ds(start, win), :]
        s_loc = _dot_nt(q, kw) * scale + bias_ref[j]
        s_ctx = _dot_nt(q, kc) * scale
        m = jnp.maximum(jnp.max(s_loc, -1, keepdims=True), jnp.max(s_ctx, -1, keepdims=True))
        e_loc = jnp.exp(s_loc - m)
        e_ctx = jnp.exp(s_ctx - m)
        den = jnp.sum(e_loc, -1, keepdims=True) + jnp.sum(e_ctx, -1, keepdims=True)
        o = (_dot(e_ctx.astype(BF16), vc) + _dot(e_loc.astype(BF16), vw)) / den
        o_ref[pl.ds(r0, CHUNK), :] = o.astype(BF16)
        return carry

    lax.fori_loop(0, nq, body, 0)


def _na_bias_table(rpb, seq):
    nh = rpb.shape[0]
    rows = seq // GRID_W
    kr = min(NA_ROWS, rows)
    nq = seq // CHUNK
    ndr, ndc = 2 * NA_ROWS - 1, 2 * NA_COLS - 1
    halves = CHUNK // GRID_W
    qc = np.arange(GRID_W)[:, None]
    kc = np.arange(GRID_W)[None, :]
    dc = (np.clip(kc - qc, 1 - NA_COLS, NA_COLS - 1) + (NA_COLS - 1)).reshape(-1)
    cs = np.clip(qc - NA_COLS // 2, 0, GRID_W - NA_COLS)
    col_ok = ((kc >= cs) & (kc < cs + NA_COLS)).reshape(-1)
    onehot = (dc[None, :] == np.arange(ndc)[:, None]).astype(np.float32)
    blocks = jnp.einsum('hrd,dq->hrq', rpb, onehot, precision=lax.Precision.HIGHEST)
    blocks = jnp.where(col_ok[None, None, :], blocks, NEG_BIG)
    blocks = jnp.concatenate([blocks, jnp.full((nh, 1, GRID_W * GRID_W), NEG_BIG, F32)], axis=1)
    blocks = blocks.reshape(nh, ndr + 1, GRID_W, GRID_W)
    j = np.arange(nq)[:, None, None]
    r = 2 * j + np.arange(halves)[None, :, None]
    ws = np.minimum(np.clip(2 * j - kr // 2, 0, rows - kr), rows - NA_WIN_ROWS)
    krow = ws + np.arange(NA_WIN_ROWS)[None, None, :]
    rs = np.clip(r - kr // 2, 0, rows - kr)
    blk = np.where((krow >= rs) & (krow < rs + kr), krow - r + (NA_ROWS - 1), ndr)
    t = jnp.take(blocks, jnp.asarray(blk.reshape(-1), I32), axis=1)
    t = t.reshape(nh, nq, halves, NA_WIN_ROWS, GRID_W, GRID_W).transpose(0, 1, 2, 4, 3, 5)
    return t.reshape(nh, nq, CHUNK, NA_WIN_ROWS * GRID_W)


def _na_attention(u2, cache_k, cache_v, bias, seq, l):
    tp = u2.shape[0]
    nb = tp // seq
    past = cache_k.shape[2]
    nq = seq // CHUNK
    win = NA_WIN_ROWS * GRID_W
    ck = cache_k.reshape(nb, DEPTH, past, NA_HEADS * HEAD_DIM)
    cv = cache_v.reshape(nb, DEPTH, past, NA_HEADS * HEAD_DIM)
    return pl.pallas_call(
        functools.partial(_na_kernel, seq=seq),
        grid=(nb, NA_HEADS),
        in_specs=[pl.BlockSpec((seq, HEAD_DIM), lambda b, h: (b, 3072 // HEAD_DIM + h)),
                  pl.BlockSpec((seq, HEAD_DIM), lambda b, h: (b, 4096 // HEAD_DIM + h)),
                  pl.BlockSpec((seq, HEAD_DIM), lambda b, h: (b, 5120 // HEAD_DIM + h)),
                  pl.BlockSpec((None, None, past, HEAD_DIM), lambda b, h: (b, l, 0, h)),
                  pl.BlockSpec((None, None, past, HEAD_DIM), lambda b, h: (b, l, 0, h)),
                  pl.BlockSpec((None, nq, CHUNK, win), lambda b, h: (h, 0, 0, 0))],
        out_specs=pl.BlockSpec((seq, HEAD_DIM), lambda b, h: (b, h)),
        out_shape=jax.ShapeDtypeStruct((tp, NA_HEADS * HEAD_DIM), BF16),
        scratch_shapes=[pltpu.VMEM((seq, HEAD_DIM), BF16), pltpu.VMEM((seq, HEAD_DIM), BF16)],
        compiler_params=_cp("arbitrary", "arbitrary"),
        name="na_attention",
    )(u2, u2, u2, ck, cv, bias)


def _merge_kernel(ya_ref, ys_ref, yr_ref, yn_ref, g0_ref, g1_ref, g2_ref, g3_ref, w_ref, o_ref, wbf):
    @pl.when(pl.program_id(1) == 0)
    def _():
        wbf[...] = w_ref[...].astype(BF16)

    acc = None
    for n, (b_ref, g_ref) in enumerate(((ya_ref, g0_ref), (ys_ref, g1_ref), (yr_ref, g2_ref), (yn_ref, g3_ref))):
        t = jax.nn.sigmoid(g_ref[...]) * _dot(b_ref[...], wbf[n])
        acc = t if acc is None else acc + t
    o_ref[...] = acc.astype(BF16)


def _merge(ya, ys, yr, yn, u2, w_branch, l):
    tp = ya.shape[0]
    tm, tn = 512, 512
    g_off = 6144 // tn
    per = D_MODEL // tn
    br = pl.BlockSpec((tm, BRANCH_W), lambda j, i: (i, 0))
    gate = lambda n: pl.BlockSpec((tm, tn), lambda j, i: (i, g_off + n * per + j))
    return pl.pallas_call(
        _merge_kernel,
        grid=(per, tp // tm),
        in_specs=[br, br, br, br, gate(0), gate(1), gate(2), gate(3),
                  pl.BlockSpec((None, N_BRANCH, BRANCH_W, tn), lambda j, i: (l, 0, 0, j))],
        out_specs=pl.BlockSpec((tm, tn), lambda j, i: (i, j)),
        out_shape=jax.ShapeDtypeStruct((tp, D_MODEL), BF16),
        scratch_shapes=[pltpu.VMEM((N_BRANCH, BRANCH_W, tn), BF16)],
        compiler_params=_cp("arbitrary", "arbitrary"),
        name="merge",
    )(ya, ys, yr, yn, u2, u2, u2, u2, w_branch)


def _out_kernel(m_ref, w_ref, x_ref, mod_ref, g_ref, b_ref, o_ref):
    mix = _dot(m_ref[...], w_ref[...])
    y = ALPHA * x_ref[...] + mod_ref[2:3, :] * mix
    o_ref[...] = _layer_norm(y, g_ref[...], b_ref[...])


def _out_proj(merged, w_out_bf, x, mod, ln_g, ln_b, l, row_fn):
    tp = x.shape[0]
    tm = TOK_TILE
    row = pl.BlockSpec((tm, D_MODEL), lambda i: (i, 0))
    vec = pl.BlockSpec((None, 1, D_MODEL), lambda i: (l, 0, 0))
    return pl.pallas_call(
        _out_kernel,
        grid=(tp // tm,),
        in_specs=[row,
                  pl.BlockSpec((None, D_MODEL, D_MODEL), lambda i: (l, 0, 0)),
                  row,
                  pl.BlockSpec((None, None, 6, D_MODEL), lambda i: (l, row_fn(i), 0, 0)),
                  vec, vec],
        out_specs=row,
        out_shape=jax.ShapeDtypeStruct((tp, D_MODEL), F32),
        compiler_params=_cp("arbitrary"),
        name="out_proj_ln1",
    )(merged, w_out_bf, x, mod, ln_g.reshape(DEPTH, 1, D_MODEL), ln_b.reshape(DEPTH, 1, D_MODEL))


def _pack_pair(a, b):
    ab = lax.bitcast_convert_type(a.astype(BF16).astype(F32), jnp.uint32)
    bb = lax.bitcast_convert_type(b.astype(BF16).astype(F32), jnp.uint32)
    return (ab & jnp.uint32(0xFFFF0000)) | (bb >> 16)


def _unpack_pair(p):
    a = lax.bitcast_convert_type(p & jnp.uint32(0xFFFF0000), F32)
    b = lax.bitcast_convert_type(p << 16, F32)
    return a, b


HALF = D_MODEL // 2
ROW_TILES = HALF // 128


def _store_rows(ref, packed):
    n = packed.shape[0]
    for c in range(ROW_TILES):
        ref[pl.ds(c, n, stride=ROW_TILES), :] = packed[:, c * 128:(c + 1) * 128]


def _load_rows(ref):
    n = ref.shape[0] // ROW_TILES
    return jnp.concatenate([ref[pl.ds(c, n, stride=ROW_TILES), :] for c in range(ROW_TILES)], axis=1)


def _row(ref, start):
    return ref.at[pl.ds(pl.multiple_of(start, ROW_TILES), ROW_TILES), :]


def _router_kernel(xc_ref, xl_ref, mod_ref, rw_ref, rb_ref, xp_ref, idx_ref, wt_ref, pos_ref, cnt_ref, carry,
                   *, n_ctx_tiles):
    i = pl.program_id(0)

    @pl.when(i == 0)
    def _():
        carry[...] = jnp.zeros(carry.shape, F32)

    x = jnp.where(i < n_ctx_tiles, xc_ref[...], xl_ref[...])
    xm = x * (1.0 + mod_ref[4:5, :]) + mod_ref[3:4, :]
    _store_rows(xp_ref, _pack_pair(xm[:, :HALF], xm[:, HALF:]))
    tm = xm.shape[0]
    scores = jax.nn.sigmoid(_dot_nt(rw_ref[...], xm.astype(BF16)))
    biased = scores + rb_ref[...]
    per = N_EXPERTS // N_EXPERT_GROUPS
    sub = lax.broadcasted_iota(I32, (per, tm), 0)
    bg = [biased[per * g:per * (g + 1), :] for g in range(N_EXPERT_GROUPS)]
    sg = [scores[per * g:per * (g + 1), :] for g in range(N_EXPERT_GROUPS)]
    gs = []
    for g in range(N_EXPERT_GROUPS):
        m1 = jnp.max(bg[g], 0, keepdims=True)
        i1 = jnp.min(jnp.where(bg[g] == m1, sub, per), 0, keepdims=True)
        m2 = jnp.max(jnp.where(sub == i1, -jnp.inf, bg[g]), 0, keepdims=True)
        gs.append(m1 + m2)
    cand = []
    for g in range(N_EXPERT_GROUPS):
        rank = jnp.zeros((1, tm), I32)
        for g2 in range(N_EXPERT_GROUPS):
            if g2 == g:
                continue
            ahead = (gs[g2] > gs[g]) | ((gs[g2] == gs[g]) & (g2 < g))
            rank = rank + ahead.astype(I32)
        cand.append(jnp.where(rank < TOPK_GROUPS, bg[g], -jnp.inf))
    eidx = [sub + per * g for g in range(N_EXPERT_GROUPS)]
    idx_rows, w_rows = [], []
    for _ in range(TOP_K):
        m = None
        for g in range(N_EXPERT_GROUPS):
            t = jnp.max(cand[g], 0, keepdims=True)
            m = t if m is None else jnp.maximum(m, t)
        ik = None
        for g in range(N_EXPERT_GROUPS):
            t = jnp.min(jnp.where(cand[g] == m, eidx[g], N_EXPERTS), 0, keepdims=True)
            ik = t if ik is None else jnp.minimum(ik, t)
        wk = jnp.zeros((1, tm), F32)
        for g in range(N_EXPERT_GROUPS):
            hit = eidx[g] == ik
            wk = wk + jnp.sum(jnp.where(hit, sg[g], 0.0), 0, keepdims=True)
            cand[g] = jnp.where(hit, -jnp.inf, cand[g])
        idx_rows.append(ik)
        w_rows.append(wk)
    tot = w_rows[0]
    for wk in w_rows[1:]:
        tot = tot + wk
    idx_ref[...] = jnp.concatenate(idx_rows, axis=0)
    wt_ref[...] = jnp.concatenate([wk / tot * ROUTED_SCALE for wk in w_rows], axis=0)

    member = []
    for g in range(N_EXPERT_GROUPS):
        sel = eidx[g] == idx_rows[0]
        for k in range(1, TOP_K):
            sel = sel | (eidx[g] == idx_rows[k])
        member.append(jnp.where(sel, 1.0, 0.0))
    member = jnp.concatenate(member, axis=0)
    earlier = (lax.broadcasted_iota(I32, (tm, tm), 0) < lax.broadcasted_iota(I32, (tm, tm), 1))
    before = carry[...] + _dot(member.astype(BF16), jnp.where(earlier, 1.0, 0.0).astype(BF16))
    pos_rows = []
    for k in range(TOP_K):
        pk = jnp.zeros((1, tm), F32)
        for g in range(N_EXPERT_GROUPS):
            pk = pk + jnp.sum(jnp.where(eidx[g] == idx_rows[k], before[per * g:per * (g + 1), :], 0.0),
                              0, keepdims=True)
        pos_rows.append(pk)
    pos_ref[...] = jnp.concatenate(pos_rows, axis=0).astype(I32)
    carry[...] = carry[...] + jnp.sum(member, axis=1, keepdims=True)
    cnt_ref[...] = carry[...]


def _router(x1c, x1l, mod, rwt_bf, rbias, l, n_lat_tiles_per_seq):
    tc, tl = x1c.shape[0], x1l.shape[0]
    tm = TOK_TILE
    nct, nlt = tc // tm, tl // tm
    t = tc + tl
    row_fn = lambda i: jnp.where(i < nct, 0, 1 + (i - nct) // n_lat_tiles_per_seq)
    return pl.pallas_call(
        functools.partial(_router_kernel, n_ctx_tiles=nct),
        grid=(nct + nlt,),
        in_specs=[pl.BlockSpec((tm, D_MODEL), lambda i: (jnp.minimum(i, nct - 1), 0)),
                  pl.BlockSpec((tm, D_MODEL), lambda i: (jnp.maximum(i - nct, 0), 0)),
                  pl.BlockSpec((None, None, 6, D_MODEL), lambda i: (l, row_fn(i), 0, 0)),
                  pl.BlockSpec((N_EXPERTS, D_MODEL), lambda i: (0, 0)),
                  pl.BlockSpec((N_EXPERTS, 1), lambda i: (0, 0))],
        out_specs=[pl.BlockSpec((tm * ROW_TILES, 128), lambda i: (i, 0)),
                   pl.BlockSpec((TOP_K, tm), lambda i: (0, i)),
                   pl.BlockSpec((TOP_K, tm), lambda i: (0, i)),
                   pl.BlockSpec((TOP_K, tm), lambda i: (0, i)),
                   pl.BlockSpec((N_EXPERTS, 1), lambda i: (0, 0))],
        out_shape=[jax.ShapeDtypeStruct((t * ROW_TILES, 128), jnp.uint32),
                   jax.ShapeDtypeStruct((TOP_K, t), I32),
                   jax.ShapeDtypeStruct((TOP_K, t), F32),
                   jax.ShapeDtypeStruct((TOP_K, t), I32),
                   jax.ShapeDtypeStruct((N_EXPERTS, 1), F32)],
        scratch_shapes=[pltpu.VMEM((N_EXPERTS, 1), F32)],
        compiler_params=_cp("arbitrary"),
        name="router",
    )(x1c, x1l, mod, rwt_bf, rbias.reshape(N_EXPERTS, 1))


DMA_UNROLL = 8
N_PAD_SLOTS = N_EXPERTS * MOE_BM


def _dispatch_kernel(pad_ref, dst_ref, xp_ref, xs_hbm, zrow, sem, zsem):
    tm = xp_ref.shape[0] // ROW_TILES
    n = TOP_K * tm

    @pl.when(pl.program_id(0) == 0)
    def _():
        zrow[...] = jnp.zeros(zrow.shape, jnp.uint32)

        def z_issue(s, carry):
            pltpu.make_async_copy(zrow, _row(xs_hbm, pad_ref[s]), zsem).start()
            return carry

        def z_drain(s, carry):
            pltpu.make_async_copy(zrow, _row(xs_hbm, 0), zsem).wait()
            return carry

        lax.fori_loop(0, N_PAD_SLOTS, z_issue, 0, unroll=DMA_UNROLL)
        lax.fori_loop(0, N_PAD_SLOTS, z_drain, 0, unroll=DMA_UNROLL)

    for k in range(TOP_K):
        def issue(r, carry, k=k):
            pltpu.make_async_copy(_row(xp_ref, r * ROW_TILES), _row(xs_hbm, dst_ref[0, k * tm + r]), sem).start()
            return carry

        lax.fori_loop(0, tm, issue, 0, unroll=DMA_UNROLL)

    def drain(s, carry):
        pltpu.make_async_copy(_row(xp_ref, 0), _row(xs_hbm, 0), sem).wait()
        return carry

    lax.fori_loop(0, n, drain, 0, unroll=DMA_UNROLL)


def _dispatch(xp, dest_tiles, pad_dst, n_rows):
    t = xp.shape[0] // ROW_TILES
    tm = TOK_TILE
    return pl.pallas_call(
        _dispatch_kernel,
        grid_spec=pltpu.PrefetchScalarGridSpec(
            num_scalar_prefetch=1,
            grid=(t // tm,),
            in_specs=[pl.BlockSpec((None, 1, TOP_K * tm), lambda i, pad: (i, 0, 0), memory_space=pltpu.SMEM),
                      pl.BlockSpec((tm * ROW_TILES, 128), lambda i, pad: (i, 0))],
            out_specs=pl.BlockSpec(memory_space=pl.ANY),
            scratch_shapes=[pltpu.VMEM((ROW_TILES, 128), jnp.uint32),
                            pltpu.SemaphoreType.DMA(()), pltpu.SemaphoreType.DMA(())]),
        out_shape=jax.ShapeDtypeStruct((n_rows * ROW_TILES, 128), jnp.uint32),
        compiler_params=_cp("arbitrary"),
        name="dispatch",
    )(pad_dst, dest_tiles, xp)


def _experts_kernel(be_ref, nu_ref, x_ref, wgu_ref, wdn_ref, y_ref, wgu_bf, wdn_bf):
    i = pl.program_id(0)

    @pl.when(i < nu_ref[0])
    def _():
        prev = be_ref[jnp.maximum(i - 1, 0)]

        @pl.when((i == 0) | (be_ref[i] != prev))
        def _():
            wgu_bf[...] = wgu_ref[...].astype(BF16)
            wdn_bf[...] = wdn_ref[...].astype(BF16)

        xa, xb = _unpack_pair(_load_rows(x_ref))
        hgu = _dot(xa.astype(BF16), wgu_bf[:HALF, :]) + _dot(xb.astype(BF16), wgu_bf[HALF:, :])
        act = (_silu(hgu[:, :EXPERT_FF]) * hgu[:, EXPERT_FF:]).astype(BF16)
        y = _dot(act, wdn_bf[...])
        _store_rows(y_ref, _pack_pair(y[:, :HALF], y[:, HALF:]))

    @pl.when(i >= nu_ref[0])
    def _():
        y_ref[...] = jnp.zeros(y_ref.shape, jnp.uint32)


def _experts(xs, blk_e, n_used, w_gu, w_dn, l):
    nblk = blk_e.shape[0]
    last = lambda i, be, nu: jnp.minimum(i, nu[0] - 1)
    return pl.pallas_call(
        _experts_kernel,
        grid_spec=pltpu.PrefetchScalarGridSpec(
            num_scalar_prefetch=2,
            grid=(nblk,),
            in_specs=[pl.BlockSpec((MOE_BM * ROW_TILES, 128), lambda i, be, nu: (last(i, be, nu), 0)),
                      pl.BlockSpec((None, None, D_MODEL, 2 * EXPERT_FF),
                                   lambda i, be, nu: (l, be[last(i, be, nu)], 0, 0)),
                      pl.BlockSpec((None, None, EXPERT_FF, D_MODEL),
                                   lambda i, be, nu: (l, be[last(i, be, nu)], 0, 0))],
            out_specs=pl.BlockSpec((MOE_BM * ROW_TILES, 128), lambda i, be, nu: (i, 0)),
            scratch_shapes=[pltpu.VMEM((D_MODEL, 2 * EXPERT_FF), BF16),
                            pltpu.VMEM((EXPERT_FF, D_MODEL), BF16)]),
        out_shape=jax.ShapeDtypeStruct((nblk * MOE_BM * ROW_TILES, 128), jnp.uint32),
        compiler_params=_cp("arbitrary"),
        name="experts",
    )(blk_e, n_used, xs, w_gu, w_dn)


FIN_TM = 128


def _final_kernel(dcur_ref, dnxt_ref, ys_hbm, x1_ref, xp_ref, wk_ref, mod_ref, g_ref, b_ref, sgu_ref, sdn_ref,
                  o_ref, gbuf, sem):
    n = TOP_K * FIN_TM
    i = pl.program_id(0)
    slot = i % 2

    def fetch(d_ref, sl):
        for k in range(TOP_K):
            def issue(r, carry, k=k):
                pltpu.make_async_copy(_row(ys_hbm, d_ref[0, k * FIN_TM + r]), _row(gbuf.at[sl, k], r * ROW_TILES),
                                      sem.at[sl]).start()
                return carry

            lax.fori_loop(0, FIN_TM, issue, 0, unroll=DMA_UNROLL)

    @pl.when(i == 0)
    def _():
        fetch(dcur_ref, 0)

    @pl.when(i + 1 < pl.num_programs(0))
    def _():
        fetch(dnxt_ref, 1 - slot)

    xa, xb = _unpack_pair(_load_rows(xp_ref))
    hgu = _dot(xa.astype(BF16), sgu_ref[:HALF, :]) + _dot(xb.astype(BF16), sgu_ref[HALF:, :])
    act = (_silu(hgu[:, :SHARED_FF]) * hgu[:, SHARED_FF:]).astype(BF16)
    shared = _dot(act, sdn_ref[...])

    def drain(s, carry):
        pltpu.make_async_copy(_row(ys_hbm, 0), _row(gbuf.at[slot, 0], 0), sem.at[slot]).wait()
        return carry

    lax.fori_loop(0, n, drain, 0, unroll=DMA_UNROLL)

    ra = rb = None
    for k in range(TOP_K):
        ya, yb = _unpack_pair(_load_rows(gbuf.at[slot, k]))
        w = wk_ref[:, k:k + 1]
        ra = w * ya if ra is None else ra + w * ya
        rb = w * yb if rb is None else rb + w * yb
    gate = mod_ref[5:6, :]
    x1 = x1_ref[...]
    ya = ALPHA * x1[:, :HALF] + gate[:, :HALF] * (ra + shared[:, :HALF])
    yb = ALPHA * x1[:, HALF:] + gate[:, HALF:] * (rb + shared[:, HALF:])
    mu = (jnp.sum(ya, -1, keepdims=True) + jnp.sum(yb, -1, keepdims=True)) / D_MODEL
    ya = ya - mu
    yb = yb - mu
    var = (jnp.sum(ya * ya, -1, keepdims=True) + jnp.sum(yb * yb, -1, keepdims=True)) / D_MODEL
    inv = lax.rsqrt(var + LN_EPS)
    o_ref[:, :HALF] = ya * inv * g_ref[:, :HALF] + b_ref[:, :HALF]
    o_ref[:, HALF:] = yb * inv * g_ref[:, HALF:] + b_ref[:, HALF:]


def _final(dest, ys, x1, xp, wk, mod, ln_g, ln_b, sgu_bf, sdn_bf, l, row_fn, tok_off):
    tp = x1.shape[0]
    tm = FIN_TM
    off = tok_off // tm
    nt = tp // tm
    row = pl.BlockSpec((tm, D_MODEL), lambda i: (i, 0))
    vec = pl.BlockSpec((None, 1, D_MODEL), lambda i: (l, 0, 0))
    return pl.pallas_call(
        _final_kernel,
        grid=(nt,),
        in_specs=[pl.BlockSpec((None, 1, TOP_K * tm), lambda i: (off + i, 0, 0), memory_space=pltpu.SMEM),
                  pl.BlockSpec((None, 1, TOP_K * tm), lambda i: (off + jnp.minimum(i + 1, nt - 1), 0, 0),
                               memory_space=pltpu.SMEM),
                  pl.BlockSpec(memory_space=pl.ANY),
                  row,
                  pl.BlockSpec((tm * ROW_TILES, 128), lambda i: (off + i, 0)),
                  pl.BlockSpec((tm, TOP_K), lambda i: (off + i, 0)),
                  pl.BlockSpec((None, None, 6, D_MODEL), lambda i: (l, row_fn(i * tm // TOK_TILE), 0, 0)),
                  vec, vec,
                  pl.BlockSpec((None, D_MODEL, 2 * SHARED_FF), lambda i: (l, 0, 0)),
                  pl.BlockSpec((None, SHARED_FF, D_MODEL), lambda i: (l, 0, 0))],
        out_specs=row,
        out_shape=jax.ShapeDtypeStruct((tp, D_MODEL), F32),
        scratch_shapes=[pltpu.VMEM((2, TOP_K, tm * ROW_TILES, 128), jnp.uint32), pltpu.SemaphoreType.DMA((2,))],
        compiler_params=_cp("arbitrary"),
        name="combine_ln2",
    )(dest, dest, ys, x1, xp, wk, mod, ln_g.reshape(DEPTH, 1, D_MODEL), ln_b.reshape(DEPTH, 1, D_MODEL),
      sgu_bf, sdn_bf)


def _dispatch_plan(idx_t, pos_t, cnt):
    t = idx_t.shape[1]
    nblk = t * TOP_K // MOE_BM + N_EXPERTS
    n_slots = nblk * MOE_BM
    e_ids = jnp.arange(N_EXPERTS, dtype=I32)
    counts = cnt[:, 0].astype(I32)
    padded = (counts + MOE_BM - 1) // MOE_BM * MOE_BM
    incl = e_ids[None, :] <= e_ids[:, None]
    pad_end = jnp.sum(jnp.where(incl, padded[None, :], 0), axis=1)
    pad_start = pad_end - padded
    start_of = jnp.sum(jnp.where(idx_t[:, :, None] == e_ids, pad_start, 0), axis=-1)
    dest = (start_of + pos_t) * ROW_TILES

    def tiles(tm):
        return dest.reshape(TOP_K, t // tm, tm).transpose(1, 0, 2).reshape(t // tm, 1, TOP_K * tm)

    blk_e = jnp.minimum(jnp.sum((pad_end[None, :] <= (jnp.arange(nblk, dtype=I32) * MOE_BM)[:, None]).astype(I32),
                                axis=1), N_EXPERTS - 1)
    n_used = (pad_end[-1] // MOE_BM).reshape(1)
    free = padded - counts
    free_end = jnp.sum(jnp.where(incl, free[None, :], 0), axis=1)
    free_start = free_end - free
    j = jnp.arange(n_slots - t * TOP_K, dtype=I32)
    owner = jnp.sum((free_end[None, :] <= j[:, None]).astype(I32), axis=1)
    own = owner[:, None] == e_ids
    in_pad = jnp.sum(jnp.where(own, (pad_start + counts - free_start)[None, :], 0), axis=1) + j
    in_tail = pad_end[-1] + j - free_end[-1]
    pad_dst = jnp.where(owner < N_EXPERTS, in_pad, in_tail) * ROW_TILES
    return tiles(TOK_TILE), tiles(FIN_TM), pad_dst, blk_e, n_used


def _rope_tables(n_tok):
    t = jnp.arange(n_tok)
    row = (t // GRID_W).astype(F32)
    col = (t % GRID_W).astype(F32)
    half = HEAD_DIM // 2
    inv = ROPE_BASE ** (-jnp.arange(0, half, 2, dtype=F32) / half)
    ar = row[:, None] * inv
    ac = col[:, None] * inv
    ang = jnp.concatenate([ar, ar, ac, ac], -1)
    cos, sin = jnp.cos(ang), jnp.sin(ang)
    quarter = (jnp.arange(HEAD_DIM) // (HEAD_DIM // 4)) % 2
    sin_a = jnp.where(quarter == 0, -sin, 0.0)
    sin_b = jnp.where(quarter == 1, sin, 0.0)
    return cos, sin_a, sin_b


def kernel(x_prompt, x_sample, cache_win_k, cache_win_v, state_ssm, state_ret, cache_na_k, cache_na_v, c, c_ctx, w_ada, b_ada, w_in, win_sink, conv_w, conv_b, dt_bias, a_log, d_skip, ssm_norm, ret_decay, na_rpb, w_branch, w_out, ln1_g, ln1_b, router_w, router_bias, exp_w_gu, exp_w_down, sh_w_gu, sh_w_down, ln2_g, ln2_b):
    bc, lc, _ = x_prompt.shape
    bl, ll, _ = x_sample.shape
    assert lc == TOK_TILE and ll % TOK_TILE == 0
    tc, tl = bc * lc, bl * ll
    lat_tiles = ll // TOK_TILE

    nr = -(-(1 + bl) // 8) * 8
    cond = jnp.zeros((nr, D_MODEL), F32).at[0].set(c_ctx).at[1:1 + bl].set(c)
    mod = _ada(cond, w_ada, b_ada).reshape(DEPTH, nr, 6, D_MODEL)
    row_ctx = lambda i: 0
    row_lat = lambda i: 1 + i // lat_tiles

    rope = _rope_tables(ll)
    perm = np.array([d * SSM_HEADS + g * 8 + hh for g in range(SSM_GROUPS) for d in range(2) for hh in range(8)])
    dsk_full = jnp.repeat(d_skip, SSM_HEAD_DIM, axis=1)
    w_in1 = w_in[:, :, :U1_COLS].astype(BF16)
    w_in2 = w_in[:, :, U2_OFF:].astype(BF16)
    w_dt = w_in[:, :, DT_OFF:U2_OFF][:, :, perm].astype(BF16)
    w_out_bf = w_out.astype(BF16)
    rwt_bf = jnp.swapaxes(router_w, 1, 2).astype(BF16)
    sgu_bf = sh_w_gu.astype(BF16)
    sdn_bf = sh_w_down.astype(BF16)

    xs = {"ctx": x_prompt.reshape(tc, D_MODEL), "lat": x_sample.reshape(tl, D_MODEL)}
    new_ctx = []
    for l in range(DEPTH):
        x1 = {}
        bias_tab = _na_bias_table(na_rpb[l], ll)
        for path in ("ctx", "lat"):
            latent = path == "lat"
            x = xs[path]
            seq = ll if latent else lc
            row_fn = row_lat if latent else row_ctx
            h = _modulate(x, mod, l, row_fn)
            u1 = _in_proj(h, w_in1, l, U1_COLS)
            u2 = _in_proj(h, w_in2, l, U2_COLS)
            dt, la, lat_t = _ssd_steps(h, w_dt[l], dt_bias[l].reshape(-1)[perm], a_log[l].reshape(-1)[perm])
            xc = _conv(u1, conv_w[l], conv_b[l], seq)
            ys, ssm_fin = _ssd(xc, u1, dt, la, lat_t, dsk_full[l:l + 1], ssm_norm[l:l + 1], seq, l,
                               state_ssm if latent else None)
            yr, ret_fin = _retention(u2, ret_decay, seq, l, rope if latent else None,
                                     state_ret if latent else None)
            if latent:
                ya = _win_attention(u1, cache_win_k, cache_win_v, win_sink, rope, seq, l)
                yn = _na_attention(u2, cache_na_k, cache_na_v, bias_tab, seq, l)
            else:
                ya, yn = _ctx_attention(u1, u2, win_sink, seq, l)
                kv = lambda t, nh: t.reshape(bc, lc, nh, HEAD_DIM)
                new_ctx.append((kv(u1[:, 1024:1280], WIN_KV_HEADS), kv(u1[:, 1280:1536], WIN_KV_HEADS),
                                ssm_fin, ret_fin,
                                kv(u2[:, 4096:5120], NA_HEADS), kv(u2[:, 5120:6144], NA_HEADS)))
            merged = _merge(ya, ys, yr, yn, u2, w_branch, l)
            x1[path] = _out_proj(merged, w_out_bf, x, mod, ln1_g, ln1_b, l, row_fn)

        xp, idx_t, w_t, pos_t, cnt = _router(x1["ctx"], x1["lat"], mod, rwt_bf[l], router_bias[l], l, lat_tiles)
        dest_disp, dest_fin, pad_dst, blk_e, n_used = _dispatch_plan(idx_t, pos_t, cnt)
        x_sorted = _dispatch(xp, dest_disp, pad_dst, blk_e.shape[0] * MOE_BM)
        y_sorted = _experts(x_sorted, blk_e, n_used, exp_w_gu, exp_w_down, l)
        wk = w_t.T
        for path, off, row_fn in (("ctx", 0, row_ctx), ("lat", tc, row_lat)):
            xs[path] = _final(dest_fin, y_sorted, x1[path], xp, wk, mod, ln2_g, ln2_b, sgu_bf, sdn_bf,
                              l, row_fn, off)

    stack = lambda i: jnp.stack([t[i] for t in new_ctx], axis=1)
    return (xs["ctx"].reshape(bc, lc, D_MODEL), xs["lat"].reshape(bl, ll, D_MODEL),
            stack(0), stack(1), stack(2), stack(3), stack(4), stack(5))
```

```python
import functools

import jax
import jax.numpy as jnp
import numpy as np
from jax import lax
from jax.experimental import pallas as pl
from jax.experimental.pallas import tpu as pltpu

F32 = jnp.float32
BF16 = jnp.bfloat16
I32 = jnp.int32

D_MODEL = 2048
DEPTH = 2
GRID_W = 64
HEAD_DIM = 128
ROPE_BASE = 10000.0
CHUNK = 128
WIN_HEADS = 8
WIN_KV_HEADS = 2
WIN_GROUP = WIN_HEADS // WIN_KV_HEADS
WINDOW = 128
SSM_D_INNER = D_MODEL // 2
SSM_HEAD_DIM = 64
SSM_HEADS = SSM_D_INNER // SSM_HEAD_DIM
SSM_GROUPS = 2
SSM_STATE = 128
SSM_CONV = 7
SSM_CONV_DIM = SSM_D_INNER + 2 * SSM_GROUPS * SSM_STATE
RET_HEADS = 4
RET_QK_DIM = 128
RET_V_DIM = 256
NA_HEADS = 8
NA_ROWS = 8
NA_COLS = 16
N_BRANCH = 4
BRANCH_W = D_MODEL // 2
N_EXPERTS = 64
TOP_K = 8
N_EXPERT_GROUPS = 8
TOPK_GROUPS = 4
EXPERT_FF = 512
SHARED_FF = 512
ROUTED_SCALE = 2.5
ALPHA = (2.0 * DEPTH) ** 0.25
LN_EPS = 1e-5
RMS_EPS = 1e-6

U1_COLS = 4096
DT_OFF = 4096
U2_OFF = DT_OFF + 2 * SSM_HEADS
U2_COLS = 6144
GATE_OFF = U2_OFF + U2_COLS
GATE_COLS = N_BRANCH * D_MODEL

TOK_TILE = 256
MOE_BM = 256
NA_WIN_ROWS = 10
NEG_BIG = -1e30

VMEM_LIMIT = 52 * 1024 * 1024


def _cp(*sem):
    return pltpu.CompilerParams(dimension_semantics=sem, vmem_limit_bytes=VMEM_LIMIT)


def _dot(a, b):
    return jnp.dot(a, b, preferred_element_type=F32)


def _dot_nt(a, b):
    return lax.dot_general(a, b, (((1,), (1,)), ((), ())), preferred_element_type=F32)


def _dot_exact(a, b):
    return jnp.dot(a, b, preferred_element_type=F32, precision=lax.Precision.HIGHEST)


def _silu(x):
    return x * jax.nn.sigmoid(x)


def _softplus(x):
    return jnp.maximum(x, 0.0) + jnp.log1p(jnp.exp(-jnp.abs(x)))


def _rope(x, cos, sin_a, sin_b):
    return x * cos + pltpu.roll(x, 96, 1) * sin_a + pltpu.roll(x, 32, 1) * sin_b


def _layer_norm(y, g, b):
    mu = jnp.mean(y, -1, keepdims=True)
    yc = y - mu
    var = jnp.mean(yc * yc, -1, keepdims=True)
    return yc * lax.rsqrt(var + LN_EPS) * g + b


def _ada_kernel(c_ref, w_ref, b_ref, o_ref):
    a = _silu(c_ref[...]).astype(BF16)
    o_ref[...] = _dot(a, w_ref[...].astype(BF16)) + b_ref[...]


def _ada(cond, w_ada, b_ada):
    nl, _, nout = w_ada.shape
    nr = cond.shape[0]
    tn = 1024
    return pl.pallas_call(
        _ada_kernel,
        grid=(nl, nout // tn),
        in_specs=[pl.BlockSpec((nr, D_MODEL), lambda l, j: (0, 0)),
                  pl.BlockSpec((None, D_MODEL, tn), lambda l, j: (l, 0, j)),
                  pl.BlockSpec((None, 1, tn), lambda l, j: (l, 0, j))],
        out_specs=pl.BlockSpec((None, nr, tn), lambda l, j: (l, 0, j)),
        out_shape=jax.ShapeDtypeStruct((nl, nr, nout), F32),
        compiler_params=_cp("arbitrary", "arbitrary"),
        name="ada",
    )(cond, w_ada, b_ada.reshape(nl, 1, nout))


def _mod_kernel(x_ref, m_ref, o_ref):
    o_ref[...] = (x_ref[...] * (1.0 + m_ref[1:2, :]) + m_ref[0:1, :]).astype(BF16)


def _modulate(x, mod, l, row_fn):
    tp = x.shape[0]
    return pl.pallas_call(
        _mod_kernel,
        grid=(tp // TOK_TILE,),
        in_specs=[pl.BlockSpec((TOK_TILE, D_MODEL), lambda i: (i, 0)),
                  pl.BlockSpec((None, None, 6, D_MODEL), lambda i: (l, row_fn(i), 0, 0))],
        out_specs=pl.BlockSpec((TOK_TILE, D_MODEL), lambda i: (i, 0)),
        out_shape=jax.ShapeDtypeStruct((tp, D_MODEL), BF16),
        compiler_params=_cp("arbitrary"),
        name="modulate",
    )(x, mod)


def _mm_kernel(x_ref, w_ref, o_ref, wbf):
    @pl.when(pl.program_id(1) == 0)
    def _():
        wbf[...] = w_ref[...].astype(BF16)

    o_ref[...] = _dot(x_ref[...], wbf[...]).astype(o_ref.dtype)


def _in_proj(h, w, l, ncols, out_dtype=F32):
    tp = h.shape[0]
    tm, tn = 512, 1024
    return pl.pallas_call(
        _mm_kernel,
        grid=(ncols // tn, tp // tm),
        in_specs=[pl.BlockSpec((tm, D_MODEL), lambda j, i: (i, 0)),
                  pl.BlockSpec((None, D_MODEL, tn), lambda j, i: (l, 0, j))],
        out_specs=pl.BlockSpec((tm, tn), lambda j, i: (i, j)),
        out_shape=jax.ShapeDtypeStruct((tp, ncols), out_dtype),
        scratch_shapes=[pltpu.VMEM((D_MODEL, tn), BF16)],
        compiler_params=_cp("arbitrary", "arbitrary"),
        name="in_proj",
    )(h, w)


def _dt_kernel(h_ref, w_ref, wt_ref, b_ref, a_ref, bt_ref, at_ref, dt_ref, la_ref, lat_ref):
    h = h_ref[...]
    dt = _softplus(_dot(h, w_ref[...]) + b_ref[...])
    la = dt * -jnp.exp(a_ref[...])
    dtt = _softplus(_dot_nt(wt_ref[...], h) + bt_ref[...])
    lat = dtt * -jnp.exp(at_ref[...])
    for g in range(SSM_GROUPS):
        dt_ref[g] = dt[:, 16 * g:16 * (g + 1)]
        la_ref[g] = la[:, 16 * g:16 * (g + 1)]
        for q in range(h.shape[0] // CHUNK):
            lat_ref[g, q] = lat[16 * g:16 * (g + 1), q * CHUNK:(q + 1) * CHUNK]


def _ssd_steps(h, w_dt, dt_bias, a_log):
    tp = h.shape[0]
    tm = 512
    nq = tm // CHUNK
    return pl.pallas_call(
        _dt_kernel,
        grid=(tp // tm,),
        in_specs=[pl.BlockSpec((tm, D_MODEL), lambda i: (i, 0)),
                  pl.BlockSpec((D_MODEL, 32), lambda i: (0, 0)),
                  pl.BlockSpec((32, D_MODEL), lambda i: (0, 0)),
                  pl.BlockSpec((1, 32), lambda i: (0, 0)),
                  pl.BlockSpec((1, 32), lambda i: (0, 0)),
                  pl.BlockSpec((32, 1), lambda i: (0, 0)),
                  pl.BlockSpec((32, 1), lambda i: (0, 0))],
        out_specs=[pl.BlockSpec((SSM_GROUPS, tm, 16), lambda i: (0, i, 0)),
                   pl.BlockSpec((SSM_GROUPS, tm, 16), lambda i: (0, i, 0)),
                   pl.BlockSpec((SSM_GROUPS, nq, 16, CHUNK), lambda i: (0, i, 0, 0))],
        out_shape=[jax.ShapeDtypeStruct((SSM_GROUPS, tp, 16), F32),
                   jax.ShapeDtypeStruct((SSM_GROUPS, tp, 16), F32),
                   jax.ShapeDtypeStruct((SSM_GROUPS, tp // CHUNK, 16, CHUNK), F32)],
        compiler_params=_cp("arbitrary"),
        name="ssd_steps",
    )(h, w_dt, w_dt.T, dt_bias.reshape(1, 32), a_log.reshape(1, 32),
      dt_bias.reshape(32, 1), a_log.reshape(32, 1))


def _conv_kernel(x_ref, w_ref, b_ref, o_ref, pad_ref, *, seq):
    ct = x_ref.shape[1]
    pad = SSM_CONV // 2
    pad_ref[0:8, :] = jnp.zeros((8, ct), F32)
    pad_ref[8 + seq:16 + seq, :] = jnp.zeros((8, ct), F32)
    pad_ref[8:8 + seq, :] = x_ref[...]
    for r in range(seq // CHUNK):
        acc = jnp.broadcast_to(b_ref[...], (CHUNK, ct))
        for j in range(SSM_CONV):
            s = 8 - pad + j + r * CHUNK
            acc = acc + w_ref[j:j + 1, :] * pad_ref[s:s + CHUNK, :]
        o_ref[r * CHUNK:(r + 1) * CHUNK, :] = _silu(acc)


def _conv(u1, conv_w, conv_b, seq):
    tp = u1.shape[0]
    ct = 512
    c0 = 2560 // ct
    return pl.pallas_call(
        functools.partial(_conv_kernel, seq=seq),
        grid=(tp // seq, SSM_CONV_DIM // ct),
        in_specs=[pl.BlockSpec((seq, ct), lambda b, j: (b, c0 + j)),
                  pl.BlockSpec((SSM_CONV, ct), lambda b, j: (0, j)),
                  pl.BlockSpec((1, ct), lambda b, j: (0, j))],
        out_specs=pl.BlockSpec((seq, ct), lambda b, j: (b, j)),
        out_shape=jax.ShapeDtypeStruct((tp, SSM_CONV_DIM), F32),
        scratch_shapes=[pltpu.VMEM((seq + 16, ct), F32)],
        compiler_params=_cp("arbitrary", "arbitrary"),
        name="conv",
    )(u1, conv_w, conv_b.reshape(1, -1))


def _expand8(a, lane_lt64):
    tiles = [jnp.where(lane_lt64, a[:, 2 * p:2 * p + 1], a[:, 2 * p + 1:2 * p + 2]) for p in range(4)]
    return jnp.concatenate(tiles, axis=1)


def _ssd_kernel(*refs, seq, has_s0, want_fin):
    (xs_ref, bm_ref, cm_ref, z_ref, dt_ref, la_ref, lat_ref, dsk_ref, nrm_ref) = refs[:9]
    pos = 9
    s0_ref = None
    if has_s0:
        s0_ref = refs[pos]
        pos += 1
    o_ref = refs[pos]
    pos += 1
    fin_ref = None
    if want_fin:
        fin_ref = refs[pos]
        pos += 1
    st_ref = refs[pos]
    y_ref = refs[pos + 1]

    nc = seq // CHUNK
    hpg = SSM_HEADS // SSM_GROUPS
    ri = lax.broadcasted_iota(I32, (CHUNK, CHUNK), 0)
    ci = lax.broadcasted_iota(I32, (CHUNK, CHUNK), 1)
    lower = ri >= ci
    upper = ri <= ci
    lt = jnp.where(lower, 1.0, 0.0).astype(F32)
    ut = jnp.where(upper, 1.0, 0.0).astype(F32)
    lane_lt64 = lax.broadcasted_iota(I32, (1, 128), 1) < SSM_HEAD_DIM
    half0 = lax.broadcasted_iota(I32, (CHUNK, 128), 1) < SSM_HEAD_DIM

    for d in range(2):
        if has_s0:
            st_ref[d] = jnp.concatenate([s0_ref[d, hh] for hh in range(hpg)], axis=1)
        else:
            st_ref[d] = jnp.zeros((SSM_STATE, hpg * SSM_HEAD_DIM), F32)

    def chunk(c, d):
        r0 = pl.multiple_of(c * CHUNK, CHUNK)
        la_c = la_ref[pl.ds(r0, CHUNK), :][:, 8 * d:8 * d + 8]
        dt_c = dt_ref[pl.ds(r0, CHUNK), :][:, 8 * d:8 * d + 8]
        lat_c = lat_ref[c][8 * d:8 * d + 8, :]
        if d == 0:
            acs = _dot_exact(lt, la_c)
            acst = _dot_exact(lat_c, ut)
            mask = lower
            edge = acs[CHUNK - 1:CHUNK, :]
        else:
            acs = _dot_exact(ut, la_c)
            acst = _dot_exact(lat_c, lt)
            mask = upper
            edge = acs[0:1, :]
        xs_c = xs_ref[pl.ds(r0, CHUNK), :]
        b_c = bm_ref[pl.ds(r0, CHUNK), :]
        c_c = cm_ref[pl.ds(r0, CHUNK), :].astype(BF16)
        g = _dot_nt(c_c, b_c.astype(BF16))
        xdt = xs_c * _expand8(dt_c, lane_lt64)
        tiles = []
        for p in range(hpg // 2):
            xt = xdt[:, 128 * p:128 * (p + 1)]
            acc = None
            for q in range(2):
                hh = 2 * p + q
                dec = jnp.exp(jnp.where(mask, acs[:, hh:hh + 1] - acst[hh:hh + 1, :], -jnp.inf))
                v = jnp.where(half0 if q == 0 else jnp.logical_not(half0), xt, 0.0).astype(BF16)
                t = _dot((g * dec).astype(BF16), v)
                acc = t if acc is None else acc + t
            tiles.append(acc)
        y = jnp.concatenate(tiles, axis=1)
        st = st_ref[d]
        y = y + _expand8(jnp.exp(acs), lane_lt64) * _dot(c_c, st.astype(BF16))
        if d == 0:
            y_ref[pl.ds(r0, CHUNK), :] = y
        else:
            y_ref[pl.ds(r0, CHUNK), :] += y
        wgt = jnp.exp(edge - acs)
        v = (xdt * _expand8(wgt, lane_lt64)).astype(BF16)
        st_ref[d] = st * _expand8(jnp.exp(edge), lane_lt64) + _dot(b_c.T.astype(BF16), v)

    def fwd(c, carry):
        chunk(c, 0)
        return carry

    def bwd(c, carry):
        chunk(nc - 1 - c, 1)
        return carry

    lax.fori_loop(0, nc, fwd, 0)
    lax.fori_loop(0, nc, bwd, 0)

    for r in range(nc):
        sl = slice(r * CHUNK, (r + 1) * CHUNK)
        y = (y_ref[sl, :] + xs_ref[sl, :] * dsk_ref[...]) * _silu(z_ref[sl, :])
        y = y * lax.rsqrt(jnp.mean(y * y, -1, keepdims=True) + RMS_EPS)
        o_ref[sl, :] = (y * nrm_ref[...]).astype(BF16)

    if want_fin:
        for d in range(2):
            st = st_ref[d]
            for hh in range(hpg):
                fin_ref[d, hh] = st[:, hh * SSM_HEAD_DIM:(hh + 1) * SSM_HEAD_DIM]


def _ssd(xc, u1, dt, la, lat, dsk, nrm, seq, l, s0):
    tp = xc.shape[0]
    nb = tp // seq
    nc = seq // CHUNK
    gw = SSM_D_INNER // SSM_GROUPS
    hpg = SSM_HEADS // SSM_GROUPS
    has_s0 = s0 is not None
    want_fin = not has_s0
    in_specs = [pl.BlockSpec((seq, gw), lambda b, g: (b, g)),
                pl.BlockSpec((seq, SSM_STATE), lambda b, g: (b, SSM_D_INNER // SSM_STATE + g)),
                pl.BlockSpec((seq, SSM_STATE), lambda b, g: (b, SSM_D_INNER // SSM_STATE + SSM_GROUPS + g)),
                pl.BlockSpec((seq, gw), lambda b, g: (b, 1536 // gw + g)),
                pl.BlockSpec((None, seq, 16), lambda b, g: (g, b, 0)),
                pl.BlockSpec((None, seq, 16), lambda b, g: (g, b, 0)),
                pl.BlockSpec((None, nc, 16, CHUNK), lambda b, g: (g, b, 0, 0)),
                pl.BlockSpec((1, gw), lambda b, g: (0, g)),
                pl.BlockSpec((1, gw), lambda b, g: (0, g))]
    args = [xc, xc, xc, u1, dt, la, lat, dsk, nrm]
    if has_s0:
        in_specs.append(pl.BlockSpec((None, None, 2, hpg, SSM_STATE, SSM_HEAD_DIM),
                                     lambda b, g: (b, l, 0, g, 0, 0)))
        args.append(s0)
    out_specs = [pl.BlockSpec((seq, gw), lambda b, g: (b, g))]
    out_shape = [jax.ShapeDtypeStruct((tp, SSM_D_INNER), BF16)]
    if want_fin:
        out_specs.append(pl.BlockSpec((None, 2, hpg, SSM_STATE, SSM_HEAD_DIM), lambda b, g: (b, 0, g, 0, 0)))
        out_shape.append(jax.ShapeDtypeStruct((nb, 2, SSM_HEADS, SSM_STATE, SSM_HEAD_DIM), F32))
    res = pl.pallas_call(
        functools.partial(_ssd_kernel, seq=seq, has_s0=has_s0, want_fin=want_fin),
        grid=(nb, SSM_GROUPS),
        in_specs=in_specs,
        out_specs=out_specs,
        out_shape=out_shape,
        scratch_shapes=[pltpu.VMEM((2, SSM_STATE, gw), F32), pltpu.VMEM((seq, gw), F32)],
        compiler_params=_cp("arbitrary", "arbitrary"),
        name="ssd_scan",
    )(*args)
    return (res[0], res[1]) if want_fin else (res[0], None)


def _ret_kernel(*refs, seq, l, latent):
    dec_ref, q_ref, k_ref, v_ref, g_ref = refs[:5]
    pos = 5
    if latent:
        cos_ref, sa_ref, sb_ref, s0_ref = refs[pos:pos + 4]
        pos += 4
    o_ref = refs[pos]
    pos += 1
    fin_ref = None
    if not latent:
        fin_ref = refs[pos]
        pos += 1
    qs_ref, ks_ref, st_ref, y_ref = refs[pos:pos + 4]

    nc = seq // CHUNK
    hd = pl.program_id(1)
    if latent:
        qs_ref[...] = _rope(q_ref[...], cos_ref[...], sa_ref[...], sb_ref[...]).astype(BF16)
        ks_ref[...] = _rope(k_ref[...] * (RET_QK_DIM ** -0.5), cos_ref[...], sa_ref[...], sb_ref[...])
        for d in range(2):
            st_ref[d] = s0_ref[d]
    else:
        qs_ref[...] = q_ref[...].astype(BF16)
        ks_ref[...] = k_ref[...] * (RET_QK_DIM ** -0.5)
        for d in range(2):
            st_ref[d] = jnp.zeros((RET_QK_DIM, RET_V_DIM), F32)

    ri = lax.broadcasted_iota(I32, (CHUNK, CHUNK), 0)
    ci = lax.broadcasted_iota(I32, (CHUNK, CHUNK), 1)
    dist = (ri - ci).astype(F32)
    rowi = lax.broadcasted_iota(I32, (CHUNK, 1), 0).astype(F32)

    def direction(d):
        raw = jnp.full((1, 1), dec_ref[l * 2 * RET_HEADS + d * RET_HEADS + hd], F32)
        lg = -_softplus(-raw)
        if d == 0:
            dec = jnp.exp(jnp.where(ri >= ci, dist * lg, -jnp.inf))
            e_in = jnp.exp((rowi + 1.0) * lg)
            wgt = jnp.exp((CHUNK - 1.0 - rowi) * lg)
        else:
            dec = jnp.exp(jnp.where(ri <= ci, -dist * lg, -jnp.inf))
            e_in = jnp.exp((CHUNK - rowi) * lg)
            wgt = jnp.exp(rowi * lg)
        full = jnp.exp(CHUNK * lg)

        def body(c, carry):
            cc = c if d == 0 else nc - 1 - c
            r0 = pl.multiple_of(cc * CHUNK, CHUNK)
            q = qs_ref[pl.ds(r0, CHUNK), :]
            k = ks_ref[pl.ds(r0, CHUNK), :]
            v = v_ref[pl.ds(r0, CHUNK), :].astype(BF16)
            g = _dot_nt(q, k.astype(BF16))
            st = st_ref[d]
            y = _dot((g * dec).astype(BF16), v) + e_in * _dot(q, st.astype(BF16))
            if d == 0:
                y_ref[pl.ds(r0, CHUNK), :] = y
            else:
                y_ref[pl.ds(r0, CHUNK), :] += y
            st_ref[d] = st * full + _dot((k * wgt).T.astype(BF16), v)
            return carry

        lax.fori_loop(0, nc, body, 0)

    direction(0)
    direction(1)

    for r in range(nc):
        sl = slice(r * CHUNK, (r + 1) * CHUNK)
        y = y_ref[sl, :]
        y = y * lax.rsqrt(jnp.mean(y * y, -1, keepdims=True) + RMS_EPS)
        o_ref[sl, :] = (y * _silu(g_ref[sl, :])).astype(BF16)

    if not latent:
        for d in range(2):
            fin_ref[d] = st_ref[d]


def _retention(u2, ret_decay, seq, l, rope, s0):
    tp = u2.shape[0]
    nb = tp // seq
    latent = s0 is not None
    in_specs = [pl.BlockSpec((seq, RET_QK_DIM), lambda b, h, *_: (b, h)),
                pl.BlockSpec((seq, RET_QK_DIM), lambda b, h, *_: (b, RET_HEADS + h)),
                pl.BlockSpec((seq, RET_V_DIM), lambda b, h, *_: (b, 1024 // RET_V_DIM + h)),
                pl.BlockSpec((seq, RET_V_DIM), lambda b, h, *_: (b, 2048 // RET_V_DIM + h))]
    args = [u2, u2, u2, u2]
    if latent:
        for t in rope:
            in_specs.append(pl.BlockSpec((seq, HEAD_DIM), lambda b, h, *_: (0, 0)))
            args.append(t)
        in_specs.append(pl.BlockSpec((None, None, 2, None, RET_QK_DIM, RET_V_DIM),
                                     lambda b, h, *_: (b, l, 0, h, 0, 0)))
        args.append(s0)
    out_specs = [pl.BlockSpec((seq, RET_V_DIM), lambda b, h, *_: (b, h))]
    out_shape = [jax.ShapeDtypeStruct((tp, RET_HEADS * RET_V_DIM), BF16)]
    if not latent:
        out_specs.append(pl.BlockSpec((None, 2, None, RET_QK_DIM, RET_V_DIM), lambda b, h, *_: (b, 0, h, 0, 0)))
        out_shape.append(jax.ShapeDtypeStruct((nb, 2, RET_HEADS, RET_QK_DIM, RET_V_DIM), F32))
    res = pl.pallas_call(
        functools.partial(_ret_kernel, seq=seq, l=l, latent=latent),
        grid_spec=pltpu.PrefetchScalarGridSpec(
            num_scalar_prefetch=1,
            grid=(nb, RET_HEADS),
            in_specs=in_specs,
            out_specs=out_specs,
            scratch_shapes=[pltpu.VMEM((seq, RET_QK_DIM), BF16),
                            pltpu.VMEM((seq, RET_QK_DIM), F32),
                            pltpu.VMEM((2, RET_QK_DIM, RET_V_DIM), F32),
                            pltpu.VMEM((seq, RET_V_DIM), F32)]),
        out_shape=out_shape,
        compiler_params=_cp("arbitrary", "arbitrary"),
        name="retention",
    )(ret_decay.reshape(-1), *args)
    return (res[0], None) if latent else (res[0], res[1])


def _softmax_pv(s, v, sink):
    m = jnp.max(s, -1, keepdims=True)
    if sink is not None:
        m = jnp.maximum(m, sink)
    e = jnp.exp(s - m)
    den = jnp.sum(e, -1, keepdims=True)
    if sink is not None:
        den = den + jnp.exp(sink - m)
    return _dot(e.astype(BF16), v) / den


def _ctx_attn_kernel(sink_ref, qa_ref, ka_ref, va_ref, qn_ref, kn_ref, vn_ref, ya_ref, yn_ref, *, l):
    scale = HEAD_DIM ** -0.5
    for h in range(WIN_HEADS):
        kv = h // WIN_GROUP
        hs = slice(h * HEAD_DIM, (h + 1) * HEAD_DIM)
        ks = slice(kv * HEAD_DIM, (kv + 1) * HEAD_DIM)
        s = _dot_nt(qa_ref[:, hs].astype(BF16), ka_ref[:, ks].astype(BF16)) * scale
        ya_ref[:, hs] = _softmax_pv(s, va_ref[:, ks].astype(BF16), sink_ref[l * WIN_HEADS + h]).astype(BF16)
    for h in range(NA_HEADS):
        hs = slice(h * HEAD_DIM, (h + 1) * HEAD_DIM)
        s = _dot_nt(qn_ref[:, hs].astype(BF16), kn_ref[:, hs].astype(BF16)) * scale
        yn_ref[:, hs] = _softmax_pv(s, vn_ref[:, hs].astype(BF16), None).astype(BF16)


def _ctx_attention(u1, u2, win_sink, seq, l):
    tp = u1.shape[0]
    nb = tp // seq
    kvw = WIN_KV_HEADS * HEAD_DIM
    hw = NA_HEADS * HEAD_DIM
    return pl.pallas_call(
        functools.partial(_ctx_attn_kernel, l=l),
        grid_spec=pltpu.PrefetchScalarGridSpec(
            num_scalar_prefetch=1,
            grid=(nb,),
            in_specs=[pl.BlockSpec((seq, hw), lambda b, *_: (b, 0)),
                      pl.BlockSpec((seq, kvw), lambda b, *_: (b, 1024 // kvw)),
                      pl.BlockSpec((seq, kvw), lambda b, *_: (b, 1280 // kvw)),
                      pl.BlockSpec((seq, hw), lambda b, *_: (b, 3072 // hw)),
                      pl.BlockSpec((seq, hw), lambda b, *_: (b, 4096 // hw)),
                      pl.BlockSpec((seq, hw), lambda b, *_: (b, 5120 // hw))],
            out_specs=[pl.BlockSpec((seq, hw), lambda b, *_: (b, 0)),
                       pl.BlockSpec((seq, hw), lambda b, *_: (b, 0))]),
        out_shape=[jax.ShapeDtypeStruct((tp, hw), BF16), jax.ShapeDtypeStruct((tp, hw), BF16)],
        compiler_params=_cp("arbitrary"),
        name="ctx_attention",
    )(win_sink.reshape(-1), u1, u1, u1, u2, u2, u2)


def _win_kernel(sink_ref, q_ref, k_ref, v_ref, kc_ref, vc_ref, cos_ref, sa_ref, sb_ref, o_ref,
                kr_ref, vb_ref, *, seq, l):
    scale = HEAD_DIM ** -0.5
    kvh = pl.program_id(1)
    nq = seq // CHUNK
    band = 3 * CHUNK
    rows = WIN_GROUP * CHUNK
    kr_ref[...] = _rope(k_ref[...], cos_ref[...], sa_ref[...], sb_ref[...]).astype(BF16)
    vb_ref[...] = v_ref[...].astype(BF16)
    kc = kc_ref[...].astype(BF16)
    vc = vc_ref[...].astype(BF16)
    rg = lax.broadcasted_iota(I32, (rows, 1), 0) // CHUNK
    sink = jnp.zeros((rows, 1), F32)
    for g in range(WIN_GROUP):
        sink = jnp.where(rg == g, sink_ref[l * WIN_HEADS + kvh * WIN_GROUP + g], sink)
    qoff = lax.broadcasted_iota(I32, (rows, band), 0) % CHUNK
    koff = lax.broadcasted_iota(I32, (rows, band), 1)

    def body(n, carry):
        r0 = pl.multiple_of(n * CHUNK, CHUNK)
        cos = cos_ref[pl.ds(r0, CHUNK), :]
        sa = sa_ref[pl.ds(r0, CHUNK), :]
        sb = sb_ref[pl.ds(r0, CHUNK), :]
        qs = jnp.concatenate(
            [_rope(q_ref[pl.ds(r0, CHUNK), g * HEAD_DIM:(g + 1) * HEAD_DIM], cos, sa, sb)
             for g in range(WIN_GROUP)], axis=0).astype(BF16)
        start = pl.multiple_of(jnp.clip((n - 1) * CHUNK, 0, seq - band), CHUNK)
        kb = kr_ref[pl.ds(start, band), :]
        vb = vb_ref[pl.ds(start, band), :]
        s_loc = _dot_nt(qs, kb) * scale
        ok = jnp.abs(r0 + qoff - (start + koff)) <= WINDOW
        s_loc = jnp.where(ok, s_loc, -jnp.inf)
        s_ctx = _dot_nt(qs, kc) * scale
        m = jnp.maximum(jnp.maximum(jnp.max(s_loc, -1, keepdims=True), jnp.max(s_ctx, -1, keepdims=True)), sink)
        e_loc = jnp.exp(s_loc - m)
        e_ctx = jnp.exp(s_ctx - m)
        den = jnp.sum(e_loc, -1, keepdims=True) + jnp.sum(e_ctx, -1, keepdims=True) + jnp.exp(sink - m)
        o = (_dot(e_ctx.astype(BF16), vc) + _dot(e_loc.astype(BF16), vb)) / den
        for g in range(WIN_GROUP):
            o_ref[pl.ds(r0, CHUNK), g * HEAD_DIM:(g + 1) * HEAD_DIM] = o[g * CHUNK:(g + 1) * CHUNK, :].astype(BF16)
        return carry

    lax.fori_loop(0, nq, body, 0)


def _win_attention(u1, cache_k, cache_v, win_sink, rope, seq, l):
    tp = u1.shape[0]
    nb = tp // seq
    past = cache_k.shape[2]
    gw = WIN_GROUP * HEAD_DIM
    ck = cache_k.reshape(nb, DEPTH, past, WIN_KV_HEADS * HEAD_DIM)
    cv = cache_v.reshape(nb, DEPTH, past, WIN_KV_HEADS * HEAD_DIM)
    rope_spec = pl.BlockSpec((seq, HEAD_DIM), lambda b, h, *_: (0, 0))
    return pl.pallas_call(
        functools.partial(_win_kernel, seq=seq, l=l),
        grid_spec=pltpu.PrefetchScalarGridSpec(
            num_scalar_prefetch=1,
            grid=(nb, WIN_KV_HEADS),
            in_specs=[pl.BlockSpec((seq, gw), lambda b, h, *_: (b, h)),
                      pl.BlockSpec((seq, HEAD_DIM), lambda b, h, *_: (b, 1024 // HEAD_DIM + h)),
                      pl.BlockSpec((seq, HEAD_DIM), lambda b, h, *_: (b, 1280 // HEAD_DIM + h)),
                      pl.BlockSpec((None, None, past, HEAD_DIM), lambda b, h, *_: (b, l, 0, h)),
                      pl.BlockSpec((None, None, past, HEAD_DIM), lambda b, h, *_: (b, l, 0, h)),
                      rope_spec, rope_spec, rope_spec],
            out_specs=pl.BlockSpec((seq, gw), lambda b, h, *_: (b, h)),
            scratch_shapes=[pltpu.VMEM((seq, HEAD_DIM), BF16), pltpu.VMEM((seq, HEAD_DIM), BF16)]),
        out_shape=jax.ShapeDtypeStruct((tp, WIN_HEADS * HEAD_DIM), BF16),
        compiler_params=_cp("arbitrary", "arbitrary"),
        name="win_attention",
    )(win_sink.reshape(-1), u1, u1, u1, ck, cv, *rope)


def _na_window_start(j, rows):
    kr = min(NA_ROWS, rows)
    rs = jnp.clip(2 * j - kr // 2, 0, rows - kr)
    return jnp.minimum(rs, rows - NA_WIN_ROWS)


def _na_kernel(q_ref, k_ref, v_ref, kc_ref, vc_ref, bias_ref, o_ref, kb_ref, vb_ref, *, seq):
    scale = HEAD_DIM ** -0.5
    rows = seq // GRID_W
    nq = seq // CHUNK
    win = NA_WIN_ROWS * GRID_W
    kb_ref[...] = k_ref[...].astype(BF16)
    vb_ref[...] = v_ref[...].astype(BF16)
    kc = kc_ref[...].astype(BF16)
    vc = vc_ref[...].astype(BF16)

    def body(j, carry):
        r0 = pl.multiple_of(j * CHUNK, CHUNK)
        q = q_ref[pl.ds(r0, CHUNK), :].astype(BF16)
        start = pl.multiple_of(_na_window_start(j, rows) * GRID_W, GRID_W)
        kw = kb_ref[pl.ds(start, win), :]
        vw = vb_ref[pl.ds(start, win), :]
        s_loc = _dot_nt(q, kw) * scale + bias_ref[j]
        s_ctx = _dot_nt(q, kc) * scale
        m = jnp.maximum(jnp.max(s_loc, -1, keepdims=True), jnp.max(s_ctx, -1, keepdims=True))
        e_loc = jnp.exp(s_loc - m)
        e_ctx = jnp.exp(s_ctx - m)
        den = jnp.sum(e_loc, -1, keepdims=True) + jnp.sum(e_ctx, -1, keepdims=True)
        o = (_dot(e_ctx.astype(BF16), vc) + _dot(e_loc.astype(BF16), vw)) / den
        o_ref[pl.ds(r0, CHUNK), :] = o.astype(BF16)
        return carry

    lax.fori_loop(0, nq, body, 0)


def _na_bias_table(rpb, seq):
    nh = rpb.shape[0]
    rows = seq // GRID_W
    kr = min(NA_ROWS, rows)
    nq = seq // CHUNK
    ndr, ndc = 2 * NA_ROWS - 1, 2 * NA_COLS - 1
    halves = CHUNK // GRID_W
    qc = np.arange(GRID_W)[:, None]
    kc = np.arange(GRID_W)[None, :]
    dc = (np.clip(kc - qc, 1 - NA_COLS, NA_COLS - 1) + (NA_COLS - 1)).reshape(-1)
    cs = np.clip(qc - NA_COLS // 2, 0, GRID_W - NA_COLS)
    col_ok = ((kc >= cs) & (kc < cs + NA_COLS)).reshape(-1)
    onehot = (dc[None, :] == np.arange(ndc)[:, None]).astype(np.float32)
    blocks = jnp.einsum('hrd,dq->hrq', rpb, onehot, precision=lax.Precision.HIGHEST)
    blocks = jnp.where(col_ok[None, None, :], blocks, NEG_BIG)
    blocks = jnp.concatenate([blocks, jnp.full((nh, 1, GRID_W * GRID_W), NEG_BIG, F32)], axis=1)
    blocks = blocks.reshape(nh, ndr + 1, GRID_W, GRID_W)
    j = np.arange(nq)[:, None, None]
    r = 2 * j + np.arange(halves)[None, :, None]
    ws = np.minimum(np.clip(2 * j - kr // 2, 0, rows - kr), rows - NA_WIN_ROWS)
    krow = ws + np.arange(NA_WIN_ROWS)[None, None, :]
    rs = np.clip(r - kr // 2, 0, rows - kr)
    blk = np.where((krow >= rs) & (krow < rs + kr), krow - r + (NA_ROWS - 1), ndr)
    t = jnp.take(blocks, jnp.asarray(blk.reshape(-1), I32), axis=1)
    t = t.reshape(nh, nq, halves, NA_WIN_ROWS, GRID_W, GRID_W).transpose(0, 1, 2, 4, 3, 5)
    return t.reshape(nh, nq, CHUNK, NA_WIN_ROWS * GRID_W)


def _na_attention(u2, cache_k, cache_v, bias, seq, l):
    tp = u2.shape[0]
    nb = tp // seq
    past = cache_k.shape[2]
    nq = seq // CHUNK
    win = NA_WIN_ROWS * GRID_W
    ck = cache_k.reshape(nb, DEPTH, past, NA_HEADS * HEAD_DIM)
    cv = cache_v.reshape(nb, DEPTH, past, NA_HEADS * HEAD_DIM)
    return pl.pallas_call(
        functools.partial(_na_kernel, seq=seq),
        grid=(nb, NA_HEADS),
        in_specs=[pl.BlockSpec((seq, HEAD_DIM), lambda b, h: (b, 3072 // HEAD_DIM + h)),
                  pl.BlockSpec((seq, HEAD_DIM), lambda b, h: (b, 4096 // HEAD_DIM + h)),
                  pl.BlockSpec((seq, HEAD_DIM), lambda b, h: (b, 5120 // HEAD_DIM + h)),
                  pl.BlockSpec((None, None, past, HEAD_DIM), lambda b, h: (b, l, 0, h)),
                  pl.BlockSpec((None, None, past, HEAD_DIM), lambda b, h: (b, l, 0, h)),
                  pl.BlockSpec((None, nq, CHUNK, win), lambda b, h: (h, 0, 0, 0))],
        out_specs=pl.BlockSpec((seq, HEAD_DIM), lambda b, h: (b, h)),
        out_shape=jax.ShapeDtypeStruct((tp, NA_HEADS * HEAD_DIM), BF16),
        scratch_shapes=[pltpu.VMEM((seq, HEAD_DIM), BF16), pltpu.VMEM((seq, HEAD_DIM), BF16)],
        compiler_params=_cp("arbitrary", "arbitrary"),
        name="na_attention",
    )(u2, u2, u2, ck, cv, bias)


def _merge_kernel(ya_ref, ys_ref, yr_ref, yn_ref, g0_ref, g1_ref, g2_ref, g3_ref, w_ref, o_ref, wbf):
    @pl.when(pl.program_id(1) == 0)
    def _():
        wbf[...] = w_ref[...].astype(BF16)

    acc = None
    for n, (b_ref, g_ref) in enumerate(((ya_ref, g0_ref), (ys_ref, g1_ref), (yr_ref, g2_ref), (yn_ref, g3_ref))):
        t = jax.nn.sigmoid(g_ref[...].astype(F32)) * _dot(b_ref[...], wbf[n])
        acc = t if acc is None else acc + t
    o_ref[...] = acc.astype(BF16)


def _merge(ya, ys, yr, yn, gates, w_branch, l):
    tp = ya.shape[0]
    tm, tn = 512, 512
    per = D_MODEL // tn
    br = pl.BlockSpec((tm, BRANCH_W), lambda j, i: (i, 0))
    gate = lambda n: pl.BlockSpec((tm, tn), lambda j, i: (i, n * per + j))
    return pl.pallas_call(
        _merge_kernel,
        grid=(per, tp // tm),
        in_specs=[br, br, br, br, gate(0), gate(1), gate(2), gate(3),
                  pl.BlockSpec((None, N_BRANCH, BRANCH_W, tn), lambda j, i: (l, 0, 0, j))],
        out_specs=pl.BlockSpec((tm, tn), lambda j, i: (i, j)),
        out_shape=jax.ShapeDtypeStruct((tp, D_MODEL), BF16),
        scratch_shapes=[pltpu.VMEM((N_BRANCH, BRANCH_W, tn), BF16)],
        compiler_params=_cp("arbitrary", "arbitrary"),
        name="merge",
    )(ya, ys, yr, yn, gates, gates, gates, gates, w_branch)


def _out_kernel(m_ref, w_ref, x_ref, mod_ref, g_ref, b_ref, o_ref):
    mix = _dot(m_ref[...], w_ref[...])
    y = ALPHA * x_ref[...] + mod_ref[2:3, :] * mix
    o_ref[...] = _layer_norm(y, g_ref[...], b_ref[...])


def _out_proj(merged, w_out_bf, x, mod, ln_g, ln_b, l, row_fn):
    tp = x.shape[0]
    tm = TOK_TILE
    row = pl.BlockSpec((tm, D_MODEL), lambda i: (i, 0))
    vec = pl.BlockSpec((None, 1, D_MODEL), lambda i: (l, 0, 0))
    return pl.pallas_call(
        _out_kernel,
        grid=(tp // tm,),
        in_specs=[row,
                  pl.BlockSpec((None, D_MODEL, D_MODEL), lambda i: (l, 0, 0)),
                  row,
                  pl.BlockSpec((None, None, 6, D_MODEL), lambda i: (l, row_fn(i), 0, 0)),
                  vec, vec],
        out_specs=row,
        out_shape=jax.ShapeDtypeStruct((tp, D_MODEL), F32),
        compiler_params=_cp("arbitrary"),
        name="out_proj_ln1",
    )(merged, w_out_bf, x, mod, ln_g.reshape(DEPTH, 1, D_MODEL), ln_b.reshape(DEPTH, 1, D_MODEL))


def _pack_pair(a, b):
    ab = lax.bitcast_convert_type(a.astype(BF16).astype(F32), jnp.uint32)
    bb = lax.bitcast_convert_type(b.astype(BF16).astype(F32), jnp.uint32)
    return (ab & jnp.uint32(0xFFFF0000)) | (bb >> 16)


def _unpack_pair(p):
    a = lax.bitcast_convert_type(p & jnp.uint32(0xFFFF0000), F32)
    b = lax.bitcast_convert_type(p << 16, F32)
    return a, b


HALF = D_MODEL // 2
ROW_TILES = HALF // 128


def _store_rows(ref, packed):
    n = packed.shape[0]
    for c in range(ROW_TILES):
        ref[pl.ds(c, n, stride=ROW_TILES), :] = packed[:, c * 128:(c + 1) * 128]


def _load_rows(ref):
    n = ref.shape[0] // ROW_TILES
    return jnp.concatenate([ref[pl.ds(c, n, stride=ROW_TILES), :] for c in range(ROW_TILES)], axis=1)


def _row(ref, start):
    return ref.at[pl.ds(pl.multiple_of(start, ROW_TILES), ROW_TILES), :]


def _router_kernel(xc_ref, xl_ref, mod_ref, rw_ref, rb_ref, xp_ref, idx_ref, wt_ref, pos_ref, cnt_ref, carry,
                   *, n_ctx_tiles):
    i = pl.program_id(0)

    @pl.when(i == 0)
    def _():
        carry[...] = jnp.zeros(carry.shape, F32)

    x = jnp.where(i < n_ctx_tiles, xc_ref[...], xl_ref[...])
    xm = x * (1.0 + mod_ref[4:5, :]) + mod_ref[3:4, :]
    _store_rows(xp_ref, _pack_pair(xm[:, :HALF], xm[:, HALF:]))
    tm = xm.shape[0]
    scores = jax.nn.sigmoid(_dot_nt(rw_ref[...], xm.astype(BF16)))
    biased = scores + rb_ref[...]
    per = N_EXPERTS // N_EXPERT_GROUPS
    sub = lax.broadcasted_iota(I32, (per, tm), 0)
    bg = [biased[per * g:per * (g + 1), :] for g in range(N_EXPERT_GROUPS)]
    sg = [scores[per * g:per * (g + 1), :] for g in range(N_EXPERT_GROUPS)]
    gs = []
    for g in range(N_EXPERT_GROUPS):
        m1 = jnp.max(bg[g], 0, keepdims=True)
        i1 = jnp.min(jnp.where(bg[g] == m1, sub, per), 0, keepdims=True)
        m2 = jnp.max(jnp.where(sub == i1, -jnp.inf, bg[g]), 0, keepdims=True)
        gs.append(m1 + m2)
    cand = []
    for g in range(N_EXPERT_GROUPS):
        rank = jnp.zeros((1, tm), I32)
        for g2 in range(N_EXPERT_GROUPS):
            if g2 == g:
                continue
            ahead = (gs[g2] > gs[g]) | ((gs[g2] == gs[g]) & (g2 < g))
            rank = rank + ahead.astype(I32)
        cand.append(jnp.where(rank < TOPK_GROUPS, bg[g], -jnp.inf))
    eidx = [sub + per * g for g in range(N_EXPERT_GROUPS)]
    idx_rows, w_rows = [], []
    for _ in range(TOP_K):
        m = None
        for g in range(N_EXPERT_GROUPS):
            t = jnp.max(cand[g], 0, keepdims=True)
            m = t if m is None else jnp.maximum(m, t)
        ik = None
        for g in range(N_EXPERT_GROUPS):
            t = jnp.min(jnp.where(cand[g] == m, eidx[g], N_EXPERTS), 0, keepdims=True)
            ik = t if ik is None else jnp.minimum(ik, t)
        wk = jnp.zeros((1, tm), F32)
        for g in range(N_EXPERT_GROUPS):
            hit = eidx[g] == ik
            wk = wk + jnp.sum(jnp.where(hit, sg[g], 0.0), 0, keepdims=True)
            cand[g] = jnp.where(hit, -jnp.inf, cand[g])
        idx_rows.append(ik)
        w_rows.append(wk)
    tot = w_rows[0]
    for wk in w_rows[1:]:
        tot = tot + wk
    idx_ref[...] = jnp.concatenate(idx_rows, axis=0)
    wt_ref[...] = jnp.concatenate([wk / tot * ROUTED_SCALE for wk in w_rows], axis=0)

    member = []
    for g in range(N_EXPERT_GROUPS):
        sel = eidx[g] == idx_rows[0]
        for k in range(1, TOP_K):
            sel = sel | (eidx[g] == idx_rows[k])
        member.append(jnp.where(sel, 1.0, 0.0))
    member = jnp.concatenate(member, axis=0)
    earlier = (lax.broadcasted_iota(I32, (tm, tm), 0) < lax.broadcasted_iota(I32, (tm, tm), 1))
    before = carry[...] + _dot(member.astype(BF16), jnp.where(earlier, 1.0, 0.0).astype(BF16))
    pos_rows = []
    for k in range(TOP_K):
        pk = jnp.zeros((1, tm), F32)
        for g in range(N_EXPERT_GROUPS):
            pk = pk + jnp.sum(jnp.where(eidx[g] == idx_rows[k], before[per * g:per * (g + 1), :], 0.0),
                              0, keepdims=True)
        pos_rows.append(pk)
    pos_ref[...] = jnp.concatenate(pos_rows, axis=0).astype(I32)
    carry[...] = carry[...] + jnp.sum(member, axis=1, keepdims=True)
    cnt_ref[...] = carry[...]


def _router(x1c, x1l, mod, rwt_bf, rbias, l, n_lat_tiles_per_seq):
    tc, tl = x1c.shape[0], x1l.shape[0]
    tm = TOK_TILE
    nct, nlt = tc // tm, tl // tm
    t = tc + tl
    row_fn = lambda i: jnp.where(i < nct, 0, 1 + (i - nct) // n_lat_tiles_per_seq)
    return pl.pallas_call(
        functools.partial(_router_kernel, n_ctx_tiles=nct),
        grid=(nct + nlt,),
        in_specs=[pl.BlockSpec((tm, D_MODEL), lambda i: (jnp.minimum(i, nct - 1), 0)),
                  pl.BlockSpec((tm, D_MODEL), lambda i: (jnp.maximum(i - nct, 0), 0)),
                  pl.BlockSpec((None, None, 6, D_MODEL), lambda i: (l, row_fn(i), 0, 0)),
                  pl.BlockSpec((N_EXPERTS, D_MODEL), lambda i: (0, 0)),
                  pl.BlockSpec((N_EXPERTS, 1), lambda i: (0, 0))],
        out_specs=[pl.BlockSpec((tm * ROW_TILES, 128), lambda i: (i, 0)),
                   pl.BlockSpec((TOP_K, tm), lambda i: (0, i)),
                   pl.BlockSpec((TOP_K, tm), lambda i: (0, i)),
                   pl.BlockSpec((TOP_K, tm), lambda i: (0, i)),
                   pl.BlockSpec((N_EXPERTS, 1), lambda i: (0, 0))],
        out_shape=[jax.ShapeDtypeStruct((t * ROW_TILES, 128), jnp.uint32),
                   jax.ShapeDtypeStruct((TOP_K, t), I32),
                   jax.ShapeDtypeStruct((TOP_K, t), F32),
                   jax.ShapeDtypeStruct((TOP_K, t), I32),
                   jax.ShapeDtypeStruct((N_EXPERTS, 1), F32)],
        scratch_shapes=[pltpu.VMEM((N_EXPERTS, 1), F32)],
        compiler_params=_cp("arbitrary"),
        name="router",
    )(x1c, x1l, mod, rwt_bf, rbias.reshape(N_EXPERTS, 1))


DMA_UNROLL = 8
N_PAD_SLOTS = N_EXPERTS * MOE_BM


def _dispatch_kernel(pad_ref, dst_ref, xp_ref, xs_hbm, zrow, sem, zsem):
    tm = xp_ref.shape[0] // ROW_TILES
    n = TOP_K * tm

    @pl.when(pl.program_id(0) == 0)
    def _():
        zrow[...] = jnp.zeros(zrow.shape, jnp.uint32)

        def z_issue(j, carry):
            for p in range(2):
                pltpu.make_async_copy(zrow, _row(xs_hbm, pad_ref[2 * j + p]), zsem).start(priority=p)
            return carry

        def z_drain(s, carry):
            pltpu.make_async_copy(zrow, _row(xs_hbm, 0), zsem).wait()
            return carry

        lax.fori_loop(0, N_PAD_SLOTS // 2, z_issue, 0, unroll=DMA_UNROLL // 2)
        lax.fori_loop(0, N_PAD_SLOTS, z_drain, 0, unroll=DMA_UNROLL)

    for k in range(TOP_K):
        def issue(j, carry, k=k):
            for p in range(2):
                r = 2 * j + p
                pltpu.make_async_copy(_row(xp_ref, r * ROW_TILES), _row(xs_hbm, dst_ref[0, k * tm + r]),
                                      sem).start(priority=p)
            return carry

        lax.fori_loop(0, tm // 2, issue, 0, unroll=DMA_UNROLL // 2)

    def drain(s, carry):
        pltpu.make_async_copy(_row(xp_ref, 0), _row(xs_hbm, 0), sem).wait()
        return carry

    lax.fori_loop(0, n, drain, 0, unroll=DMA_UNROLL)


def _dispatch(xp, dest_tiles, pad_dst, n_rows):
    t = xp.shape[0] // ROW_TILES
    tm = TOK_TILE
    return pl.pallas_call(
        _dispatch_kernel,
        grid_spec=pltpu.PrefetchScalarGridSpec(
            num_scalar_prefetch=1,
            grid=(t // tm,),
            in_specs=[pl.BlockSpec((None, 1, TOP_K * tm), lambda i, pad: (i, 0, 0), memory_space=pltpu.SMEM),
                      pl.BlockSpec((tm * ROW_TILES, 128), lambda i, pad: (i, 0))],
            out_specs=pl.BlockSpec(memory_space=pl.ANY),
            scratch_shapes=[pltpu.VMEM((ROW_TILES, 128), jnp.uint32),
                            pltpu.SemaphoreType.DMA(()), pltpu.SemaphoreType.DMA(())]),
        out_shape=jax.ShapeDtypeStruct((n_rows * ROW_TILES, 128), jnp.uint32),
        compiler_params=_cp("arbitrary"),
        name="dispatch",
    )(pad_dst, dest_tiles, xp)


def _experts_kernel(be_ref, nu_ref, x_ref, wgu_ref, wdn_ref, y_ref, wgu_bf, wdn_bf):
    i = pl.program_id(0)

    @pl.when(i < nu_ref[0])
    def _():
        prev = be_ref[jnp.maximum(i - 1, 0)]

        @pl.when((i == 0) | (be_ref[i] != prev))
        def _():
            wgu_bf[...] = wgu_ref[...].astype(BF16)
            wdn_bf[...] = wdn_ref[...].astype(BF16)

        xa, xb = _unpack_pair(_load_rows(x_ref))
        hgu = _dot(xa.astype(BF16), wgu_bf[:HALF, :]) + _dot(xb.astype(BF16), wgu_bf[HALF:, :])
        act = (_silu(hgu[:, :EXPERT_FF]) * hgu[:, EXPERT_FF:]).astype(BF16)
        y = _dot(act, wdn_bf[...])
        _store_rows(y_ref, _pack_pair(y[:, :HALF], y[:, HALF:]))

    @pl.when(i >= nu_ref[0])
    def _():
        y_ref[...] = jnp.zeros(y_ref.shape, jnp.uint32)


def _experts(xs, blk_e, n_used, w_gu, w_dn, l):
    nblk = blk_e.shape[0]
    last = lambda i, be, nu: jnp.minimum(i, nu[0] - 1)
    return pl.pallas_call(
        _experts_kernel,
        grid_spec=pltpu.PrefetchScalarGridSpec(
            num_scalar_prefetch=2,
            grid=(nblk,),
            in_specs=[pl.BlockSpec((MOE_BM * ROW_TILES, 128), lambda i, be, nu: (last(i, be, nu), 0)),
                      pl.BlockSpec((None, None, D_MODEL, 2 * EXPERT_FF),
                                   lambda i, be, nu: (l, be[last(i, be, nu)], 0, 0)),
                      pl.BlockSpec((None, None, EXPERT_FF, D_MODEL),
                                   lambda i, be, nu: (l, be[last(i, be, nu)], 0, 0))],
            out_specs=pl.BlockSpec((MOE_BM * ROW_TILES, 128), lambda i, be, nu: (i, 0)),
            scratch_shapes=[pltpu.VMEM((D_MODEL, 2 * EXPERT_FF), BF16),
                            pltpu.VMEM((EXPERT_FF, D_MODEL), BF16)]),
        out_shape=jax.ShapeDtypeStruct((nblk * MOE_BM * ROW_TILES, 128), jnp.uint32),
        compiler_params=_cp("arbitrary"),
        name="experts",
    )(blk_e, n_used, xs, w_gu, w_dn)


FIN_TM = 128


def _final_kernel(dcur_ref, dnxt_ref, ys_hbm, x1_ref, xp_ref, wk_ref, mod_ref, g_ref, b_ref, sgu_ref, sdn_ref,
                  o_ref, gbuf, sem):
    n = TOP_K * FIN_TM
    i = pl.program_id(0)
    slot = i % 2

    def fetch(d_ref, sl):
        for k in range(TOP_K):
            def issue(j, carry, k=k):
                for p in range(2):
                    r = 2 * j + p
                    pltpu.make_async_copy(_row(ys_hbm, d_ref[0, k * FIN_TM + r]),
                                          _row(gbuf.at[sl, k], r * ROW_TILES), sem.at[sl]).start(priority=p)
                return carry

            lax.fori_loop(0, FIN_TM // 2, issue, 0, unroll=DMA_UNROLL // 2)

    @pl.when(i == 0)
    def _():
        fetch(dcur_ref, 0)

    @pl.when(i + 1 < pl.num_programs(0))
    def _():
        fetch(dnxt_ref, 1 - slot)

    xa, xb = _unpack_pair(_load_rows(xp_ref))
    hgu = _dot(xa.astype(BF16), sgu_ref[:HALF, :]) + _dot(xb.astype(BF16), sgu_ref[HALF:, :])
    act = (_silu(hgu[:, :SHARED_FF]) * hgu[:, SHARED_FF:]).astype(BF16)
    shared = _dot(act, sdn_ref[...])

    def drain(s, carry):
        pltpu.make_async_copy(_row(ys_hbm, 0), _row(gbuf.at[slot, 0], 0), sem.at[slot]).wait()
        return carry

    lax.fori_loop(0, n, drain, 0, unroll=DMA_UNROLL)

    ra = rb = None
    for k in range(TOP_K):
        ya, yb = _unpack_pair(_load_rows(gbuf.at[slot, k]))
        w = wk_ref[:, k:k + 1]
        ra = w * ya if ra is None else ra + w * ya
        rb = w * yb if rb is None else rb + w * yb
    gate = mod_ref[5:6, :]
    x1 = x1_ref[...]
    ya = ALPHA * x1[:, :HALF] + gate[:, :HALF] * (ra + shared[:, :HALF])
    yb = ALPHA * x1[:, HALF:] + gate[:, HALF:] * (rb + shared[:, HALF:])
    mu = (jnp.sum(ya, -1, keepdims=True) + jnp.sum(yb, -1, keepdims=True)) / D_MODEL
    ya = ya - mu
    yb = yb - mu
    var = (jnp.sum(ya * ya, -1, keepdims=True) + jnp.sum(yb * yb, -1, keepdims=True)) / D_MODEL
    inv = lax.rsqrt(var + LN_EPS)
    o_ref[:, :HALF] = ya * inv * g_ref[:, :HALF] + b_ref[:, :HALF]
    o_ref[:, HALF:] = yb * inv * g_ref[:, HALF:] + b_ref[:, HALF:]


def _final(dest, ys, x1, xp, wk, mod, ln_g, ln_b, sgu_bf, sdn_bf, l, row_fn, tok_off):
    tp = x1.shape[0]
    tm = FIN_TM
    off = tok_off // tm
    nt = tp // tm
    row = pl.BlockSpec((tm, D_MODEL), lambda i: (i, 0))
    vec = pl.BlockSpec((None, 1, D_MODEL), lambda i: (l, 0, 0))
    return pl.pallas_call(
        _final_kernel,
        grid=(nt,),
        in_specs=[pl.BlockSpec((None, 1, TOP_K * tm), lambda i: (off + i, 0, 0), memory_space=pltpu.SMEM),
                  pl.BlockSpec((None, 1, TOP_K * tm), lambda i: (off + jnp.minimum(i + 1, nt - 1), 0, 0),
                               memory_space=pltpu.SMEM),
                  pl.BlockSpec(memory_space=pl.ANY),
                  row,
                  pl.BlockSpec((tm * ROW_TILES, 128), lambda i: (off + i, 0)),
                  pl.BlockSpec((tm, TOP_K), lambda i: (off + i, 0)),
                  pl.BlockSpec((None, None, 6, D_MODEL), lambda i: (l, row_fn(i * tm // TOK_TILE), 0, 0)),
                  vec, vec,
                  pl.BlockSpec((None, D_MODEL, 2 * SHARED_FF), lambda i: (l, 0, 0)),
                  pl.BlockSpec((None, SHARED_FF, D_MODEL), lambda i: (l, 0, 0))],
        out_specs=row,
        out_shape=jax.ShapeDtypeStruct((tp, D_MODEL), F32),
        scratch_shapes=[pltpu.VMEM((2, TOP_K, tm * ROW_TILES, 128), jnp.uint32), pltpu.SemaphoreType.DMA((2,))],
        compiler_params=_cp("arbitrary"),
        name="combine_ln2",
    )(dest, dest, ys, x1, xp, wk, mod, ln_g.reshape(DEPTH, 1, D_MODEL), ln_b.reshape(DEPTH, 1, D_MODEL),
      sgu_bf, sdn_bf)


def _dispatch_plan(idx_t, pos_t, cnt):
    t = idx_t.shape[1]
    nblk = t * TOP_K // MOE_BM + N_EXPERTS
    n_slots = nblk * MOE_BM
    e_ids = jnp.arange(N_EXPERTS, dtype=I32)
    counts = cnt[:, 0].astype(I32)
    padded = (counts + MOE_BM - 1) // MOE_BM * MOE_BM
    incl = e_ids[None, :] <= e_ids[:, None]
    pad_end = jnp.sum(jnp.where(incl, padded[None, :], 0), axis=1)
    pad_start = pad_end - padded
    start_of = jnp.sum(jnp.where(idx_t[:, :, None] == e_ids, pad_start, 0), axis=-1)
    dest = (start_of + pos_t) * ROW_TILES

    def tiles(tm):
        return dest.reshape(TOP_K, t // tm, tm).transpose(1, 0, 2).reshape(t // tm, 1, TOP_K * tm)

    blk_e = jnp.minimum(jnp.sum((pad_end[None, :] <= (jnp.arange(nblk, dtype=I32) * MOE_BM)[:, None]).astype(I32),
                                axis=1), N_EXPERTS - 1)
    n_used = (pad_end[-1] // MOE_BM).reshape(1)
    free = padded - counts
    free_end = jnp.sum(jnp.where(incl, free[None, :], 0), axis=1)
    free_start = free_end - free
    j = jnp.arange(n_slots - t * TOP_K, dtype=I32)
    owner = jnp.sum((free_end[None, :] <= j[:, None]).astype(I32), axis=1)
    own = owner[:, None] == e_ids
    in_pad = jnp.sum(jnp.where(own, (pad_start + counts - free_start)[None, :], 0), axis=1) + j
    in_tail = pad_end[-1] + j - free_end[-1]
    pad_dst = jnp.where(owner < N_EXPERTS, in_pad, in_tail) * ROW_TILES
    return tiles(TOK_TILE), tiles(FIN_TM), pad_dst, blk_e, n_used


def _rope_tables(n_tok):
    t = jnp.arange(n_tok)
    row = (t // GRID_W).astype(F32)
    col = (t % GRID_W).astype(F32)
    half = HEAD_DIM // 2
    inv = ROPE_BASE ** (-jnp.arange(0, half, 2, dtype=F32) / half)
    ar = row[:, None] * inv
    ac = col[:, None] * inv
    ang = jnp.concatenate([ar, ar, ac, ac], -1)
    cos, sin = jnp.cos(ang), jnp.sin(ang)
    quarter = (jnp.arange(HEAD_DIM) // (HEAD_DIM // 4)) % 2
    sin_a = jnp.where(quarter == 0, -sin, 0.0)
    sin_b = jnp.where(quarter == 1, sin, 0.0)
    return cos, sin_a, sin_b


def kernel(x_prompt, x_sample, cache_win_k, cache_win_v, state_ssm, state_ret, cache_na_k, cache_na_v, c, c_ctx, w_ada, b_ada, w_in, win_sink, conv_w, conv_b, dt_bias, a_log, d_skip, ssm_norm, ret_decay, na_rpb, w_branch, w_out, ln1_g, ln1_b, router_w, router_bias, exp_w_gu, exp_w_down, sh_w_gu, sh_w_down, ln2_g, ln2_b):
    bc, lc, _ = x_prompt.shape
    bl, ll, _ = x_sample.shape
    assert lc == TOK_TILE and ll % TOK_TILE == 0
    tc, tl = bc * lc, bl * ll
    lat_tiles = ll // TOK_TILE

    nr = -(-(1 + bl) // 8) * 8
    cond = jnp.zeros((nr, D_MODEL), F32).at[0].set(c_ctx).at[1:1 + bl].set(c)
    mod = _ada(cond, w_ada, b_ada).reshape(DEPTH, nr, 6, D_MODEL)
    row_ctx = lambda i: 0
    row_lat = lambda i: 1 + i // lat_tiles

    rope = _rope_tables(ll)
    perm = np.array([d * SSM_HEADS + g * 8 + hh for g in range(SSM_GROUPS) for d in range(2) for hh in range(8)])
    dsk_full = jnp.repeat(d_skip, SSM_HEAD_DIM, axis=1)
    w_in1 = w_in[:, :, :U1_COLS].astype(BF16)
    w_in2 = w_in[:, :, U2_OFF:GATE_OFF].astype(BF16)
    w_ing = w_in[:, :, GATE_OFF:].astype(BF16)
    w_dt = w_in[:, :, DT_OFF:U2_OFF][:, :, perm].astype(BF16)
    w_out_bf = w_out.astype(BF16)
    rwt_bf = jnp.swapaxes(router_w, 1, 2).astype(BF16)
    sgu_bf = sh_w_gu.astype(BF16)
    sdn_bf = sh_w_down.astype(BF16)

    xs = {"ctx": x_prompt.reshape(tc, D_MODEL), "lat": x_sample.reshape(tl, D_MODEL)}
    new_ctx = []
    for l in range(DEPTH):
        x1 = {}
        bias_tab = _na_bias_table(na_rpb[l], ll)
        for path in ("ctx", "lat"):
            latent = path == "lat"
            x = xs[path]
            seq = ll if latent else lc
            row_fn = row_lat if latent else row_ctx
            h = _modulate(x, mod, l, row_fn)
            u1 = _in_proj(h, w_in1, l, U1_COLS)
            u2 = _in_proj(h, w_in2, l, U2_COLS)
            gates = _in_proj(h, w_ing, l, GATE_COLS, BF16)
            dt, la, lat_t = _ssd_steps(h, w_dt[l], dt_bias[l].reshape(-1)[perm], a_log[l].reshape(-1)[perm])
            xc = _conv(u1, conv_w[l], conv_b[l], seq)
            ys, ssm_fin = _ssd(xc, u1, dt, la, lat_t, dsk_full[l:l + 1], ssm_norm[l:l + 1], seq, l,
                               state_ssm if latent else None)
            yr, ret_fin = _retention(u2, ret_decay, seq, l, rope if latent else None,
                                     state_ret if latent else None)
            if latent:
                ya = _win_attention(u1, cache_win_k, cache_win_v, win_sink, rope, seq, l)
                yn = _na_attention(u2, cache_na_k, cache_na_v, bias_tab, seq, l)
            else:
                ya, yn = _ctx_attention(u1, u2, win_sink, seq, l)
                kv = lambda t, nh: t.reshape(bc, lc, nh, HEAD_DIM)
                new_ctx.append((kv(u1[:, 1024:1280], WIN_KV_HEADS), kv(u1[:, 1280:1536], WIN_KV_HEADS),
                                ssm_fin, ret_fin,
                                kv(u2[:, 4096:5120], NA_HEADS), kv(u2[:, 5120:6144], NA_HEADS)))
            merged = _merge(ya, ys, yr, yn, gates, w_branch, l)
            x1[path] = _out_proj(merged, w_out_bf, x, mod, ln1_g, ln1_b, l, row_fn)

        xp, idx_t, w_t, pos_t, cnt = _router(x1["ctx"], x1["lat"], mod, rwt_bf[l], router_bias[l], l, lat_tiles)
        dest_disp, dest_fin, pad_dst, blk_e, n_used = _dispatch_plan(idx_t, pos_t, cnt)
        x_sorted = _dispatch(xp, dest_disp, pad_dst, blk_e.shape[0] * MOE_BM)
        y_sorted = _experts(x_sorted, blk_e, n_used, exp_w_gu, exp_w_down, l)
        wk = w_t.T
        for path, off, row_fn in (("ctx", 0, row_ctx), ("lat", tc, row_lat)):
            xs[path] = _final(dest_fin, y_sorted, x1[path], xp, wk, mod, ln2_g, ln2_b, sgu_bf, sdn_bf,
                              l, row_fn, off)

    stack = lambda i: jnp.stack([t[i] for t in new_ctx], axis=1)
    return (xs["ctx"].reshape(bc, lc, D_MODEL), xs["lat"].reshape(bl, ll, D_MODEL),
            stack(0), stack(1), stack(2), stack(3), stack(4), stack(5))
```

```python
import functools

import jax
import jax.numpy as jnp
import numpy as np
from jax import lax
from jax.experimental import pallas as pl
from jax.experimental.pallas import tpu as pltpu

F32 = jnp.float32
BF16 = jnp.bfloat16
I32 = jnp.int32

D_MODEL = 2048
DEPTH = 2
GRID_W = 64
HEAD_DIM = 128
ROPE_BASE = 10000.0
CHUNK = 128
WIN_HEADS = 8
WIN_KV_HEADS = 2
WIN_GROUP = WIN_HEADS // WIN_KV_HEADS
WINDOW = 128
SSM_D_INNER = D_MODEL // 2
SSM_HEAD_DIM = 64
SSM_HEADS = SSM_D_INNER // SSM_HEAD_DIM
SSM_GROUPS = 2
SSM_STATE = 128
SSM_CONV = 7
SSM_CONV_DIM = SSM_D_INNER + 2 * SSM_GROUPS * SSM_STATE
RET_HEADS = 4
RET_QK_DIM = 128
RET_V_DIM = 256
NA_HEADS = 8
NA_ROWS = 8
NA_COLS = 16
N_BRANCH = 4
BRANCH_W = D_MODEL // 2
N_EXPERTS = 64
TOP_K = 8
N_EXPERT_GROUPS = 8
TOPK_GROUPS = 4
EXPERT_FF = 512
SHARED_FF = 512
ROUTED_SCALE = 2.5
ALPHA = (2.0 * DEPTH) ** 0.25
LN_EPS = 1e-5
RMS_EPS = 1e-6

U1_COLS = 4096
DT_OFF = 4096
U2_OFF = DT_OFF + 2 * SSM_HEADS
U2_COLS = 6144
GATE_OFF = U2_OFF + U2_COLS
GATE_COLS = N_BRANCH * D_MODEL

TOK_TILE = 256
MOE_BM = 256
NA_WIN_ROWS = 10
NEG_BIG = -1e30

VMEM_LIMIT = 52 * 1024 * 1024


def _cp(*sem):
    return pltpu.CompilerParams(dimension_semantics=sem, vmem_limit_bytes=VMEM_LIMIT)


def _dot(a, b):
    return jnp.dot(a, b, preferred_element_type=F32)


def _dot_nt(a, b):
    return lax.dot_general(a, b, (((1,), (1,)), ((), ())), preferred_element_type=F32)


def _dot_exact(a, b):
    return jnp.dot(a, b, preferred_element_type=F32, precision=lax.Precision.HIGHEST)


def _silu(x):
    return x * jax.nn.sigmoid(x)


def _softplus(x):
    return jnp.maximum(x, 0.0) + jnp.log1p(jnp.exp(-jnp.abs(x)))


def _rope(x, cos, sin_a, sin_b):
    return x * cos + pltpu.roll(x, 96, 1) * sin_a + pltpu.roll(x, 32, 1) * sin_b


def _layer_norm(y, g, b):
    mu = jnp.mean(y, -1, keepdims=True)
    yc = y - mu
    var = jnp.mean(yc * yc, -1, keepdims=True)
    return yc * lax.rsqrt(var + LN_EPS) * g + b


def _ada_kernel(c_ref, w_ref, b_ref, o_ref):
    a = _silu(c_ref[...]).astype(BF16)
    o_ref[...] = _dot(a, w_ref[...].astype(BF16)) + b_ref[...]


def _ada(cond, w_ada, b_ada):
    nl, _, nout = w_ada.shape
    nr = cond.shape[0]
    tn = 1024
    return pl.pallas_call(
        _ada_kernel,
        grid=(nl, nout // tn),
        in_specs=[pl.BlockSpec((nr, D_MODEL), lambda l, j: (0, 0)),
                  pl.BlockSpec((None, D_MODEL, tn), lambda l, j: (l, 0, j)),
                  pl.BlockSpec((None, 1, tn), lambda l, j: (l, 0, j))],
        out_specs=pl.BlockSpec((None, nr, tn), lambda l, j: (l, 0, j)),
        out_shape=jax.ShapeDtypeStruct((nl, nr, nout), F32),
        compiler_params=_cp("arbitrary", "arbitrary"),
        name="ada",
    )(cond, w_ada, b_ada.reshape(nl, 1, nout))


def _mod_kernel(x_ref, m_ref, o_ref):
    o_ref[...] = (x_ref[...] * (1.0 + m_ref[1:2, :]) + m_ref[0:1, :]).astype(BF16)


def _modulate(x, mod, l, row_fn):
    tp = x.shape[0]
    return pl.pallas_call(
        _mod_kernel,
        grid=(tp // TOK_TILE,),
        in_specs=[pl.BlockSpec((TOK_TILE, D_MODEL), lambda i: (i, 0)),
                  pl.BlockSpec((None, None, 6, D_MODEL), lambda i: (l, row_fn(i), 0, 0))],
        out_specs=pl.BlockSpec((TOK_TILE, D_MODEL), lambda i: (i, 0)),
        out_shape=jax.ShapeDtypeStruct((tp, D_MODEL), BF16),
        compiler_params=_cp("arbitrary"),
        name="modulate",
    )(x, mod)


def _mm_kernel(x_ref, w_ref, o_ref, wbf):
    @pl.when(pl.program_id(1) == 0)
    def _():
        wbf[...] = w_ref[...].astype(BF16)

    o_ref[...] = _dot(x_ref[...], wbf[...]).astype(o_ref.dtype)


def _in_proj(h, w, l, ncols, out_dtype=F32):
    tp = h.shape[0]
    tm, tn = 512, 1024
    return pl.pallas_call(
        _mm_kernel,
        grid=(ncols // tn, tp // tm),
        in_specs=[pl.BlockSpec((tm, D_MODEL), lambda j, i: (i, 0)),
                  pl.BlockSpec((None, D_MODEL, tn), lambda j, i: (l, 0, j))],
        out_specs=pl.BlockSpec((tm, tn), lambda j, i: (i, j)),
        out_shape=jax.ShapeDtypeStruct((tp, ncols), out_dtype),
        scratch_shapes=[pltpu.VMEM((D_MODEL, tn), BF16)],
        compiler_params=_cp("arbitrary", "arbitrary"),
        name="in_proj",
    )(h, w)


def _dt_kernel(h_ref, w_ref, wt_ref, b_ref, a_ref, bt_ref, at_ref, dt_ref, la_ref, lat_ref):
    h = h_ref[...]
    dt = _softplus(_dot(h, w_ref[...]) + b_ref[...])
    la = dt * -jnp.exp(a_ref[...])
    dtt = _softplus(_dot_nt(wt_ref[...], h) + bt_ref[...])
    lat = dtt * -jnp.exp(at_ref[...])
    for g in range(SSM_GROUPS):
        dt_ref[g] = dt[:, 16 * g:16 * (g + 1)]
        la_ref[g] = la[:, 16 * g:16 * (g + 1)]
        for q in range(h.shape[0] // CHUNK):
            lat_ref[g, q] = lat[16 * g:16 * (g + 1), q * CHUNK:(q + 1) * CHUNK]


def _ssd_steps(h, w_dt, dt_bias, a_log):
    tp = h.shape[0]
    tm = 512
    nq = tm // CHUNK
    return pl.pallas_call(
        _dt_kernel,
        grid=(tp // tm,),
        in_specs=[pl.BlockSpec((tm, D_MODEL), lambda i: (i, 0)),
                  pl.BlockSpec((D_MODEL, 32), lambda i: (0, 0)),
                  pl.BlockSpec((32, D_MODEL), lambda i: (0, 0)),
                  pl.BlockSpec((1, 32), lambda i: (0, 0)),
                  pl.BlockSpec((1, 32), lambda i: (0, 0)),
                  pl.BlockSpec((32, 1), lambda i: (0, 0)),
                  pl.BlockSpec((32, 1), lambda i: (0, 0))],
        out_specs=[pl.BlockSpec((SSM_GROUPS, tm, 16), lambda i: (0, i, 0)),
                   pl.BlockSpec((SSM_GROUPS, tm, 16), lambda i: (0, i, 0)),
                   pl.BlockSpec((SSM_GROUPS, nq, 16, CHUNK), lambda i: (0, i, 0, 0))],
        out_shape=[jax.ShapeDtypeStruct((SSM_GROUPS, tp, 16), F32),
                   jax.ShapeDtypeStruct((SSM_GROUPS, tp, 16), F32),
                   jax.ShapeDtypeStruct((SSM_GROUPS, tp // CHUNK, 16, CHUNK), F32)],
        compiler_params=_cp("arbitrary"),
        name="ssd_steps",
    )(h, w_dt, w_dt.T, dt_bias.reshape(1, 32), a_log.reshape(1, 32),
      dt_bias.reshape(32, 1), a_log.reshape(32, 1))


def _conv_kernel(x_ref, w_ref, b_ref, o_ref, pad_ref, *, seq):
    ct = x_ref.shape[1]
    pad = SSM_CONV // 2
    pad_ref[0:8, :] = jnp.zeros((8, ct), F32)
    pad_ref[8 + seq:16 + seq, :] = jnp.zeros((8, ct), F32)
    pad_ref[8:8 + seq, :] = x_ref[...]
    for r in range(seq // CHUNK):
        acc = jnp.broadcast_to(b_ref[...], (CHUNK, ct))
        for j in range(SSM_CONV):
            s = 8 - pad + j + r * CHUNK
            acc = acc + w_ref[j:j + 1, :] * pad_ref[s:s + CHUNK, :]
        o_ref[r * CHUNK:(r + 1) * CHUNK, :] = _silu(acc)


def _conv(u1, conv_w, conv_b, seq):
    tp = u1.shape[0]
    ct = 512
    c0 = 2560 // ct
    return pl.pallas_call(
        functools.partial(_conv_kernel, seq=seq),
        grid=(tp // seq, SSM_CONV_DIM // ct),
        in_specs=[pl.BlockSpec((seq, ct), lambda b, j: (b, c0 + j)),
                  pl.BlockSpec((SSM_CONV, ct), lambda b, j: (0, j)),
                  pl.BlockSpec((1, ct), lambda b, j: (0, j))],
        out_specs=pl.BlockSpec((seq, ct), lambda b, j: (b, j)),
        out_shape=jax.ShapeDtypeStruct((tp, SSM_CONV_DIM), F32),
        scratch_shapes=[pltpu.VMEM((seq + 16, ct), F32)],
        compiler_params=_cp("arbitrary", "arbitrary"),
        name="conv",
    )(u1, conv_w, conv_b.reshape(1, -1))


def _expand_heads(a, sel):
    hi = a.astype(BF16)
    lo = (a - hi.astype(F32)).astype(BF16)
    return _dot(hi, sel) + _dot(lo, sel)


def _ssd_kernel(*refs, seq, has_s0, want_fin):
    (xs_ref, bm_ref, cm_ref, z_ref, dt_ref, la_ref, lat_ref, dsk_ref, nrm_ref) = refs[:9]
    pos = 9
    s0_ref = None
    if has_s0:
        s0_ref = refs[pos]
        pos += 1
    o_ref = refs[pos]
    pos += 1
    fin_ref = None
    if want_fin:
        fin_ref = refs[pos]
        pos += 1
    st_ref = refs[pos]
    y_ref = refs[pos + 1]

    nc = seq // CHUNK
    hpg = SSM_HEADS // SSM_GROUPS
    ri = lax.broadcasted_iota(I32, (CHUNK, CHUNK), 0)
    ci = lax.broadcasted_iota(I32, (CHUNK, CHUNK), 1)
    lower = ri >= ci
    upper = ri <= ci
    lt = jnp.where(lower, 1.0, 0.0).astype(F32)
    ut = jnp.where(upper, 1.0, 0.0).astype(F32)
    half0 = lax.broadcasted_iota(I32, (CHUNK, 128), 1) < SSM_HEAD_DIM
    sel_row = lax.broadcasted_iota(I32, (2 * hpg, hpg * SSM_HEAD_DIM), 0)
    sel_head = lax.broadcasted_iota(I32, (2 * hpg, hpg * SSM_HEAD_DIM), 1) // SSM_HEAD_DIM
    sels = [jnp.where(sel_row == hpg * d + sel_head, 1.0, 0.0).astype(BF16) for d in range(2)]

    for d in range(2):
        if has_s0:
            st_ref[d] = jnp.concatenate([s0_ref[d, hh] for hh in range(hpg)], axis=1)
        else:
            st_ref[d] = jnp.zeros((SSM_STATE, hpg * SSM_HEAD_DIM), F32)

    def chunk(c, d):
        r0 = pl.multiple_of(c * CHUNK, CHUNK)
        la_c = la_ref[pl.ds(r0, CHUNK), :]
        dt_c = dt_ref[pl.ds(r0, CHUNK), :]
        lat_c = lat_ref[c]
        if d == 0:
            acs = _dot_exact(lt, la_c)
            acst = _dot_exact(lat_c, ut)
            mask = lower
            er = CHUNK - 1
        else:
            acs = _dot_exact(ut, la_c)
            acst = _dot_exact(lat_c, lt)
            mask = upper
            er = 0
        edge = acs[er:er + 1, :]
        sel = sels[d]
        xs_c = xs_ref[pl.ds(r0, CHUNK), :]
        b_c = bm_ref[pl.ds(r0, CHUNK), :]
        c_c = cm_ref[pl.ds(r0, CHUNK), :].astype(BF16)
        g = _dot_nt(c_c, b_c.astype(BF16))
        xdt = xs_c * _expand_heads(dt_c, sel)
        tiles = []
        for p in range(hpg // 2):
            xt = xdt[:, 128 * p:128 * (p + 1)]
            acc = None
            for q in range(2):
                hh = hpg * d + 2 * p + q
                dec = jnp.exp(jnp.where(mask, acs[:, hh:hh + 1] - acst[hh:hh + 1, :], -jnp.inf))
                v = jnp.where(half0 if q == 0 else jnp.logical_not(half0), xt, 0.0).astype(BF16)
                t = _dot((g * dec).astype(BF16), v)
                acc = t if acc is None else acc + t
            tiles.append(acc)
        y = jnp.concatenate(tiles, axis=1)
        st = st_ref[d]
        e_in = _expand_heads(jnp.exp(acs), sel)
        y = y + e_in * _dot(c_c, st.astype(BF16))
        y_ref[d, pl.ds(r0, CHUNK), :] = y
        wgt = jnp.exp(edge - acs)
        v = (xdt * _expand_heads(wgt, sel)).astype(BF16)
        st_ref[d] = st * e_in[er:er + 1, :] + _dot(b_c.T.astype(BF16), v)

    def step(c, carry):
        chunk(c, 0)
        chunk(nc - 1 - c, 1)
        return carry

    lax.fori_loop(0, nc, step, 0)

    for r in range(nc):
        sl = slice(r * CHUNK, (r + 1) * CHUNK)
        y = (y_ref[0, sl, :] + y_ref[1, sl, :] + xs_ref[sl, :] * dsk_ref[...]) * _silu(z_ref[sl, :])
        y = y * lax.rsqrt(jnp.mean(y * y, -1, keepdims=True) + RMS_EPS)
        o_ref[sl, :] = (y * nrm_ref[...]).astype(BF16)

    if want_fin:
        for d in range(2):
            st = st_ref[d]
            for hh in range(hpg):
                fin_ref[d, hh] = st[:, hh * SSM_HEAD_DIM:(hh + 1) * SSM_HEAD_DIM]


def _ssd(xc, u1, dt, la, lat, dsk, nrm, seq, l, s0):
    tp = xc.shape[0]
    nb = tp // seq
    nc = seq // CHUNK
    gw = SSM_D_INNER // SSM_GROUPS
    hpg = SSM_HEADS // SSM_GROUPS
    has_s0 = s0 is not None
    want_fin = not has_s0
    in_specs = [pl.BlockSpec((seq, gw), lambda b, g: (b, g)),
                pl.BlockSpec((seq, SSM_STATE), lambda b, g: (b, SSM_D_INNER // SSM_STATE + g)),
                pl.BlockSpec((seq, SSM_STATE), lambda b, g: (b, SSM_D_INNER // SSM_STATE + SSM_GROUPS + g)),
                pl.BlockSpec((seq, gw), lambda b, g: (b, 1536 // gw + g)),
                pl.BlockSpec((None, seq, 16), lambda b, g: (g, b, 0)),
                pl.BlockSpec((None, seq, 16), lambda b, g: (g, b, 0)),
                pl.BlockSpec((None, nc, 16, CHUNK), lambda b, g: (g, b, 0, 0)),
                pl.BlockSpec((1, gw), lambda b, g: (0, g)),
                pl.BlockSpec((1, gw), lambda b, g: (0, g))]
    args = [xc, xc, xc, u1, dt, la, lat, dsk, nrm]
    if has_s0:
        in_specs.append(pl.BlockSpec((None, None, 2, hpg, SSM_STATE, SSM_HEAD_DIM),
                                     lambda b, g: (b, l, 0, g, 0, 0)))
        args.append(s0)
    out_specs = [pl.BlockSpec((seq, gw), lambda b, g: (b, g))]
    out_shape = [jax.ShapeDtypeStruct((tp, SSM_D_INNER), BF16)]
    if want_fin:
        out_specs.append(pl.BlockSpec((None, 2, hpg, SSM_STATE, SSM_HEAD_DIM), lambda b, g: (b, 0, g, 0, 0)))
        out_shape.append(jax.ShapeDtypeStruct((nb, 2, SSM_HEADS, SSM_STATE, SSM_HEAD_DIM), F32))
    res = pl.pallas_call(
        functools.partial(_ssd_kernel, seq=seq, has_s0=has_s0, want_fin=want_fin),
        grid=(nb, SSM_GROUPS),
        in_specs=in_specs,
        out_specs=out_specs,
        out_shape=out_shape,
        scratch_shapes=[pltpu.VMEM((2, SSM_STATE, gw), F32), pltpu.VMEM((2, seq, gw), F32)],
        compiler_params=_cp("arbitrary", "arbitrary"),
        name="ssd_scan",
    )(*args)
    return (res[0], res[1]) if want_fin else (res[0], None)


def _ret_kernel(*refs, seq, l, latent):
    dec_ref, q_ref, k_ref, v_ref, g_ref = refs[:5]
    pos = 5
    if latent:
        cos_ref, sa_ref, sb_ref, s0_ref = refs[pos:pos + 4]
        pos += 4
    o_ref = refs[pos]
    pos += 1
    fin_ref = None
    if not latent:
        fin_ref = refs[pos]
        pos += 1
    qs_ref, ks_ref, st_ref, y_ref = refs[pos:pos + 4]

    nc = seq // CHUNK
    hd = pl.program_id(1)
    if latent:
        qs_ref[...] = _rope(q_ref[...], cos_ref[...], sa_ref[...], sb_ref[...]).astype(BF16)
        ks_ref[...] = _rope(k_ref[...] * (RET_QK_DIM ** -0.5), cos_ref[...], sa_ref[...], sb_ref[...])
        for d in range(2):
            st_ref[d] = s0_ref[d]
    else:
        qs_ref[...] = q_ref[...].astype(BF16)
        ks_ref[...] = k_ref[...] * (RET_QK_DIM ** -0.5)
        for d in range(2):
            st_ref[d] = jnp.zeros((RET_QK_DIM, RET_V_DIM), F32)

    ri = lax.broadcasted_iota(I32, (CHUNK, CHUNK), 0)
    ci = lax.broadcasted_iota(I32, (CHUNK, CHUNK), 1)
    dist = (ri - ci).astype(F32)
    rowi = lax.broadcasted_iota(I32, (CHUNK, 1), 0).astype(F32)

    consts = []
    for d in range(2):
        raw = jnp.full((1, 1), dec_ref[l * 2 * RET_HEADS + d * RET_HEADS + hd], F32)
        lg = -_softplus(-raw)
        if d == 0:
            dec = jnp.exp(jnp.where(ri >= ci, dist * lg, -jnp.inf))
            e_in = jnp.exp((rowi + 1.0) * lg)
            wgt = jnp.exp((CHUNK - 1.0 - rowi) * lg)
        else:
            dec = jnp.exp(jnp.where(ri <= ci, -dist * lg, -jnp.inf))
            e_in = jnp.exp((CHUNK - rowi) * lg)
            wgt = jnp.exp(rowi * lg)
        consts.append((dec, e_in, wgt, jnp.exp(CHUNK * lg)))

    def chunk(cc, d):
        dec, e_in, wgt, full = consts[d]
        r0 = pl.multiple_of(cc * CHUNK, CHUNK)
        q = qs_ref[pl.ds(r0, CHUNK), :]
        k = ks_ref[pl.ds(r0, CHUNK), :]
        v = v_ref[pl.ds(r0, CHUNK), :].astype(BF16)
        g = _dot_nt(q, k.astype(BF16))
        st = st_ref[d]
        y_ref[d, pl.ds(r0, CHUNK), :] = _dot((g * dec).astype(BF16), v) + e_in * _dot(q, st.astype(BF16))
        st_ref[d] = st * full + _dot((k * wgt).T.astype(BF16), v)

    def step(c, carry):
        chunk(c, 0)
        chunk(nc - 1 - c, 1)
        return carry

    lax.fori_loop(0, nc, step, 0)

    for r in range(nc):
        sl = slice(r * CHUNK, (r + 1) * CHUNK)
        y = y_ref[0, sl, :] + y_ref[1, sl, :]
        y = y * lax.rsqrt(jnp.mean(y * y, -1, keepdims=True) + RMS_EPS)
        o_ref[sl, :] = (y * _silu(g_ref[sl, :])).astype(BF16)

    if not latent:
        for d in range(2):
            fin_ref[d] = st_ref[d]


def _retention(u2, ret_decay, seq, l, rope, s0):
    tp = u2.shape[0]
    nb = tp // seq
    latent = s0 is not None
    in_specs = [pl.BlockSpec((seq, RET_QK_DIM), lambda b, h, *_: (b, h)),
                pl.BlockSpec((seq, RET_QK_DIM), lambda b, h, *_: (b, RET_HEADS + h)),
                pl.BlockSpec((seq, RET_V_DIM), lambda b, h, *_: (b, 1024 // RET_V_DIM + h)),
                pl.BlockSpec((seq, RET_V_DIM), lambda b, h, *_: (b, 2048 // RET_V_DIM + h))]
    args = [u2, u2, u2, u2]
    if latent:
        for t in rope:
            in_specs.append(pl.BlockSpec((seq, HEAD_DIM), lambda b, h, *_: (0, 0)))
            args.append(t)
        in_specs.append(pl.BlockSpec((None, None, 2, None, RET_QK_DIM, RET_V_DIM),
                                     lambda b, h, *_: (b, l, 0, h, 0, 0)))
        args.append(s0)
    out_specs = [pl.BlockSpec((seq, RET_V_DIM), lambda b, h, *_: (b, h))]
    out_shape = [jax.ShapeDtypeStruct((tp, RET_HEADS * RET_V_DIM), BF16)]
    if not latent:
        out_specs.append(pl.BlockSpec((None, 2, None, RET_QK_DIM, RET_V_DIM), lambda b, h, *_: (b, 0, h, 0, 0)))
        out_shape.append(jax.ShapeDtypeStruct((nb, 2, RET_HEADS, RET_QK_DIM, RET_V_DIM), F32))
    res = pl.pallas_call(
        functools.partial(_ret_kernel, seq=seq, l=l, latent=latent),
        grid_spec=pltpu.PrefetchScalarGridSpec(
            num_scalar_prefetch=1,
            grid=(nb, RET_HEADS),
            in_specs=in_specs,
            out_specs=out_specs,
            scratch_shapes=[pltpu.VMEM((seq, RET_QK_DIM), BF16),
                            pltpu.VMEM((seq, RET_QK_DIM), F32),
                            pltpu.VMEM((2, RET_QK_DIM, RET_V_DIM), F32),
                            pltpu.VMEM((2, seq, RET_V_DIM), F32)]),
        out_shape=out_shape,
        compiler_params=_cp("arbitrary", "arbitrary"),
        name="retention",
    )(ret_decay.reshape(-1), *args)
    return (res[0], None) if latent else (res[0], res[1])


def _softmax_pv(s, v, sink):
    m = jnp.max(s, -1, keepdims=True)
    if sink is not None:
        m = jnp.maximum(m, sink)
    e = jnp.exp(s - m)
    den = jnp.sum(e, -1, keepdims=True)
    if sink is not None:
        den = den + jnp.exp(sink - m)
    return _dot(e.astype(BF16), v) / den


def _ctx_attn_kernel(sink_ref, qa_ref, ka_ref, va_ref, qn_ref, kn_ref, vn_ref, ya_ref, yn_ref, *, l):
    scale = HEAD_DIM ** -0.5
    for h in range(WIN_HEADS):
        kv = h // WIN_GROUP
        hs = slice(h * HEAD_DIM, (h + 1) * HEAD_DIM)
        ks = slice(kv * HEAD_DIM, (kv + 1) * HEAD_DIM)
        s = _dot_nt(qa_ref[:, hs].astype(BF16), ka_ref[:, ks].astype(BF16)) * scale
        ya_ref[:, hs] = _softmax_pv(s, va_ref[:, ks].astype(BF16), sink_ref[l * WIN_HEADS + h]).astype(BF16)
    for h in range(NA_HEADS):
        hs = slice(h * HEAD_DIM, (h + 1) * HEAD_DIM)
        s = _dot_nt(qn_ref[:, hs].astype(BF16), kn_ref[:, hs].astype(BF16)) * scale
        yn_ref[:, hs] = _softmax_pv(s, vn_ref[:, hs].astype(BF16), None).astype(BF16)


def _ctx_attention(u1, u2, win_sink, seq, l):
    tp = u1.shape[0]
    nb = tp // seq
    kvw = WIN_KV_HEADS * HEAD_DIM
    hw = NA_HEADS * HEAD_DIM
    return pl.pallas_call(
        functools.partial(_ctx_attn_kernel, l=l),
        grid_spec=pltpu.PrefetchScalarGridSpec(
            num_scalar_prefetch=1,
            grid=(nb,),
            in_specs=[pl.BlockSpec((seq, hw), lambda b, *_: (b, 0)),
                      pl.BlockSpec((seq, kvw), lambda b, *_: (b, 1024 // kvw)),
                      pl.BlockSpec((seq, kvw), lambda b, *_: (b, 1280 // kvw)),
                      pl.BlockSpec((seq, hw), lambda b, *_: (b, 3072 // hw)),
                      pl.BlockSpec((seq, hw), lambda b, *_: (b, 4096 // hw)),
                      pl.BlockSpec((seq, hw), lambda b, *_: (b, 5120 // hw))],
            out_specs=[pl.BlockSpec((seq, hw), lambda b, *_: (b, 0)),
                       pl.BlockSpec((seq, hw), lambda b, *_: (b, 0))]),
        out_shape=[jax.ShapeDtypeStruct((tp, hw), BF16), jax.ShapeDtypeStruct((tp, hw), BF16)],
        compiler_params=_cp("arbitrary"),
        name="ctx_attention",
    )(win_sink.reshape(-1), u1, u1, u1, u2, u2, u2)


def _win_kernel(sink_ref, q_ref, k_ref, v_ref, kc_ref, vc_ref, cos_ref, sa_ref, sb_ref, o_ref,
                kr_ref, vb_ref, *, seq, l):
    scale = HEAD_DIM ** -0.5
    kvh = pl.program_id(1)
    nq = seq // CHUNK
    band = 3 * CHUNK
    rows = WIN_GROUP * CHUNK
    kr_ref[...] = _rope(k_ref[...], cos_ref[...], sa_ref[...], sb_ref[...]).astype(BF16)
    vb_ref[...] = v_ref[...].astype(BF16)
    kc = kc_ref[...].astype(BF16)
    vc = vc_ref[...].astype(BF16)
    rg = lax.broadcasted_iota(I32, (rows, 1), 0) // CHUNK
    sink = jnp.zeros((rows, 1), F32)
    for g in range(WIN_GROUP):
        sink = jnp.where(rg == g, sink_ref[l * WIN_HEADS + kvh * WIN_GROUP + g], sink)
    qoff = lax.broadcasted_iota(I32, (rows, band), 0) % CHUNK
    koff = lax.broadcasted_iota(I32, (rows, band), 1)

    def body(n, carry):
        r0 = pl.multiple_of(n * CHUNK, CHUNK)
        cos = cos_ref[pl.ds(r0, CHUNK), :]
        sa = sa_ref[pl.ds(r0, CHUNK), :]
        sb = sb_ref[pl.ds(r0, CHUNK), :]
        qs = jnp.concatenate(
            [_rope(q_ref[pl.ds(r0, CHUNK), g * HEAD_DIM:(g + 1) * HEAD_DIM], cos, sa, sb)
             for g in range(WIN_GROUP)], axis=0).astype(BF16)
        start = pl.multiple_of(jnp.clip((n - 1) * CHUNK, 0, seq - band), CHUNK)
        kb = kr_ref[pl.ds(start, band), :]
        vb = vb_ref[pl.ds(start, band), :]
        s_loc = _dot_nt(qs, kb) * scale
        ok = jnp.abs(r0 + qoff - (start + koff)) <= WINDOW
        s_loc = jnp.where(ok, s_loc, -jnp.inf)
        s_ctx = _dot_nt(qs, kc) * scale
        m = jnp.maximum(jnp.maximum(jnp.max(s_loc, -1, keepdims=True), jnp.max(s_ctx, -1, keepdims=True)), sink)
        e_loc = jnp.exp(s_loc - m)
        e_ctx = jnp.exp(s_ctx - m)
        den = jnp.sum(e_loc, -1, keepdims=True) + jnp.sum(e_ctx, -1, keepdims=True) + jnp.exp(sink - m)
        o = (_dot(e_ctx.astype(BF16), vc) + _dot(e_loc.astype(BF16), vb)) / den
        for g in range(WIN_GROUP):
            o_ref[pl.ds(r0, CHUNK), g * HEAD_DIM:(g + 1) * HEAD_DIM] = o[g * CHUNK:(g + 1) * CHUNK, :].astype(BF16)
        return carry

    lax.fori_loop(0, nq, body, 0)


def _win_attention(u1, cache_k, cache_v, win_sink, rope, seq, l):
    tp = u1.shape[0]
    nb = tp // seq
    past = cache_k.shape[2]
    gw = WIN_GROUP * HEAD_DIM
    ck = cache_k.reshape(nb, DEPTH, past, WIN_KV_HEADS * HEAD_DIM)
    cv = cache_v.reshape(nb, DEPTH, past, WIN_KV_HEADS * HEAD_DIM)
    rope_spec = pl.BlockSpec((seq, HEAD_DIM), lambda b, h, *_: (0, 0))
    return pl.pallas_call(
        functools.partial(_win_kernel, seq=seq, l=l),
        grid_spec=pltpu.PrefetchScalarGridSpec(
            num_scalar_prefetch=1,
            grid=(nb, WIN_KV_HEADS),
            in_specs=[pl.BlockSpec((seq, gw), lambda b, h, *_: (b, h)),
                      pl.BlockSpec((seq, HEAD_DIM), lambda b, h, *_: (b, 1024 // HEAD_DIM + h)),
                      pl.BlockSpec((seq, HEAD_DIM), lambda b, h, *_: (b, 1280 // HEAD_DIM + h)),
                      pl.BlockSpec((None, None, past, HEAD_DIM), lambda b, h, *_: (b, l, 0, h)),
                      pl.BlockSpec((None, None, past, HEAD_DIM), lambda b, h, *_: (b, l, 0, h)),
                      rope_spec, rope_spec, rope_spec],
            out_specs=pl.BlockSpec((seq, gw), lambda b, h, *_: (b, h)),
            scratch_shapes=[pltpu.VMEM((seq, HEAD_DIM), BF16), pltpu.VMEM((seq, HEAD_DIM), BF16)]),
        out_shape=jax.ShapeDtypeStruct((tp, WIN_HEADS * HEAD_DIM), BF16),
        compiler_params=_cp("arbitrary", "arbitrary"),
        name="win_attention",
    )(win_sink.reshape(-1), u1, u1, u1, ck, cv, *rope)


def _na_window_start(j, rows):
    kr = min(NA_ROWS, rows)
    rs = jnp.clip(2 * j - kr // 2, 0, rows - kr)
    return jnp.minimum(rs, rows - NA_WIN_ROWS)


def _na_kernel(q_ref, k_ref, v_ref, kc_ref, vc_ref, bias_ref, o_ref, kb_ref, vb_ref, *, seq):
    scale = HEAD_DIM ** -0.5
    rows = seq // GRID_W
    nq = seq // CHUNK
    win = NA_WIN_ROWS * GRID_W
    kb_ref[...] = k_ref[...].astype(BF16)
    vb_ref[...] = v_ref[...].astype(BF16)
    kc = kc_ref[...].astype(BF16)
    vc = vc_ref[...].astype(BF16)

    def body(j, carry):
        r0 = pl.multiple_of(j * CHUNK, CHUNK)
        q = q_ref[pl.ds(r0, CHUNK), :].astype(BF16)
        start = pl.multiple_of(_na_window_start(j, rows) * GRID_W, GRID_W)
        kw = kb_ref[pl.ds(start, win), :]
        vw = vb_ref[pl.ds(start, win), :]
        s_loc = _dot_nt(q, kw) * scale + bias_ref[j]
        s_ctx = _dot_nt(q, kc) * scale
        m = jnp.maximum(jnp.max(s_loc, -1, keepdims=True), jnp.max(s_ctx, -1, keepdims=True))
        e_loc = jnp.exp(s_loc - m)
        e_ctx = jnp.exp(s_ctx - m)
        den = jnp.sum(e_loc, -1, keepdims=True) + jnp.sum(e_ctx, -1, keepdims=True)
        o = (_dot(e_ctx.astype(BF16), vc) + _dot(e_loc.astype(BF16), vw)) / den
        o_ref[pl.ds(r0, CHUNK), :] = o.astype(BF16)
        return carry

    lax.fori_loop(0, nq, body, 0)


def _na_bias_table(rpb, seq):
    nh = rpb.shape[0]
    rows = seq // GRID_W
    kr = min(NA_ROWS, rows)
    nq = seq // CHUNK
    ndr, ndc = 2 * NA_ROWS - 1, 2 * NA_COLS - 1
    halves = CHUNK // GRID_W
    qc = np.arange(GRID_W)[:, None]
    kc = np.arange(GRID_W)[None, :]
    dc = (np.clip(kc - qc, 1 - NA_COLS, NA_COLS - 1) + (NA_COLS - 1)).reshape(-1)
    cs = np.clip(qc - NA_COLS // 2, 0, GRID_W - NA_COLS)
    col_ok = ((kc >= cs) & (kc < cs + NA_COLS)).reshape(-1)
    onehot = (dc[None, :] == np.arange(ndc)[:, None]).astype(np.float32)
    blocks = jnp.einsum('hrd,dq->hrq', rpb, onehot, precision=lax.Precision.HIGHEST)
    blocks = jnp.where(col_ok[None, None, :], blocks, NEG_BIG)
    blocks = jnp.concatenate([blocks, jnp.full((nh, 1, GRID_W * GRID_W), NEG_BIG, F32)], axis=1)
    blocks = blocks.reshape(nh, ndr + 1, GRID_W, GRID_W)
    j = np.arange(nq)[:, None, None]
    r = 2 * j + np.arange(halves)[None, :, None]
    ws = np.minimum(np.clip(2 * j - kr // 2, 0, rows - kr), rows - NA_WIN_ROWS)
    krow = ws + np.arange(NA_WIN_ROWS)[None, None, :]
    rs = np.clip(r - kr // 2, 0, rows - kr)
    blk = np.where((krow >= rs) & (krow < rs + kr), krow - r + (NA_ROWS - 1), ndr)
    t = jnp.take(blocks, jnp.asarray(blk.reshape(-1), I32), axis=1)
    t = t.reshape(nh, nq, halves, NA_WIN_ROWS, GRID_W, GRID_W).transpose(0, 1, 2, 4, 3, 5)
    return t.reshape(nh, nq, CHUNK, NA_WIN_ROWS * GRID_W)


def _na_attention(u2, cache_k, cache_v, bias, seq, l):
    tp = u2.shape[0]
    nb = tp // seq
    past = cache_k.shape[2]
    nq = seq // CHUNK
    win = NA_WIN_ROWS * GRID_W
    ck = cache_k.reshape(nb, DEPTH, past, NA_HEADS * HEAD_DIM)
    cv = cache_v.reshape(nb, DEPTH, past, NA_HEADS * HEAD_DIM)
    return pl.pallas_call(
        functools.partial(_na_kernel, seq=seq),
        grid=(nb, NA_HEADS),
        in_specs=[pl.BlockSpec((seq, HEAD_DIM), lambda b, h: (b, 3072 // HEAD_DIM + h)),
                  pl.BlockSpec((seq, HEAD_DIM), lambda b, h: (b, 4096 // HEAD_DIM + h)),
                  pl.BlockSpec((seq, HEAD_DIM), lambda b, h: (b, 5120 // HEAD_DIM + h)),
                  pl.BlockSpec((None, None, past, HEAD_DIM), lambda b, h: (b, l, 0, h)),
                  pl.BlockSpec((None, None, past, HEAD_DIM), lambda b, h: (b, l, 0, h)),
                  pl.BlockSpec((None, nq, CHUNK, win), lambda b, h: (h, 0, 0, 0))],
        out_specs=pl.BlockSpec((seq, HEAD_DIM), lambda b, h: (b, h)),
        out_shape=jax.ShapeDtypeStruct((tp, NA_HEADS * HEAD_DIM), BF16),
        scratch_shapes=[pltpu.VMEM((seq, HEAD_DIM), BF16), pltpu.VMEM((seq, HEAD_DIM), BF16)],
        compiler_params=_cp("arbitrary", "arbitrary"),
        name="na_attention",
    )(u2, u2, u2, ck, cv, bias)


def _merge_kernel(ya_ref, ys_ref, yr_ref, yn_ref, g0_ref, g1_ref, g2_ref, g3_ref, w_ref, o_ref, wbf):
    @pl.when(pl.program_id(1) == 0)
    def _():
        wbf[...] = w_ref[...].astype(BF16)

    acc = None
    for n, (b_ref, g_ref) in enumerate(((ya_ref, g0_ref), (ys_ref, g1_ref), (yr_ref, g2_ref), (yn_ref, g3_ref))):
        t = jax.nn.sigmoid(g_ref[...].astype(F32)) * _dot(b_ref[...], wbf[n])
        acc = t if acc is None else acc + t
    o_ref[...] = acc.astype(BF16)


def _merge(ya, ys, yr, yn, gates, w_branch, l):
    tp = ya.shape[0]
    tm, tn = 512, 512
    per = D_MODEL // tn
    br = pl.BlockSpec((tm, BRANCH_W), lambda j, i: (i, 0))
    gate = lambda n: pl.BlockSpec((tm, tn), lambda j, i: (i, n * per + j))
    return pl.pallas_call(
        _merge_kernel,
        grid=(per, tp // tm),
        in_specs=[br, br, br, br, gate(0), gate(1), gate(2), gate(3),
                  pl.BlockSpec((None, N_BRANCH, BRANCH_W, tn), lambda j, i: (l, 0, 0, j))],
        out_specs=pl.BlockSpec((tm, tn), lambda j, i: (i, j)),
        out_shape=jax.ShapeDtypeStruct((tp, D_MODEL), BF16),
        scratch_shapes=[pltpu.VMEM((N_BRANCH, BRANCH_W, tn), BF16)],
        compiler_params=_cp("arbitrary", "arbitrary"),
        name="merge",
    )(ya, ys, yr, yn, gates, gates, gates, gates, w_branch)


def _out_kernel(m_ref, w_ref, x_ref, mod_ref, g_ref, b_ref, o_ref):
    mix = _dot(m_ref[...], w_ref[...])
    y = ALPHA * x_ref[...] + mod_ref[2:3, :] * mix
    o_ref[...] = _layer_norm(y, g_ref[...], b_ref[...])


def _out_proj(merged, w_out_bf, x, mod, ln_g, ln_b, l, row_fn):
    tp = x.shape[0]
    tm = TOK_TILE
    row = pl.BlockSpec((tm, D_MODEL), lambda i: (i, 0))
    vec = pl.BlockSpec((None, 1, D_MODEL), lambda i: (l, 0, 0))
    return pl.pallas_call(
        _out_kernel,
        grid=(tp // tm,),
        in_specs=[row,
                  pl.BlockSpec((None, D_MODEL, D_MODEL), lambda i: (l, 0, 0)),
                  row,
                  pl.BlockSpec((None, None, 6, D_MODEL), lambda i: (l, row_fn(i), 0, 0)),
                  vec, vec],
        out_specs=row,
        out_shape=jax.ShapeDtypeStruct((tp, D_MODEL), F32),
        compiler_params=_cp("arbitrary"),
        name="out_proj_ln1",
    )(merged, w_out_bf, x, mod, ln_g.reshape(DEPTH, 1, D_MODEL), ln_b.reshape(DEPTH, 1, D_MODEL))


def _pack_pair(a, b):
    ab = lax.bitcast_convert_type(a.astype(BF16).astype(F32), jnp.uint32)
    bb = lax.bitcast_convert_type(b.astype(BF16).astype(F32), jnp.uint32)
    return (ab & jnp.uint32(0xFFFF0000)) | (bb >> 16)


def _unpack_pair(p):
    a = lax.bitcast_convert_type(p & jnp.uint32(0xFFFF0000), F32)
    b = lax.bitcast_convert_type(p << 16, F32)
    return a, b


HALF = D_MODEL // 2
ROW_TILES = HALF // 128


def _store_rows(ref, packed):
    n = packed.shape[0]
    for c in range(ROW_TILES):
        ref[pl.ds(c, n, stride=ROW_TILES), :] = packed[:, c * 128:(c + 1) * 128]


def _load_rows(ref):
    n = ref.shape[0] // ROW_TILES
    return jnp.concatenate([ref[pl.ds(c, n, stride=ROW_TILES), :] for c in range(ROW_TILES)], axis=1)


def _row(ref, start):
    return ref.at[pl.ds(pl.multiple_of(start, ROW_TILES), ROW_TILES), :]


def _router_kernel(xc_ref, xl_ref, mod_ref, rw_ref, rb_ref, xp_ref, idx_ref, wt_ref, pos_ref, cnt_ref, carry,
                   *, n_ctx_tiles):
    i = pl.program_id(0)

    @pl.when(i == 0)
    def _():
        carry[...] = jnp.zeros(carry.shape, F32)

    x = jnp.where(i < n_ctx_tiles, xc_ref[...], xl_ref[...])
    xm = x * (1.0 + mod_ref[4:5, :]) + mod_ref[3:4, :]
    _store_rows(xp_ref, _pack_pair(xm[:, :HALF], xm[:, HALF:]))
    tm = xm.shape[0]
    scores = jax.nn.sigmoid(_dot_nt(rw_ref[...], xm.astype(BF16)))
    biased = scores + rb_ref[...]
    per = N_EXPERTS // N_EXPERT_GROUPS
    sub = lax.broadcasted_iota(I32, (per, tm), 0)
    bg = [biased[per * g:per * (g + 1), :] for g in range(N_EXPERT_GROUPS)]
    sg = [scores[per * g:per * (g + 1), :] for g in range(N_EXPERT_GROUPS)]
    gs = []
    for g in range(N_EXPERT_GROUPS):
        m1 = jnp.max(bg[g], 0, keepdims=True)
        i1 = jnp.min(jnp.where(bg[g] == m1, sub, per), 0, keepdims=True)
        m2 = jnp.max(jnp.where(sub == i1, -jnp.inf, bg[g]), 0, keepdims=True)
        gs.append(m1 + m2)
    cand = []
    for g in range(N_EXPERT_GROUPS):
        rank = jnp.zeros((1, tm), I32)
        for g2 in range(N_EXPERT_GROUPS):
            if g2 == g:
                continue
            ahead = (gs[g2] > gs[g]) | ((gs[g2] == gs[g]) & (g2 < g))
            rank = rank + ahead.astype(I32)
        cand.append(jnp.where(rank < TOPK_GROUPS, bg[g], -jnp.inf))
    eidx = [sub + per * g for g in range(N_EXPERT_GROUPS)]
    idx_rows, w_rows = [], []
    for _ in range(TOP_K):
        m = None
        for g in range(N_EXPERT_GROUPS):
            t = jnp.max(cand[g], 0, keepdims=True)
            m = t if m is None else jnp.maximum(m, t)
        ik = None
        for g in range(N_EXPERT_GROUPS):
            t = jnp.min(jnp.where(cand[g] == m, eidx[g], N_EXPERTS), 0, keepdims=True)
            ik = t if ik is None else jnp.minimum(ik, t)
        wk = jnp.zeros((1, tm), F32)
        for g in range(N_EXPERT_GROUPS):
            hit = eidx[g] == ik
            wk = wk + jnp.sum(jnp.where(hit, sg[g], 0.0), 0, keepdims=True)
            cand[g] = jnp.where(hit, -jnp.inf, cand[g])
        idx_rows.append(ik)
        w_rows.append(wk)
    tot = w_rows[0]
    for wk in w_rows[1:]:
        tot = tot + wk
    idx_ref[...] = jnp.concatenate(idx_rows, axis=0)
    wt_ref[...] = jnp.concatenate([wk / tot * ROUTED_SCALE for wk in w_rows], axis=0)

    member = []
    for g in range(N_EXPERT_GROUPS):
        sel = eidx[g] == idx_rows[0]
        for k in range(1, TOP_K):
            sel = sel | (eidx[g] == idx_rows[k])
        member.append(jnp.where(sel, 1.0, 0.0))
    member = jnp.concatenate(member, axis=0)
    earlier = (lax.broadcasted_iota(I32, (tm, tm), 0) < lax.broadcasted_iota(I32, (tm, tm), 1))
    before = carry[...] + _dot(member.astype(BF16), jnp.where(earlier, 1.0, 0.0).astype(BF16))
    pos_rows = []
    for k in range(TOP_K):
        pk = jnp.zeros((1, tm), F32)
        for g in range(N_EXPERT_GROUPS):
            pk = pk + jnp.sum(jnp.where(eidx[g] == idx_rows[k], before[per * g:per * (g + 1), :], 0.0),
                              0, keepdims=True)
        pos_rows.append(pk)
    pos_ref[...] = jnp.concatenate(pos_rows, axis=0).astype(I32)
    carry[...] = carry[...] + jnp.sum(member, axis=1, keepdims=True)
    cnt_ref[...] = carry[...]


def _router(x1c, x1l, mod, rwt_bf, rbias, l, n_lat_tiles_per_seq):
    tc, tl = x1c.shape[0], x1l.shape[0]
    tm = TOK_TILE
    nct, nlt = tc // tm, tl // tm
    t = tc + tl
    row_fn = lambda i: jnp.where(i < nct, 0, 1 + (i - nct) // n_lat_tiles_per_seq)
    return pl.pallas_call(
        functools.partial(_router_kernel, n_ctx_tiles=nct),
        grid=(nct + nlt,),
        in_specs=[pl.BlockSpec((tm, D_MODEL), lambda i: (jnp.minimum(i, nct - 1), 0)),
                  pl.BlockSpec((tm, D_MODEL), lambda i: (jnp.maximum(i - nct, 0), 0)),
                  pl.BlockSpec((None, None, 6, D_MODEL), lambda i: (l, row_fn(i), 0, 0)),
                  pl.BlockSpec((N_EXPERTS, D_MODEL), lambda i: (0, 0)),
                  pl.BlockSpec((N_EXPERTS, 1), lambda i: (0, 0))],
        out_specs=[pl.BlockSpec((tm * ROW_TILES, 128), lambda i: (i, 0)),
                   pl.BlockSpec((TOP_K, tm), lambda i: (0, i)),
                   pl.BlockSpec((TOP_K, tm), lambda i: (0, i)),
                   pl.BlockSpec((TOP_K, tm), lambda i: (0, i)),
                   pl.BlockSpec((N_EXPERTS, 1), lambda i: (0, 0))],
        out_shape=[jax.ShapeDtypeStruct((t * ROW_TILES, 128), jnp.uint32),
                   jax.ShapeDtypeStruct((TOP_K, t), I32),
                   jax.ShapeDtypeStruct((TOP_K, t), F32),
                   jax.ShapeDtypeStruct((TOP_K, t), I32),
                   jax.ShapeDtypeStruct((N_EXPERTS, 1), F32)],
        scratch_shapes=[pltpu.VMEM((N_EXPERTS, 1), F32)],
        compiler_params=_cp("arbitrary"),
        name="router",
    )(x1c, x1l, mod, rwt_bf, rbias.reshape(N_EXPERTS, 1))


DMA_UNROLL = 8
N_PAD_SLOTS = N_EXPERTS * MOE_BM


def _dispatch_kernel(pad_ref, dst_ref, xp_ref, xs_hbm, zrow, sem, zsem):
    tm = xp_ref.shape[0] // ROW_TILES
    n = TOP_K * tm

    @pl.when(pl.program_id(0) == 0)
    def _():
        zrow[...] = jnp.zeros(zrow.shape, jnp.uint32)

        def z_issue(j, carry):
            for p in range(2):
                pltpu.make_async_copy(zrow, _row(xs_hbm, pad_ref[2 * j + p]), zsem).start(priority=p)
            return carry

        def z_drain(s, carry):
            pltpu.make_async_copy(zrow, _row(xs_hbm, 0), zsem).wait()
            return carry

        lax.fori_loop(0, N_PAD_SLOTS // 2, z_issue, 0, unroll=DMA_UNROLL // 2)
        lax.fori_loop(0, N_PAD_SLOTS, z_drain, 0, unroll=DMA_UNROLL)

    for k in range(TOP_K):
        def issue(j, carry, k=k):
            for p in range(2):
                r = 2 * j + p
                pltpu.make_async_copy(_row(xp_ref, r * ROW_TILES), _row(xs_hbm, dst_ref[0, k * tm + r]),
                                      sem).start(priority=p)
            return carry

        lax.fori_loop(0, tm // 2, issue, 0, unroll=DMA_UNROLL // 2)

    def drain(s, carry):
        pltpu.make_async_copy(_row(xp_ref, 0), _row(xs_hbm, 0), sem).wait()
        return carry

    lax.fori_loop(0, n, drain, 0, unroll=DMA_UNROLL)


def _dispatch(xp, dest_tiles, pad_dst, n_rows):
    t = xp.shape[0] // ROW_TILES
    tm = TOK_TILE
    return pl.pallas_call(
        _dispatch_kernel,
        grid_spec=pltpu.PrefetchScalarGridSpec(
            num_scalar_prefetch=1,
            grid=(t // tm,),
            in_specs=[pl.BlockSpec((None, 1, TOP_K * tm), lambda i, pad: (i, 0, 0), memory_space=pltpu.SMEM),
                      pl.BlockSpec((tm * ROW_TILES, 128), lambda i, pad: (i, 0))],
            out_specs=pl.BlockSpec(memory_space=pl.ANY),
            scratch_shapes=[pltpu.VMEM((ROW_TILES, 128), jnp.uint32),
                            pltpu.SemaphoreType.DMA(()), pltpu.SemaphoreType.DMA(())]),
        out_shape=jax.ShapeDtypeStruct((n_rows * ROW_TILES, 128), jnp.uint32),
        compiler_params=_cp("arbitrary"),
        name="dispatch",
    )(pad_dst, dest_tiles, xp)


def _experts_kernel(be_ref, nu_ref, x_ref, wgu_ref, wdn_ref, y_ref, wgu_bf, wdn_bf):
    i = pl.program_id(0)

    @pl.when(i < nu_ref[0])
    def _():
        prev = be_ref[jnp.maximum(i - 1, 0)]

        @pl.when((i == 0) | (be_ref[i] != prev))
        def _():
            wgu_bf[...] = wgu_ref[...].astype(BF16)
            wdn_bf[...] = wdn_ref[...].astype(BF16)

        xa, xb = _unpack_pair(_load_rows(x_ref))
        hgu = _dot(xa.astype(BF16), wgu_bf[:HALF, :]) + _dot(xb.astype(BF16), wgu_bf[HALF:, :])
        act = (_silu(hgu[:, :EXPERT_FF]) * hgu[:, EXPERT_FF:]).astype(BF16)
        y = _dot(act, wdn_bf[...])
        _store_rows(y_ref, _pack_pair(y[:, :HALF], y[:, HALF:]))

    @pl.when(i >= nu_ref[0])
    def _():
        y_ref[...] = jnp.zeros(y_ref.shape, jnp.uint32)


def _experts(xs, blk_e, n_used, w_gu, w_dn, l):
    nblk = blk_e.shape[0]
    last = lambda i, be, nu: jnp.minimum(i, nu[0] - 1)
    return pl.pallas_call(
        _experts_kernel,
        grid_spec=pltpu.PrefetchScalarGridSpec(
            num_scalar_prefetch=2,
            grid=(nblk,),
            in_specs=[pl.BlockSpec((MOE_BM * ROW_TILES, 128), lambda i, be, nu: (last(i, be, nu), 0)),
                      pl.BlockSpec((None, None, D_MODEL, 2 * EXPERT_FF),
                                   lambda i, be, nu: (l, be[last(i, be, nu)], 0, 0)),
                      pl.BlockSpec((None, None, EXPERT_FF, D_MODEL),
                                   lambda i, be, nu: (l, be[last(i, be, nu)], 0, 0))],
            out_specs=pl.BlockSpec((MOE_BM * ROW_TILES, 128), lambda i, be, nu: (i, 0)),
            scratch_shapes=[pltpu.VMEM((D_MODEL, 2 * EXPERT_FF), BF16),
                            pltpu.VMEM((EXPERT_FF, D_MODEL), BF16)]),
        out_shape=jax.ShapeDtypeStruct((nblk * MOE_BM * ROW_TILES, 128), jnp.uint32),
        compiler_params=_cp("arbitrary"),
        name="experts",
    )(blk_e, n_used, xs, w_gu, w_dn)


FIN_TM = 128


def _final_kernel(dcur_ref, dnxt_ref, ys_hbm, x1_ref, xp_ref, wk_ref, mod_ref, g_ref, b_ref, sgu_ref, sdn_ref,
                  o_ref, gbuf, sem):
    n = TOP_K * FIN_TM
    i = pl.program_id(0)
    slot = i % 2

    def fetch(d_ref, sl):
        for k in range(TOP_K):
            def issue(j, carry, k=k):
                for p in range(2):
                    r = 2 * j + p
                    pltpu.make_async_copy(_row(ys_hbm, d_ref[0, k * FIN_TM + r]),
                                          _row(gbuf.at[sl, k], r * ROW_TILES), sem.at[sl]).start(priority=p)
                return carry

            lax.fori_loop(0, FIN_TM // 2, issue, 0, unroll=DMA_UNROLL // 2)

    @pl.when(i == 0)
    def _():
        fetch(dcur_ref, 0)

    @pl.when(i + 1 < pl.num_programs(0))
    def _():
        fetch(dnxt_ref, 1 - slot)

    xa, xb = _unpack_pair(_load_rows(xp_ref))
    hgu = _dot(xa.astype(BF16), sgu_ref[:HALF, :]) + _dot(xb.astype(BF16), sgu_ref[HALF:, :])
    act = (_silu(hgu[:, :SHARED_FF]) * hgu[:, SHARED_FF:]).astype(BF16)
    shared = _dot(act, sdn_ref[...])

    def drain(s, carry):
        pltpu.make_async_copy(_row(ys_hbm, 0), _row(gbuf.at[slot, 0], 0), sem.at[slot]).wait()
        return carry

    lax.fori_loop(0, n, drain, 0, unroll=DMA_UNROLL)

    ra = rb = None
    for k in range(TOP_K):
        ya, yb = _unpack_pair(_load_rows(gbuf.at[slot, k]))
        w = wk_ref[:, k:k + 1]
        ra = w * ya if ra is None else ra + w * ya
        rb = w * yb if rb is None else rb + w * yb
    gate = mod_ref[5:6, :]
    x1 = x1_ref[...]
    ya = ALPHA * x1[:, :HALF] + gate[:, :HALF] * (ra + shared[:, :HALF])
    yb = ALPHA * x1[:, HALF:] + gate[:, HALF:] * (rb + shared[:, HALF:])
    mu = (jnp.sum(ya, -1, keepdims=True) + jnp.sum(yb, -1, keepdims=True)) / D_MODEL
    ya = ya - mu
    yb = yb - mu
    var = (jnp.sum(ya * ya, -1, keepdims=True) + jnp.sum(yb * yb, -1, keepdims=True)) / D_MODEL
    inv = lax.rsqrt(var + LN_EPS)
    o_ref[:, :HALF] = ya * inv * g_ref[:, :HALF] + b_ref[:, :HALF]
    o_ref[:, HALF:] = yb * inv * g_ref[:, HALF:] + b_ref[:, HALF:]


def _final(dest, ys, x1, xp, wk, mod, ln_g, ln_b, sgu_bf, sdn_bf, l, row_fn, tok_off):
    tp = x1.shape[0]
    tm = FIN_TM
    off = tok_off // tm
    nt = tp // tm
    row = pl.BlockSpec((tm, D_MODEL), lambda i: (i, 0))
    vec = pl.BlockSpec((None, 1, D_MODEL), lambda i: (l, 0, 0))
    return pl.pallas_call(
        _final_kernel,
        grid=(nt,),
        in_specs=[pl.BlockSpec((None, 1, TOP_K * tm), lambda i: (off + i, 0, 0), memory_space=pltpu.SMEM),
                  pl.BlockSpec((None, 1, TOP_K * tm), lambda i: (off + jnp.minimum(i + 1, nt - 1), 0, 0),
                               memory_space=pltpu.SMEM),
                  pl.BlockSpec(memory_space=pl.ANY),
                  row,
                  pl.BlockSpec((tm * ROW_TILES, 128), lambda i: (off + i, 0)),
                  pl.BlockSpec((tm, TOP_K), lambda i: (off + i, 0)),
                  pl.BlockSpec((None, None, 6, D_MODEL), lambda i: (l, row_fn(i * tm // TOK_TILE), 0, 0)),
                  vec, vec,
                  pl.BlockSpec((None, D_MODEL, 2 * SHARED_FF), lambda i: (l, 0, 0)),
                  pl.BlockSpec((None, SHARED_FF, D_MODEL), lambda i: (l, 0, 0))],
        out_specs=row,
        out_shape=jax.ShapeDtypeStruct((tp, D_MODEL), F32),
        scratch_shapes=[pltpu.VMEM((2, TOP_K, tm * ROW_TILES, 128), jnp.uint32), pltpu.SemaphoreType.DMA((2,))],
        compiler_params=_cp("arbitrary"),
        name="combine_ln2",
    )(dest, dest, ys, x1, xp, wk, mod, ln_g.reshape(DEPTH, 1, D_MODEL), ln_b.reshape(DEPTH, 1, D_MODEL),
      sgu_bf, sdn_bf)


def _dispatch_plan(idx_t, pos_t, cnt):
    t = idx_t.shape[1]
    nblk = t * TOP_K // MOE_BM + N_EXPERTS
    n_slots = nblk * MOE_BM
    e_ids = jnp.arange(N_EXPERTS, dtype=I32)
    counts = cnt[:, 0].astype(I32)
    padded = (counts + MOE_BM - 1) // MOE_BM * MOE_BM
    incl = e_ids[None, :] <= e_ids[:, None]
    pad_end = jnp.sum(jnp.where(incl, padded[None, :], 0), axis=1)
    pad_start = pad_end - padded
    start_of = jnp.sum(jnp.where(idx_t[:, :, None] == e_ids, pad_start, 0), axis=-1)
    dest = (start_of + pos_t) * ROW_TILES

    def tiles(tm):
        return dest.reshape(TOP_K, t // tm, tm).transpose(1, 0, 2).reshape(t // tm, 1, TOP_K * tm)

    blk_e = jnp.minimum(jnp.sum((pad_end[None, :] <= (jnp.arange(nblk, dtype=I32) * MOE_BM)[:, None]).astype(I32),
                                axis=1), N_EXPERTS - 1)
    n_used = (pad_end[-1] // MOE_BM).reshape(1)
    free = padded - counts
    free_end = jnp.sum(jnp.where(incl, free[None, :], 0), axis=1)
    free_start = free_end - free
    j = jnp.arange(n_slots - t * TOP_K, dtype=I32)
    owner = jnp.sum((free_end[None, :] <= j[:, None]).astype(I32), axis=1)
    own = owner[:, None] == e_ids
    in_pad = jnp.sum(jnp.where(own, (pad_start + counts - free_start)[None, :], 0), axis=1) + j
    in_tail = pad_end[-1] + j - free_end[-1]
    pad_dst = jnp.where(owner < N_EXPERTS, in_pad, in_tail) * ROW_TILES
    return tiles(TOK_TILE), tiles(FIN_TM), pad_dst, blk_e, n_used


def _rope_tables(n_tok):
    t = jnp.arange(n_tok)
    row = (t // GRID_W).astype(F32)
    col = (t % GRID_W).astype(F32)
    half = HEAD_DIM // 2
    inv = ROPE_BASE ** (-jnp.arange(0, half, 2, dtype=F32) / half)
    ar = row[:, None] * inv
    ac = col[:, None] * inv
    ang = jnp.concatenate([ar, ar, ac, ac], -1)
    cos, sin = jnp.cos(ang), jnp.sin(ang)
    quarter = (jnp.arange(HEAD_DIM) // (HEAD_DIM // 4)) % 2
    sin_a = jnp.where(quarter == 0, -sin, 0.0)
    sin_b = jnp.where(quarter == 1, sin, 0.0)
    return cos, sin_a, sin_b


def kernel(x_prompt, x_sample, cache_win_k, cache_win_v, state_ssm, state_ret, cache_na_k, cache_na_v, c, c_ctx, w_ada, b_ada, w_in, win_sink, conv_w, conv_b, dt_bias, a_log, d_skip, ssm_norm, ret_decay, na_rpb, w_branch, w_out, ln1_g, ln1_b, router_w, router_bias, exp_w_gu, exp_w_down, sh_w_gu, sh_w_down, ln2_g, ln2_b):
    bc, lc, _ = x_prompt.shape
    bl, ll, _ = x_sample.shape
    assert lc == TOK_TILE and ll % TOK_TILE == 0
    tc, tl = bc * lc, bl * ll
    lat_tiles = ll // TOK_TILE

    nr = -(-(1 + bl) // 8) * 8
    cond = jnp.zeros((nr, D_MODEL), F32).at[0].set(c_ctx).at[1:1 + bl].set(c)
    mod = _ada(cond, w_ada, b_ada).reshape(DEPTH, nr, 6, D_MODEL)
    row_ctx = lambda i: 0
    row_lat = lambda i: 1 + i // lat_tiles

    rope = _rope_tables(ll)
    perm = np.array([d * SSM_HEADS + g * 8 + hh for g in range(SSM_GROUPS) for d in range(2) for hh in range(8)])
    dsk_full = jnp.repeat(d_skip, SSM_HEAD_DIM, axis=1)
    w_in1 = w_in[:, :, :U1_COLS].astype(BF16)
    w_in2 = w_in[:, :, U2_OFF:GATE_OFF].astype(BF16)
    w_ing = w_in[:, :, GATE_OFF:].astype(BF16)
    w_dt = w_in[:, :, DT_OFF:U2_OFF][:, :, perm].astype(BF16)
    w_out_bf = w_out.astype(BF16)
    rwt_bf = jnp.swapaxes(router_w, 1, 2).astype(BF16)
    sgu_bf = sh_w_gu.astype(BF16)
    sdn_bf = sh_w_down.astype(BF16)

    xs = {"ctx": x_prompt.reshape(tc, D_MODEL), "lat": x_sample.reshape(tl, D_MODEL)}
    new_ctx = []
    for l in range(DEPTH):
        x1 = {}
        bias_tab = _na_bias_table(na_rpb[l], ll)
        for path in ("ctx", "lat"):
            latent = path == "lat"
            x = xs[path]
            seq = ll if latent else lc
            row_fn = row_lat if latent else row_ctx
            h = _modulate(x, mod, l, row_fn)
            u1 = _in_proj(h, w_in1, l, U1_COLS)
            u2 = _in_proj(h, w_in2, l, U2_COLS)
            gates = _in_proj(h, w_ing, l, GATE_COLS, BF16)
            dt, la, lat_t = _ssd_steps(h, w_dt[l], dt_bias[l].reshape(-1)[perm], a_log[l].reshape(-1)[perm])
            xc = _conv(u1, conv_w[l], conv_b[l], seq)
            ys, ssm_fin = _ssd(xc, u1, dt, la, lat_t, dsk_full[l:l + 1], ssm_norm[l:l + 1], seq, l,
                               state_ssm if latent else None)
            yr, ret_fin = _retention(u2, ret_decay, seq, l, rope if latent else None,
                                     state_ret if latent else None)
            if latent:
                ya = _win_attention(u1, cache_win_k, cache_win_v, win_sink, rope, seq, l)
                yn = _na_attention(u2, cache_na_k, cache_na_v, bias_tab, seq, l)
            else:
                ya, yn = _ctx_attention(u1, u2, win_sink, seq, l)
                kv = lambda t, nh: t.reshape(bc, lc, nh, HEAD_DIM)
                new_ctx.append((kv(u1[:, 1024:1280], WIN_KV_HEADS), kv(u1[:, 1280:1536], WIN_KV_HEADS),
                                ssm_fin, ret_fin,
                                kv(u2[:, 4096:5120], NA_HEADS), kv(u2[:, 5120:6144], NA_HEADS)))
            merged = _merge(ya, ys, yr, yn, gates, w_branch, l)
            x1[path] = _out_proj(merged, w_out_bf, x, mod, ln1_g, ln1_b, l, row_fn)

        xp, idx_t, w_t, pos_t, cnt = _router(x1["ctx"], x1["lat"], mod, rwt_bf[l], router_bias[l], l, lat_tiles)
        dest_disp, dest_fin, pad_dst, blk_e, n_used = _dispatch_plan(idx_t, pos_t, cnt)
        x_sorted = _dispatch(xp, dest_disp, pad_dst, blk_e.shape[0] * MOE_BM)
        y_sorted = _experts(x_sorted, blk_e, n_used, exp_w_gu, exp_w_down, l)
        wk = w_t.T
        for path, off, row_fn in (("ctx", 0, row_ctx), ("lat", tc, row_lat)):
            xs[path] = _final(dest_fin, y_sorted, x1[path], xp, wk, mod, ln2_g, ln2_b, sgu_bf, sdn_bf,
                              l, row_fn, off)

    stack = lambda i: jnp.stack([t[i] for t in new_ctx], axis=1)
    return (xs["ctx"].reshape(bc, lc, D_MODEL), xs["lat"].reshape(bl, ll, D_MODEL),
            stack(0), stack(1), stack(2), stack(3), stack(4), stack(5))
```

```python
import functools

import jax
import jax.numpy as jnp
import numpy as np
from jax import lax
from jax.experimental import pallas as pl
from jax.experimental.pallas import tpu as pltpu

F32 = jnp.float32
BF16 = jnp.bfloat16
I32 = jnp.int32

D_MODEL = 2048
DEPTH = 2
GRID_W = 64
HEAD_DIM = 128
ROPE_BASE = 10000.0
CHUNK = 128
WIN_HEADS = 8
WIN_KV_HEADS = 2
WIN_GROUP = WIN_HEADS // WIN_KV_HEADS
WINDOW = 128
SSM_D_INNER = D_MODEL // 2
SSM_HEAD_DIM = 64
SSM_HEADS = SSM_D_INNER // SSM_HEAD_DIM
SSM_GROUPS = 2
SSM_STATE = 128
SSM_CONV = 7
SSM_CONV_DIM = SSM_D_INNER + 2 * SSM_GROUPS * SSM_STATE
RET_HEADS = 4
RET_QK_DIM = 128
RET_V_DIM = 256
NA_HEADS = 8
NA_ROWS = 8
NA_COLS = 16
N_BRANCH = 4
BRANCH_W = D_MODEL // 2
N_EXPERTS = 64
TOP_K = 8
N_EXPERT_GROUPS = 8
TOPK_GROUPS = 4
EXPERT_FF = 512
SHARED_FF = 512
ROUTED_SCALE = 2.5
ALPHA = (2.0 * DEPTH) ** 0.25
LN_EPS = 1e-5
RMS_EPS = 1e-6

U1_COLS = 4096
DT_OFF = 4096
U2_OFF = DT_OFF + 2 * SSM_HEADS
U2_COLS = 6144
GATE_OFF = U2_OFF + U2_COLS
GATE_COLS = N_BRANCH * D_MODEL

TOK_TILE = 256
MOE_BM = 256
NA_WIN_ROWS = 10
NEG_BIG = -1e30

VMEM_LIMIT = 52 * 1024 * 1024


def _cp(*sem):
    return pltpu.CompilerParams(dimension_semantics=sem, vmem_limit_bytes=VMEM_LIMIT)


def _dot(a, b):
    return jnp.dot(a, b, preferred_element_type=F32)


def _dot_nt(a, b):
    return lax.dot_general(a, b, (((1,), (1,)), ((), ())), preferred_element_type=F32)


def _dot_exact(a, b):
    return jnp.dot(a, b, preferred_element_type=F32, precision=lax.Precision.HIGHEST)


def _silu(x):
    return x * jax.nn.sigmoid(x)


def _softplus(x):
    return jnp.maximum(x, 0.0) + jnp.log1p(jnp.exp(-jnp.abs(x)))


def _rope(x, cos, sin_a, sin_b):
    return x * cos + pltpu.roll(x, 96, 1) * sin_a + pltpu.roll(x, 32, 1) * sin_b


def _layer_norm(y, g, b):
    mu = jnp.mean(y, -1, keepdims=True)
    yc = y - mu
    var = jnp.mean(yc * yc, -1, keepdims=True)
    return yc * lax.rsqrt(var + LN_EPS) * g + b


def _ada_kernel(c_ref, w_ref, b_ref, o_ref):
    a = _silu(c_ref[...]).astype(BF16)
    o_ref[...] = _dot(a, w_ref[...].astype(BF16)) + b_ref[...]


def _ada(cond, w_ada, b_ada):
    nl, _, nout = w_ada.shape
    nr = cond.shape[0]
    tn = 1024
    return pl.pallas_call(
        _ada_kernel,
        grid=(nl, nout // tn),
        in_specs=[pl.BlockSpec((nr, D_MODEL), lambda l, j: (0, 0)),
                  pl.BlockSpec((None, D_MODEL, tn), lambda l, j: (l, 0, j)),
                  pl.BlockSpec((None, 1, tn), lambda l, j: (l, 0, j))],
        out_specs=pl.BlockSpec((None, nr, tn), lambda l, j: (l, 0, j)),
        out_shape=jax.ShapeDtypeStruct((nl, nr, nout), F32),
        compiler_params=_cp("arbitrary", "arbitrary"),
        name="ada",
    )(cond, w_ada, b_ada.reshape(nl, 1, nout))


def _mod_kernel(x_ref, m_ref, o_ref):
    o_ref[...] = (x_ref[...] * (1.0 + m_ref[1:2, :]) + m_ref[0:1, :]).astype(BF16)


def _modulate(x, mod, l, row_fn):
    tp = x.shape[0]
    return pl.pallas_call(
        _mod_kernel,
        grid=(tp // TOK_TILE,),
        in_specs=[pl.BlockSpec((TOK_TILE, D_MODEL), lambda i: (i, 0)),
                  pl.BlockSpec((None, None, 6, D_MODEL), lambda i: (l, row_fn(i), 0, 0))],
        out_specs=pl.BlockSpec((TOK_TILE, D_MODEL), lambda i: (i, 0)),
        out_shape=jax.ShapeDtypeStruct((tp, D_MODEL), BF16),
        compiler_params=_cp("arbitrary"),
        name="modulate",
    )(x, mod)


def _mm_kernel(x_ref, w_ref, o_ref):
    o_ref[...] = _dot(x_ref[...], w_ref[...]).astype(o_ref.dtype)


def _in_proj(h, w, l, ncols, out_dtype=F32):
    tp = h.shape[0]
    tm, tn = 1024, 2048
    assert w.dtype == BF16 and ncols % tn == 0 and tp % tm == 0
    return pl.pallas_call(
        _mm_kernel,
        grid=(ncols // tn, tp // tm),
        in_specs=[pl.BlockSpec((tm, D_MODEL), lambda j, i: (i, 0)),
                  pl.BlockSpec((None, D_MODEL, tn), lambda j, i: (l, 0, j))],
        out_specs=pl.BlockSpec((tm, tn), lambda j, i: (i, j)),
        out_shape=jax.ShapeDtypeStruct((tp, ncols), out_dtype),
        compiler_params=_cp("arbitrary", "arbitrary"),
        name="in_proj",
    )(h, w)


def _dt_kernel(h_ref, w_ref, wt_ref, b_ref, a_ref, bt_ref, at_ref, dt_ref, la_ref, lat_ref):
    h = h_ref[...]
    dt = _softplus(_dot(h, w_ref[...]) + b_ref[...])
    la = dt * -jnp.exp(a_ref[...])
    dtt = _softplus(_dot_nt(wt_ref[...], h) + bt_ref[...])
    lat = dtt * -jnp.exp(at_ref[...])
    for g in range(SSM_GROUPS):
        dt_ref[g] = dt[:, 16 * g:16 * (g + 1)]
        la_ref[g] = la[:, 16 * g:16 * (g + 1)]
        for q in range(h.shape[0] // CHUNK):
            lat_ref[g, q] = lat[16 * g:16 * (g + 1), q * CHUNK:(q + 1) * CHUNK]


def _ssd_steps(h, w_dt, dt_bias, a_log):
    tp = h.shape[0]
    tm = 512
    nq = tm // CHUNK
    return pl.pallas_call(
        _dt_kernel,
        grid=(tp // tm,),
        in_specs=[pl.BlockSpec((tm, D_MODEL), lambda i: (i, 0)),
                  pl.BlockSpec((D_MODEL, 32), lambda i: (0, 0)),
                  pl.BlockSpec((32, D_MODEL), lambda i: (0, 0)),
                  pl.BlockSpec((1, 32), lambda i: (0, 0)),
                  pl.BlockSpec((1, 32), lambda i: (0, 0)),
                  pl.BlockSpec((32, 1), lambda i: (0, 0)),
                  pl.BlockSpec((32, 1), lambda i: (0, 0))],
        out_specs=[pl.BlockSpec((SSM_GROUPS, tm, 16), lambda i: (0, i, 0)),
                   pl.BlockSpec((SSM_GROUPS, tm, 16), lambda i: (0, i, 0)),
                   pl.BlockSpec((SSM_GROUPS, nq, 16, CHUNK), lambda i: (0, i, 0, 0))],
        out_shape=[jax.ShapeDtypeStruct((SSM_GROUPS, tp, 16), F32),
                   jax.ShapeDtypeStruct((SSM_GROUPS, tp, 16), F32),
                   jax.ShapeDtypeStruct((SSM_GROUPS, tp // CHUNK, 16, CHUNK), F32)],
        compiler_params=_cp("arbitrary"),
        name="ssd_steps",
    )(h, w_dt, w_dt.T, dt_bias.reshape(1, 32), a_log.reshape(1, 32),
      dt_bias.reshape(32, 1), a_log.reshape(32, 1))


def _conv_kernel(x_ref, w_ref, b_ref, o_ref, pad_ref, *, seq):
    ct = x_ref.shape[1]
    pad = SSM_CONV // 2
    pad_ref[0:8, :] = jnp.zeros((8, ct), F32)
    pad_ref[8 + seq:16 + seq, :] = jnp.zeros((8, ct), F32)
    pad_ref[8:8 + seq, :] = x_ref[...]
    for r in range(seq // CHUNK):
        acc = jnp.broadcast_to(b_ref[...], (CHUNK, ct))
        for j in range(SSM_CONV):
            s = 8 - pad + j + r * CHUNK
            acc = acc + w_ref[j:j + 1, :] * pad_ref[s:s + CHUNK, :]
        o_ref[r * CHUNK:(r + 1) * CHUNK, :] = _silu(acc)


def _conv(u1, conv_w, conv_b, seq):
    tp = u1.shape[0]
    ct = 512
    c0 = 2560 // ct
    return pl.pallas_call(
        functools.partial(_conv_kernel, seq=seq),
        grid=(tp // seq, SSM_CONV_DIM // ct),
        in_specs=[pl.BlockSpec((seq, ct), lambda b, j: (b, c0 + j)),
                  pl.BlockSpec((SSM_CONV, ct), lambda b, j: (0, j)),
                  pl.BlockSpec((1, ct), lambda b, j: (0, j))],
        out_specs=pl.BlockSpec((seq, ct), lambda b, j: (b, j)),
        out_shape=jax.ShapeDtypeStruct((tp, SSM_CONV_DIM), F32),
        scratch_shapes=[pltpu.VMEM((seq + 16, ct), F32)],
        compiler_params=_cp("arbitrary", "arbitrary"),
        name="conv",
    )(u1, conv_w, conv_b.reshape(1, -1))


def _expand_heads(a, sel):
    hi = a.astype(BF16)
    lo = (a - hi.astype(F32)).astype(BF16)
    return _dot(hi, sel) + _dot(lo, sel)


def _ssd_kernel(*refs, seq, has_s0, want_fin):
    (xs_ref, bm_ref, cm_ref, z_ref, dt_ref, la_ref, lat_ref, dsk_ref, nrm_ref) = refs[:9]
    pos = 9
    s0_ref = None
    if has_s0:
        s0_ref = refs[pos]
        pos += 1
    o_ref = refs[pos]
    pos += 1
    fin_ref = None
    if want_fin:
        fin_ref = refs[pos]
        pos += 1
    st_ref = refs[pos]
    y_ref = refs[pos + 1]

    nc = seq // CHUNK
    hpg = SSM_HEADS // SSM_GROUPS
    ri = lax.broadcasted_iota(I32, (CHUNK, CHUNK), 0)
    ci = lax.broadcasted_iota(I32, (CHUNK, CHUNK), 1)
    lower = ri >= ci
    upper = ri <= ci
    lt = jnp.where(lower, 1.0, 0.0).astype(F32)
    ut = jnp.where(upper, 1.0, 0.0).astype(F32)
    half0 = lax.broadcasted_iota(I32, (CHUNK, 128), 1) < SSM_HEAD_DIM
    sel_row = lax.broadcasted_iota(I32, (2 * hpg, hpg * SSM_HEAD_DIM), 0)
    sel_head = lax.broadcasted_iota(I32, (2 * hpg, hpg * SSM_HEAD_DIM), 1) // SSM_HEAD_DIM
    sels = [jnp.where(sel_row == hpg * d + sel_head, 1.0, 0.0).astype(BF16) for d in range(2)]

    for d in range(2):
        if has_s0:
            st_ref[d] = jnp.concatenate([s0_ref[d, hh] for hh in range(hpg)], axis=1)
        else:
            st_ref[d] = jnp.zeros((SSM_STATE, hpg * SSM_HEAD_DIM), F32)

    def chunk(c, d):
        r0 = pl.multiple_of(c * CHUNK, CHUNK)
        la_c = la_ref[pl.ds(r0, CHUNK), :]
        dt_c = dt_ref[pl.ds(r0, CHUNK), :]
        lat_c = lat_ref[c]
        if d == 0:
            acs = _dot_exact(lt, la_c)
            acst = _dot_exact(lat_c, ut)
            mask = lower
            er = CHUNK - 1
        else:
            acs = _dot_exact(ut, la_c)
            acst = _dot_exact(lat_c, lt)
            mask = upper
            er = 0
        edge = acs[er:er + 1, :]
        sel = sels[d]
        xs_c = xs_ref[pl.ds(r0, CHUNK), :]
        b_c = bm_ref[pl.ds(r0, CHUNK), :]
        c_c = cm_ref[pl.ds(r0, CHUNK), :].astype(BF16)
        g = _dot_nt(c_c, b_c.astype(BF16))
        xdt = xs_c * _expand_heads(dt_c, sel)
        tiles = []
        for p in range(hpg // 2):
            xt = xdt[:, 128 * p:128 * (p + 1)]
            acc = None
            for q in range(2):
                hh = hpg * d + 2 * p + q
                dec = jnp.exp(jnp.where(mask, acs[:, hh:hh + 1] - acst[hh:hh + 1, :], -jnp.inf))
                v = jnp.where(half0 if q == 0 else jnp.logical_not(half0), xt, 0.0).astype(BF16)
                t = _dot((g * dec).astype(BF16), v)
                acc = t if acc is None else acc + t
            tiles.append(acc)
        y = jnp.concatenate(tiles, axis=1)
        st = st_ref[d]
        e_in = _expand_heads(jnp.exp(acs), sel)
        y = y + e_in * _dot(c_c, st.astype(BF16))
        y_ref[d, pl.ds(r0, CHUNK), :] = y
        wgt = jnp.exp(edge - acs)
        v = (xdt * _expand_heads(wgt, sel)).astype(BF16)
        st_ref[d] = st * e_in[er:er + 1, :] + _dot(b_c.T.astype(BF16), v)

    def step(c, carry):
        chunk(c, 0)
        chunk(nc - 1 - c, 1)
        return carry

    lax.fori_loop(0, nc, step, 0)

    for r in range(nc):
        sl = slice(r * CHUNK, (r + 1) * CHUNK)
        y = (y_ref[0, sl, :] + y_ref[1, sl, :] + xs_ref[sl, :] * dsk_ref[...]) * _silu(z_ref[sl, :])
        y = y * lax.rsqrt(jnp.mean(y * y, -1, keepdims=True) + RMS_EPS)
        o_ref[sl, :] = (y * nrm_ref[...]).astype(BF16)

    if want_fin:
        for d in range(2):
            st = st_ref[d]
            for hh in range(hpg):
                fin_ref[d, hh] = st[:, hh * SSM_HEAD_DIM:(hh + 1) * SSM_HEAD_DIM]


def _ssd(xc, u1, dt, la, lat, dsk, nrm, seq, l, s0):
    tp = xc.shape[0]
    nb = tp // seq
    nc = seq // CHUNK
    gw = SSM_D_INNER // SSM_GROUPS
    hpg = SSM_HEADS // SSM_GROUPS
    has_s0 = s0 is not None
    want_fin = not has_s0
    in_specs = [pl.BlockSpec((seq, gw), lambda b, g: (b, g)),
                pl.BlockSpec((seq, SSM_STATE), lambda b, g: (b, SSM_D_INNER // SSM_STATE + g)),
                pl.BlockSpec((seq, SSM_STATE), lambda b, g: (b, SSM_D_INNER // SSM_STATE + SSM_GROUPS + g)),
                pl.BlockSpec((seq, gw), lambda b, g: (b, 1536 // gw + g)),
                pl.BlockSpec((None, seq, 16), lambda b, g: (g, b, 0)),
                pl.BlockSpec((None, seq, 16), lambda b, g: (g, b, 0)),
                pl.BlockSpec((None, nc, 16, CHUNK), lambda b, g: (g, b, 0, 0)),
                pl.BlockSpec((1, gw), lambda b, g: (0, g)),
                pl.BlockSpec((1, gw), lambda b, g: (0, g))]
    args = [xc, xc, xc, u1, dt, la, lat, dsk, nrm]
    if has_s0:
        in_specs.append(pl.BlockSpec((None, None, 2, hpg, SSM_STATE, SSM_HEAD_DIM),
                                     lambda b, g: (b, l, 0, g, 0, 0)))
        args.append(s0)
    out_specs = [pl.BlockSpec((seq, gw), lambda b, g: (b, g))]
    out_shape = [jax.ShapeDtypeStruct((tp, SSM_D_INNER), BF16)]
    if want_fin:
        out_specs.append(pl.BlockSpec((None, 2, hpg, SSM_STATE, SSM_HEAD_DIM), lambda b, g: (b, 0, g, 0, 0)))
        out_shape.append(jax.ShapeDtypeStruct((nb, 2, SSM_HEADS, SSM_STATE, SSM_HEAD_DIM), F32))
    res = pl.pallas_call(
        functools.partial(_ssd_kernel, seq=seq, has_s0=has_s0, want_fin=want_fin),
        grid=(nb, SSM_GROUPS),
        in_specs=in_specs,
        out_specs=out_specs,
        out_shape=out_shape,
        scratch_shapes=[pltpu.VMEM((2, SSM_STATE, gw), F32), pltpu.VMEM((2, seq, gw), F32)],
        compiler_params=_cp("arbitrary", "arbitrary"),
        name="ssd_scan",
    )(*args)
    return (res[0], res[1]) if want_fin else (res[0], None)


def _ret_kernel(*refs, seq, l, latent):
    dec_ref, q_ref, k_ref, v_ref, g_ref = refs[:5]
    pos = 5
    if latent:
        cos_ref, sa_ref, sb_ref, s0_ref = refs[pos:pos + 4]
        pos += 4
    o_ref = refs[pos]
    pos += 1
    fin_ref = None
    if not latent:
        fin_ref = refs[pos]
        pos += 1
    qs_ref, ks_ref, st_ref, y_ref = refs[pos:pos + 4]

    nc = seq // CHUNK
    hd = pl.program_id(1)
    if latent:
        qs_ref[...] = _rope(q_ref[...], cos_ref[...], sa_ref[...], sb_ref[...]).astype(BF16)
        ks_ref[...] = _rope(k_ref[...] * (RET_QK_DIM ** -0.5), cos_ref[...], sa_ref[...], sb_ref[...])
        for d in range(2):
            st_ref[d] = s0_ref[d]
    else:
        qs_ref[...] = q_ref[...].astype(BF16)
        ks_ref[...] = k_ref[...] * (RET_QK_DIM ** -0.5)
        for d in range(2):
            st_ref[d] = jnp.zeros((RET_QK_DIM, RET_V_DIM), F32)

    ri = lax.broadcasted_iota(I32, (CHUNK, CHUNK), 0)
    ci = lax.broadcasted_iota(I32, (CHUNK, CHUNK), 1)
    dist = (ri - ci).astype(F32)
    rowi = lax.broadcasted_iota(I32, (CHUNK, 1), 0).astype(F32)

    consts = []
    for d in range(2):
        raw = jnp.full((1, 1), dec_ref[l * 2 * RET_HEADS + d * RET_HEADS + hd], F32)
        lg = -_softplus(-raw)
        if d == 0:
            dec = jnp.exp(jnp.where(ri >= ci, dist * lg, -jnp.inf))
            e_in = jnp.exp((rowi + 1.0) * lg)
            wgt = jnp.exp((CHUNK - 1.0 - rowi) * lg)
        else:
            dec = jnp.exp(jnp.where(ri <= ci, -dist * lg, -jnp.inf))
            e_in = jnp.exp((CHUNK - rowi) * lg)
            wgt = jnp.exp(rowi * lg)
        consts.append((dec, e_in, wgt, jnp.exp(CHUNK * lg)))

    def chunk(cc, d):
        dec, e_in, wgt, full = consts[d]
        r0 = pl.multiple_of(cc * CHUNK, CHUNK)
        q = qs_ref[pl.ds(r0, CHUNK), :]
        k = ks_ref[pl.ds(r0, CHUNK), :]
        v = v_ref[pl.ds(r0, CHUNK), :].astype(BF16)
        g = _dot_nt(q, k.astype(BF16))
        st = st_ref[d]
        y_ref[d, pl.ds(r0, CHUNK), :] = _dot((g * dec).astype(BF16), v) + e_in * _dot(q, st.astype(BF16))
        st_ref[d] = st * full + _dot((k * wgt).T.astype(BF16), v)

    def step(c, carry):
        chunk(c, 0)
        chunk(nc - 1 - c, 1)
        return carry

    lax.fori_loop(0, nc, step, 0)

    for r in range(nc):
        sl = slice(r * CHUNK, (r + 1) * CHUNK)
        y = y_ref[0, sl, :] + y_ref[1, sl, :]
        y = y * lax.rsqrt(jnp.mean(y * y, -1, keepdims=True) + RMS_EPS)
        o_ref[sl, :] = (y * _silu(g_ref[sl, :])).astype(BF16)

    if not latent:
        for d in range(2):
            fin_ref[d] = st_ref[d]


def _retention(u2, ret_decay, seq, l, rope, s0):
    tp = u2.shape[0]
    nb = tp // seq
    latent = s0 is not None
    in_specs = [pl.BlockSpec((seq, RET_QK_DIM), lambda b, h, *_: (b, h)),
                pl.BlockSpec((seq, RET_QK_DIM), lambda b, h, *_: (b, RET_HEADS + h)),
                pl.BlockSpec((seq, RET_V_DIM), lambda b, h, *_: (b, 1024 // RET_V_DIM + h)),
                pl.BlockSpec((seq, RET_V_DIM), lambda b, h, *_: (b, 2048 // RET_V_DIM + h))]
    args = [u2, u2, u2, u2]
    if latent:
        for t in rope:
            in_specs.append(pl.BlockSpec((seq, HEAD_DIM), lambda b, h, *_: (0, 0)))
            args.append(t)
        in_specs.append(pl.BlockSpec((None, None, 2, None, RET_QK_DIM, RET_V_DIM),
                                     lambda b, h, *_: (b, l, 0, h, 0, 0)))
        args.append(s0)
    out_specs = [pl.BlockSpec((seq, RET_V_DIM), lambda b, h, *_: (b, h))]
    out_shape = [jax.ShapeDtypeStruct((tp, RET_HEADS * RET_V_DIM), BF16)]
    if not latent:
        out_specs.append(pl.BlockSpec((None, 2, None, RET_QK_DIM, RET_V_DIM), lambda b, h, *_: (b, 0, h, 0, 0)))
        out_shape.append(jax.ShapeDtypeStruct((nb, 2, RET_HEADS, RET_QK_DIM, RET_V_DIM), F32))
    res = pl.pallas_call(
        functools.partial(_ret_kernel, seq=seq, l=l, latent=latent),
        grid_spec=pltpu.PrefetchScalarGridSpec(
            num_scalar_prefetch=1,
            grid=(nb, RET_HEADS),
            in_specs=in_specs,
            out_specs=out_specs,
            scratch_shapes=[pltpu.VMEM((seq, RET_QK_DIM), BF16),
                            pltpu.VMEM((seq, RET_QK_DIM), F32),
                            pltpu.VMEM((2, RET_QK_DIM, RET_V_DIM), F32),
                            pltpu.VMEM((2, seq, RET_V_DIM), F32)]),
        out_shape=out_shape,
        compiler_params=_cp("arbitrary", "arbitrary"),
        name="retention",
    )(ret_decay.reshape(-1), *args)
    return (res[0], None) if latent else (res[0], res[1])


def _softmax_pv(s, v, sink):
    m = jnp.max(s, -1, keepdims=True)
    if sink is not None:
        m = jnp.maximum(m, sink)
    e = jnp.exp(s - m)
    den = jnp.sum(e, -1, keepdims=True)
    if sink is not None:
        den = den + jnp.exp(sink - m)
    return _dot(e.astype(BF16), v) / den


def _ctx_attn_kernel(sink_ref, qa_ref, ka_ref, va_ref, qn_ref, kn_ref, vn_ref, ya_ref, yn_ref, *, l):
    scale = HEAD_DIM ** -0.5
    for h in range(WIN_HEADS):
        kv = h // WIN_GROUP
        hs = slice(h * HEAD_DIM, (h + 1) * HEAD_DIM)
        ks = slice(kv * HEAD_DIM, (kv + 1) * HEAD_DIM)
        s = _dot_nt(qa_ref[:, hs].astype(BF16), ka_ref[:, ks].astype(BF16)) * scale
        ya_ref[:, hs] = _softmax_pv(s, va_ref[:, ks].astype(BF16), sink_ref[l * WIN_HEADS + h]).astype(BF16)
    for h in range(NA_HEADS):
        hs = slice(h * HEAD_DIM, (h + 1) * HEAD_DIM)
        s = _dot_nt(qn_ref[:, hs].astype(BF16), kn_ref[:, hs].astype(BF16)) * scale
        yn_ref[:, hs] = _softmax_pv(s, vn_ref[:, hs].astype(BF16), None).astype(BF16)


def _ctx_attention(u1, u2, win_sink, seq, l):
    tp = u1.shape[0]
    nb = tp // seq
    kvw = WIN_KV_HEADS * HEAD_DIM
    hw = NA_HEADS * HEAD_DIM
    return pl.pallas_call(
        functools.partial(_ctx_attn_kernel, l=l),
        grid_spec=pltpu.PrefetchScalarGridSpec(
            num_scalar_prefetch=1,
            grid=(nb,),
            in_specs=[pl.BlockSpec((seq, hw), lambda b, *_: (b, 0)),
                      pl.BlockSpec((seq, kvw), lambda b, *_: (b, 1024 // kvw)),
                      pl.BlockSpec((seq, kvw), lambda b, *_: (b, 1280 // kvw)),
                      pl.BlockSpec((seq, hw), lambda b, *_: (b, 3072 // hw)),
                      pl.BlockSpec((seq, hw), lambda b, *_: (b, 4096 // hw)),
                      pl.BlockSpec((seq, hw), lambda b, *_: (b, 5120 // hw))],
            out_specs=[pl.BlockSpec((seq, hw), lambda b, *_: (b, 0)),
                       pl.BlockSpec((seq, hw), lambda b, *_: (b, 0))]),
        out_shape=[jax.ShapeDtypeStruct((tp, hw), BF16), jax.ShapeDtypeStruct((tp, hw), BF16)],
        compiler_params=_cp("arbitrary"),
        name="ctx_attention",
    )(win_sink.reshape(-1), u1, u1, u1, u2, u2, u2)


def _win_kernel(sink_ref, q_ref, k_ref, v_ref, kc_ref, vc_ref, cos_ref, sa_ref, sb_ref, o_ref,
                kr_ref, vb_ref, *, seq, l):
    scale = HEAD_DIM ** -0.5
    kvh = pl.program_id(1)
    nq = seq // CHUNK
    band = 3 * CHUNK
    rows = WIN_GROUP * CHUNK
    kr_ref[...] = _rope(k_ref[...], cos_ref[...], sa_ref[...], sb_ref[...]).astype(BF16)
    vb_ref[...] = v_ref[...].astype(BF16)
    kc = kc_ref[...].astype(BF16)
    vc = vc_ref[...].astype(BF16)
    rg = lax.broadcasted_iota(I32, (rows, 1), 0) // CHUNK
    sink = jnp.zeros((rows, 1), F32)
    for g in range(WIN_GROUP):
        sink = jnp.where(rg == g, sink_ref[l * WIN_HEADS + kvh * WIN_GROUP + g], sink)
    qoff = lax.broadcasted_iota(I32, (rows, band), 0) % CHUNK
    koff = lax.broadcasted_iota(I32, (rows, band), 1)

    def body(n, carry):
        r0 = pl.multiple_of(n * CHUNK, CHUNK)
        cos = cos_ref[pl.ds(r0, CHUNK), :]
        sa = sa_ref[pl.ds(r0, CHUNK), :]
        sb = sb_ref[pl.ds(r0, CHUNK), :]
        qs = jnp.concatenate(
            [_rope(q_ref[pl.ds(r0, CHUNK), g * HEAD_DIM:(g + 1) * HEAD_DIM], cos, sa, sb)
             for g in range(WIN_GROUP)], axis=0).astype(BF16)
        start = pl.multiple_of(jnp.clip((n - 1) * CHUNK, 0, seq - band), CHUNK)
        kb = kr_ref[pl.ds(start, band), :]
        vb = vb_ref[pl.ds(start, band), :]
        s_loc = _dot_nt(qs, kb) * scale
        ok = jnp.abs(r0 + qoff - (start + koff)) <= WINDOW
        s_loc = jnp.where(ok, s_loc, -jnp.inf)
        s_ctx = _dot_nt(qs, kc) * scale
        m = jnp.maximum(jnp.maximum(jnp.max(s_loc, -1, keepdims=True), jnp.max(s_ctx, -1, keepdims=True)), sink)
        e_loc = jnp.exp(s_loc - m)
        e_ctx = jnp.exp(s_ctx - m)
        den = jnp.sum(e_loc, -1, keepdims=True) + jnp.sum(e_ctx, -1, keepdims=True) + jnp.exp(sink - m)
        o = (_dot(e_ctx.astype(BF16), vc) + _dot(e_loc.astype(BF16), vb)) / den
        for g in range(WIN_GROUP):
            o_ref[pl.ds(r0, CHUNK), g * HEAD_DIM:(g + 1) * HEAD_DIM] = o[g * CHUNK:(g + 1) * CHUNK, :].astype(BF16)
        return carry

    lax.fori_loop(0, nq, body, 0)


def _win_attention(u1, cache_k, cache_v, win_sink, rope, seq, l):
    tp = u1.shape[0]
    nb = tp // seq
    past = cache_k.shape[2]
    gw = WIN_GROUP * HEAD_DIM
    ck = cache_k.reshape(nb, DEPTH, past, WIN_KV_HEADS * HEAD_DIM)
    cv = cache_v.reshape(nb, DEPTH, past, WIN_KV_HEADS * HEAD_DIM)
    rope_spec = pl.BlockSpec((seq, HEAD_DIM), lambda b, h, *_: (0, 0))
    return pl.pallas_call(
        functools.partial(_win_kernel, seq=seq, l=l),
        grid_spec=pltpu.PrefetchScalarGridSpec(
            num_scalar_prefetch=1,
            grid=(nb, WIN_KV_HEADS),
            in_specs=[pl.BlockSpec((seq, gw), lambda b, h, *_: (b, h)),
                      pl.BlockSpec((seq, HEAD_DIM), lambda b, h, *_: (b, 1024 // HEAD_DIM + h)),
                      pl.BlockSpec((seq, HEAD_DIM), lambda b, h, *_: (b, 1280 // HEAD_DIM + h)),
                      pl.BlockSpec((None, None, past, HEAD_DIM), lambda b, h, *_: (b, l, 0, h)),
                      pl.BlockSpec((None, None, past, HEAD_DIM), lambda b, h, *_: (b, l, 0, h)),
                      rope_spec, rope_spec, rope_spec],
            out_specs=pl.BlockSpec((seq, gw), lambda b, h, *_: (b, h)),
            scratch_shapes=[pltpu.VMEM((seq, HEAD_DIM), BF16), pltpu.VMEM((seq, HEAD_DIM), BF16)]),
        out_shape=jax.ShapeDtypeStruct((tp, WIN_HEADS * HEAD_DIM), BF16),
        compiler_params=_cp("arbitrary", "arbitrary"),
        name="win_attention",
    )(win_sink.reshape(-1), u1, u1, u1, ck, cv, *rope)


def _na_window_start(j, rows):
    kr = min(NA_ROWS, rows)
    rs = jnp.clip(2 * j - kr // 2, 0, rows - kr)
    return jnp.minimum(rs, rows - NA_WIN_ROWS)


def _na_kernel(q_ref, k_ref, v_ref, kc_ref, vc_ref, bias_ref, o_ref, kb_ref, vb_ref, *, seq):
    scale = HEAD_DIM ** -0.5
    rows = seq // GRID_W
    nq = seq // CHUNK
    win = NA_WIN_ROWS * GRID_W
    kb_ref[...] = k_ref[...].astype(BF16)
    vb_ref[...] = v_ref[...].astype(BF16)
    kc = kc_ref[...].astype(BF16)
    vc = vc_ref[...].astype(BF16)

    def body(j, carry):
        r0 = pl.multiple_of(j * CHUNK, CHUNK)
        q = q_ref[pl.ds(r0, CHUNK), :].astype(BF16)
        start = pl.multiple_of(_na_window_start(j, rows) * GRID_W, GRID_W)
        kw = kb_ref[pl.ds(start, win), :]
        vw = vb_ref[pl.ds(start, win), :]
        s_loc = _dot_nt(q, kw) * scale + bias_ref[j]
        s_ctx = _dot_nt(q, kc) * scale
        m = jnp.maximum(jnp.max(s_loc, -1, keepdims=True), jnp.max(s_ctx, -1, keepdims=True))
        e_loc = jnp.exp(s_loc - m)
        e_ctx = jnp.exp(s_ctx - m)
        den = jnp.sum(e_loc, -1, keepdims=True) + jnp.sum(e_ctx, -1, keepdims=True)
        o = (_dot(e_ctx.astype(BF16), vc) + _dot(e_loc.astype(BF16), vw)) / den
        o_ref[pl.ds(r0, CHUNK), :] = o.astype(BF16)
        return carry

    lax.fori_loop(0, nq, body, 0)


def _na_bias_table(rpb, seq):
    nh = rpb.shape[0]
    rows = seq // GRID_W
    kr = min(NA_ROWS, rows)
    nq = seq // CHUNK
    ndr, ndc = 2 * NA_ROWS - 1, 2 * NA_COLS - 1
    halves = CHUNK // GRID_W
    qc = np.arange(GRID_W)[:, None]
    kc = np.arange(GRID_W)[None, :]
    dc = (np.clip(kc - qc, 1 - NA_COLS, NA_COLS - 1) + (NA_COLS - 1)).reshape(-1)
    cs = np.clip(qc - NA_COLS // 2, 0, GRID_W - NA_COLS)
    col_ok = ((kc >= cs) & (kc < cs + NA_COLS)).reshape(-1)
    onehot = (dc[None, :] == np.arange(ndc)[:, None]).astype(np.float32)
    blocks = jnp.einsum('hrd,dq->hrq', rpb, onehot, precision=lax.Precision.HIGHEST)
    blocks = jnp.where(col_ok[None, None, :], blocks, NEG_BIG)
    blocks = jnp.concatenate([blocks, jnp.full((nh, 1, GRID_W * GRID_W), NEG_BIG, F32)], axis=1)
    blocks = blocks.reshape(nh, ndr + 1, GRID_W, GRID_W)
    j = np.arange(nq)[:, None, None]
    r = 2 * j + np.arange(halves)[None, :, None]
    ws = np.minimum(np.clip(2 * j - kr // 2, 0, rows - kr), rows - NA_WIN_ROWS)
    krow = ws + np.arange(NA_WIN_ROWS)[None, None, :]
    rs = np.clip(r - kr // 2, 0, rows - kr)
    blk = np.where((krow >= rs) & (krow < rs + kr), krow - r + (NA_ROWS - 1), ndr)
    t = jnp.take(blocks, jnp.asarray(blk.reshape(-1), I32), axis=1)
    t = t.reshape(nh, nq, halves, NA_WIN_ROWS, GRID_W, GRID_W).transpose(0, 1, 2, 4, 3, 5)
    return t.reshape(nh, nq, CHUNK, NA_WIN_ROWS * GRID_W)


def _na_attention(u2, cache_k, cache_v, bias, seq, l):
    tp = u2.shape[0]
    nb = tp // seq
    past = cache_k.shape[2]
    nq = seq // CHUNK
    win = NA_WIN_ROWS * GRID_W
    ck = cache_k.reshape(nb, DEPTH, past, NA_HEADS * HEAD_DIM)
    cv = cache_v.reshape(nb, DEPTH, past, NA_HEADS * HEAD_DIM)
    return pl.pallas_call(
        functools.partial(_na_kernel, seq=seq),
        grid=(nb, NA_HEADS),
        in_specs=[pl.BlockSpec((seq, HEAD_DIM), lambda b, h: (b, 3072 // HEAD_DIM + h)),
                  pl.BlockSpec((seq, HEAD_DIM), lambda b, h: (b, 4096 // HEAD_DIM + h)),
                  pl.BlockSpec((seq, HEAD_DIM), lambda b, h: (b, 5120 // HEAD_DIM + h)),
                  pl.BlockSpec((None, None, past, HEAD_DIM), lambda b, h: (b, l, 0, h)),
                  pl.BlockSpec((None, None, past, HEAD_DIM), lambda b, h: (b, l, 0, h)),
                  pl.BlockSpec((None, nq, CHUNK, win), lambda b, h: (h, 0, 0, 0))],
        out_specs=pl.BlockSpec((seq, HEAD_DIM), lambda b, h: (b, h)),
        out_shape=jax.ShapeDtypeStruct((tp, NA_HEADS * HEAD_DIM), BF16),
        scratch_shapes=[pltpu.VMEM((seq, HEAD_DIM), BF16), pltpu.VMEM((seq, HEAD_DIM), BF16)],
        compiler_params=_cp("arbitrary", "arbitrary"),
        name="na_attention",
    )(u2, u2, u2, ck, cv, bias)


def _merge_kernel(ya_ref, ys_ref, yr_ref, yn_ref, g0_ref, g1_ref, g2_ref, g3_ref, w_ref, o_ref, wbf):
    @pl.when(pl.program_id(1) == 0)
    def _():
        wbf[...] = w_ref[...].astype(BF16)

    acc = None
    for n, (b_ref, g_ref) in enumerate(((ya_ref, g0_ref), (ys_ref, g1_ref), (yr_ref, g2_ref), (yn_ref, g3_ref))):
        t = jax.nn.sigmoid(g_ref[...].astype(F32)) * _dot(b_ref[...], wbf[n])
        acc = t if acc is None else acc + t
    o_ref[...] = acc.astype(BF16)


def _merge(ya, ys, yr, yn, gates, w_branch, l):
    tp = ya.shape[0]
    tm, tn = 512, 512
    per = D_MODEL // tn
    br = pl.BlockSpec((tm, BRANCH_W), lambda j, i: (i, 0))
    gate = lambda n: pl.BlockSpec((tm, tn), lambda j, i: (i, n * per + j))
    return pl.pallas_call(
        _merge_kernel,
        grid=(per, tp // tm),
        in_specs=[br, br, br, br, gate(0), gate(1), gate(2), gate(3),
                  pl.BlockSpec((None, N_BRANCH, BRANCH_W, tn), lambda j, i: (l, 0, 0, j))],
        out_specs=pl.BlockSpec((tm, tn), lambda j, i: (i, j)),
        out_shape=jax.ShapeDtypeStruct((tp, D_MODEL), BF16),
        scratch_shapes=[pltpu.VMEM((N_BRANCH, BRANCH_W, tn), BF16)],
        compiler_params=_cp("arbitrary", "arbitrary"),
        name="merge",
    )(ya, ys, yr, yn, gates, gates, gates, gates, w_branch)


def _out_kernel(m_ref, w_ref, x_ref, mod_ref, g_ref, b_ref, o_ref):
    mix = _dot(m_ref[...], w_ref[...])
    y = ALPHA * x_ref[...] + mod_ref[2:3, :] * mix
    o_ref[...] = _layer_norm(y, g_ref[...], b_ref[...])


def _out_proj(merged, w_out_bf, x, mod, ln_g, ln_b, l, row_fn):
    tp = x.shape[0]
    tm = TOK_TILE
    row = pl.BlockSpec((tm, D_MODEL), lambda i: (i, 0))
    vec = pl.BlockSpec((None, 1, D_MODEL), lambda i: (l, 0, 0))
    return pl.pallas_call(
        _out_kernel,
        grid=(tp // tm,),
        in_specs=[row,
                  pl.BlockSpec((None, D_MODEL, D_MODEL), lambda i: (l, 0, 0)),
                  row,
                  pl.BlockSpec((None, None, 6, D_MODEL), lambda i: (l, row_fn(i), 0, 0)),
                  vec, vec],
        out_specs=row,
        out_shape=jax.ShapeDtypeStruct((tp, D_MODEL), F32),
        compiler_params=_cp("arbitrary"),
        name="out_proj_ln1",
    )(merged, w_out_bf, x, mod, ln_g.reshape(DEPTH, 1, D_MODEL), ln_b.reshape(DEPTH, 1, D_MODEL))


def _pack_pair(a, b):
    ab = lax.bitcast_convert_type(a.astype(BF16).astype(F32), jnp.uint32)
    bb = lax.bitcast_convert_type(b.astype(BF16).astype(F32), jnp.uint32)
    return (ab & jnp.uint32(0xFFFF0000)) | (bb >> 16)


def _unpack_pair(p):
    a = lax.bitcast_convert_type(p & jnp.uint32(0xFFFF0000), F32)
    b = lax.bitcast_convert_type(p << 16, F32)
    return a, b


HALF = D_MODEL // 2
ROW_TILES = HALF // 128


def _store_rows(ref, packed):
    n = packed.shape[0]
    for c in range(ROW_TILES):
        ref[pl.ds(c, n, stride=ROW_TILES), :] = packed[:, c * 128:(c + 1) * 128]


def _load_rows(ref):
    n = ref.shape[0] // ROW_TILES
    return jnp.concatenate([ref[pl.ds(c, n, stride=ROW_TILES), :] for c in range(ROW_TILES)], axis=1)


def _row(ref, start):
    return ref.at[pl.ds(pl.multiple_of(start, ROW_TILES), ROW_TILES), :]


def _router_kernel(xc_ref, xl_ref, mod_ref, rw_ref, rb_ref, xp_ref, idx_ref, wt_ref, pos_ref, cnt_ref, carry,
                   *, n_ctx_tiles):
    i = pl.program_id(0)

    @pl.when(i == 0)
    def _():
        carry[...] = jnp.zeros(carry.shape, F32)

    x = jnp.where(i < n_ctx_tiles, xc_ref[...], xl_ref[...])
    xm = x * (1.0 + mod_ref[4:5, :]) + mod_ref[3:4, :]
    _store_rows(xp_ref, _pack_pair(xm[:, :HALF], xm[:, HALF:]))
    tm = xm.shape[0]
    scores = jax.nn.sigmoid(_dot_nt(rw_ref[...], xm.astype(BF16)))
    biased = scores + rb_ref[...]
    per = N_EXPERTS // N_EXPERT_GROUPS
    sub = lax.broadcasted_iota(I32, (per, tm), 0)
    bg = [biased[per * g:per * (g + 1), :] for g in range(N_EXPERT_GROUPS)]
    sg = [scores[per * g:per * (g + 1), :] for g in range(N_EXPERT_GROUPS)]
    gs = []
    for g in range(N_EXPERT_GROUPS):
        m1 = jnp.max(bg[g], 0, keepdims=True)
        i1 = jnp.min(jnp.where(bg[g] == m1, sub, per), 0, keepdims=True)
        m2 = jnp.max(jnp.where(sub == i1, -jnp.inf, bg[g]), 0, keepdims=True)
        gs.append(m1 + m2)
    cand = []
    for g in range(N_EXPERT_GROUPS):
        rank = jnp.zeros((1, tm), I32)
        for g2 in range(N_EXPERT_GROUPS):
            if g2 == g:
                continue
            ahead = (gs[g2] > gs[g]) | ((gs[g2] == gs[g]) & (g2 < g))
            rank = rank + ahead.astype(I32)
        cand.append(jnp.where(rank < TOPK_GROUPS, bg[g], -jnp.inf))
    eidx = [sub + per * g for g in range(N_EXPERT_GROUPS)]
    idx_rows, w_rows = [], []
    for _ in range(TOP_K):
        m = None
        for g in range(N_EXPERT_GROUPS):
            t = jnp.max(cand[g], 0, keepdims=True)
            m = t if m is None else jnp.maximum(m, t)
        ik = None
        for g in range(N_EXPERT_GROUPS):
            t = jnp.min(jnp.where(cand[g] == m, eidx[g], N_EXPERTS), 0, keepdims=True)
            ik = t if ik is None else jnp.minimum(ik, t)
        wk = jnp.zeros((1, tm), F32)
        for g in range(N_EXPERT_GROUPS):
            hit = eidx[g] == ik
            wk = wk + jnp.sum(jnp.where(hit, sg[g], 0.0), 0, keepdims=True)
            cand[g] = jnp.where(hit, -jnp.inf, cand[g])
        idx_rows.append(ik)
        w_rows.append(wk)
    tot = w_rows[0]
    for wk in w_rows[1:]:
        tot = tot + wk
    idx_ref[...] = jnp.concatenate(idx_rows, axis=0)
    wt_ref[...] = jnp.concatenate([wk / tot * ROUTED_SCALE for wk in w_rows], axis=0)

    member = []
    for g in range(N_EXPERT_GROUPS):
        sel = eidx[g] == idx_rows[0]
        for k in range(1, TOP_K):
            sel = sel | (eidx[g] == idx_rows[k])
        member.append(jnp.where(sel, 1.0, 0.0))
    member = jnp.concatenate(member, axis=0)
    earlier = (lax.broadcasted_iota(I32, (tm, tm), 0) < lax.broadcasted_iota(I32, (tm, tm), 1))
    before = carry[...] + _dot(member.astype(BF16), jnp.where(earlier, 1.0, 0.0).astype(BF16))
    pos_rows = []
    for k in range(TOP_K):
        pk = jnp.zeros((1, tm), F32)
        for g in range(N_EXPERT_GROUPS):
            pk = pk + jnp.sum(jnp.where(eidx[g] == idx_rows[k], before[per * g:per * (g + 1), :], 0.0),
                              0, keepdims=True)
        pos_rows.append(pk)
    pos_ref[...] = jnp.concatenate(pos_rows, axis=0).astype(I32)
    carry[...] = carry[...] + jnp.sum(member, axis=1, keepdims=True)
    cnt_ref[...] = carry[...]


def _router(x1c, x1l, mod, rwt_bf, rbias, l, n_lat_tiles_per_seq):
    tc, tl = x1c.shape[0], x1l.shape[0]
    tm = TOK_TILE
    nct, nlt = tc // tm, tl // tm
    t = tc + tl
    row_fn = lambda i: jnp.where(i < nct, 0, 1 + (i - nct) // n_lat_tiles_per_seq)
    return pl.pallas_call(
        functools.partial(_router_kernel, n_ctx_tiles=nct),
        grid=(nct + nlt,),
        in_specs=[pl.BlockSpec((tm, D_MODEL), lambda i: (jnp.minimum(i, nct - 1), 0)),
                  pl.BlockSpec((tm, D_MODEL), lambda i: (jnp.maximum(i - nct, 0), 0)),
                  pl.BlockSpec((None, None, 6, D_MODEL), lambda i: (l, row_fn(i), 0, 0)),
                  pl.BlockSpec((N_EXPERTS, D_MODEL), lambda i: (0, 0)),
                  pl.BlockSpec((N_EXPERTS, 1), lambda i: (0, 0))],
        out_specs=[pl.BlockSpec((tm * ROW_TILES, 128), lambda i: (i, 0)),
                   pl.BlockSpec((TOP_K, tm), lambda i: (0, i)),
                   pl.BlockSpec((TOP_K, tm), lambda i: (0, i)),
                   pl.BlockSpec((TOP_K, tm), lambda i: (0, i)),
                   pl.BlockSpec((N_EXPERTS, 1), lambda i: (0, 0))],
        out_shape=[jax.ShapeDtypeStruct((t * ROW_TILES, 128), jnp.uint32),
                   jax.ShapeDtypeStruct((TOP_K, t), I32),
                   jax.ShapeDtypeStruct((TOP_K, t), F32),
                   jax.ShapeDtypeStruct((TOP_K, t), I32),
                   jax.ShapeDtypeStruct((N_EXPERTS, 1), F32)],
        scratch_shapes=[pltpu.VMEM((N_EXPERTS, 1), F32)],
        compiler_params=_cp("arbitrary"),
        name="router",
    )(x1c, x1l, mod, rwt_bf, rbias.reshape(N_EXPERTS, 1))


DMA_UNROLL = 8
N_PAD_SLOTS = N_EXPERTS * MOE_BM


def _dispatch_kernel(pad_ref, dst_ref, xp_ref, xs_hbm, zrow, sem, zsem):
    tm = xp_ref.shape[0] // ROW_TILES
    n = TOP_K * tm

    @pl.when(pl.program_id(0) == 0)
    def _():
        zrow[...] = jnp.zeros(zrow.shape, jnp.uint32)

        def z_issue(j, carry):
            for p in range(2):
                pltpu.make_async_copy(zrow, _row(xs_hbm, pad_ref[2 * j + p]), zsem).start(priority=p)
            return carry

        def z_drain(s, carry):
            pltpu.make_async_copy(zrow, _row(xs_hbm, 0), zsem).wait()
            return carry

        lax.fori_loop(0, N_PAD_SLOTS // 2, z_issue, 0, unroll=DMA_UNROLL // 2)
        lax.fori_loop(0, N_PAD_SLOTS, z_drain, 0, unroll=DMA_UNROLL)

    for k in range(TOP_K):
        def issue(j, carry, k=k):
            for p in range(2):
                r = 2 * j + p
                pltpu.make_async_copy(_row(xp_ref, r * ROW_TILES), _row(xs_hbm, dst_ref[0, k * tm + r]),
                                      sem).start(priority=p)
            return carry

        lax.fori_loop(0, tm // 2, issue, 0, unroll=DMA_UNROLL // 2)

    def drain(s, carry):
        pltpu.make_async_copy(_row(xp_ref, 0), _row(xs_hbm, 0), sem).wait()
        return carry

    lax.fori_loop(0, n, drain, 0, unroll=DMA_UNROLL)


def _dispatch(xp, dest_tiles, pad_dst, n_rows):
    t = xp.shape[0] // ROW_TILES
    tm = TOK_TILE
    return pl.pallas_call(
        _dispatch_kernel,
        grid_spec=pltpu.PrefetchScalarGridSpec(
            num_scalar_prefetch=1,
            grid=(t // tm,),
            in_specs=[pl.BlockSpec((None, 1, TOP_K * tm), lambda i, pad: (i, 0, 0), memory_space=pltpu.SMEM),
                      pl.BlockSpec((tm * ROW_TILES, 128), lambda i, pad: (i, 0))],
            out_specs=pl.BlockSpec(memory_space=pl.ANY),
            scratch_shapes=[pltpu.VMEM((ROW_TILES, 128), jnp.uint32),
                            pltpu.SemaphoreType.DMA(()), pltpu.SemaphoreType.DMA(())]),
        out_shape=jax.ShapeDtypeStruct((n_rows * ROW_TILES, 128), jnp.uint32),
        compiler_params=_cp("arbitrary"),
        name="dispatch",
    )(pad_dst, dest_tiles, xp)


def _experts_kernel(be_ref, nu_ref, x_ref, wgu_ref, wdn_ref, y_ref, wgu_bf, wdn_bf):
    i = pl.program_id(0)

    @pl.when(i < nu_ref[0])
    def _():
        prev = be_ref[jnp.maximum(i - 1, 0)]

        @pl.when((i == 0) | (be_ref[i] != prev))
        def _():
            wgu_bf[...] = wgu_ref[...].astype(BF16)
            wdn_bf[...] = wdn_ref[...].astype(BF16)

        xa, xb = _unpack_pair(_load_rows(x_ref))
        hgu = _dot(xa.astype(BF16), wgu_bf[:HALF, :]) + _dot(xb.astype(BF16), wgu_bf[HALF:, :])
        act = (_silu(hgu[:, :EXPERT_FF]) * hgu[:, EXPERT_FF:]).astype(BF16)
        y = _dot(act, wdn_bf[...])
        _store_rows(y_ref, _pack_pair(y[:, :HALF], y[:, HALF:]))

    @pl.when(i >= nu_ref[0])
    def _():
        y_ref[...] = jnp.zeros(y_ref.shape, jnp.uint32)


def _experts(xs, blk_e, n_used, w_gu, w_dn, l):
    nblk = blk_e.shape[0]
    last = lambda i, be, nu: jnp.minimum(i, nu[0] - 1)
    return pl.pallas_call(
        _experts_kernel,
        grid_spec=pltpu.PrefetchScalarGridSpec(
            num_scalar_prefetch=2,
            grid=(nblk,),
            in_specs=[pl.BlockSpec((MOE_BM * ROW_TILES, 128), lambda i, be, nu: (last(i, be, nu), 0)),
                      pl.BlockSpec((None, None, D_MODEL, 2 * EXPERT_FF),
                                   lambda i, be, nu: (l, be[last(i, be, nu)], 0, 0)),
                      pl.BlockSpec((None, None, EXPERT_FF, D_MODEL),
                                   lambda i, be, nu: (l, be[last(i, be, nu)], 0, 0))],
            out_specs=pl.BlockSpec((MOE_BM * ROW_TILES, 128), lambda i, be, nu: (i, 0)),
            scratch_shapes=[pltpu.VMEM((D_MODEL, 2 * EXPERT_FF), BF16),
                            pltpu.VMEM((EXPERT_FF, D_MODEL), BF16)]),
        out_shape=jax.ShapeDtypeStruct((nblk * MOE_BM * ROW_TILES, 128), jnp.uint32),
        compiler_params=_cp("arbitrary"),
        name="experts",
    )(blk_e, n_used, xs, w_gu, w_dn)


FIN_TM = 128


def _final_kernel(dcur_ref, dnxt_ref, ys_hbm, x1_ref, xp_ref, wk_ref, mod_ref, g_ref, b_ref, sgu_ref, sdn_ref,
                  o_ref, gbuf, sem):
    n = TOP_K * FIN_TM
    i = pl.program_id(0)
    slot = i % 2

    def fetch(d_ref, sl):
        for k in range(TOP_K):
            def issue(j, carry, k=k):
                for p in range(2):
                    r = 2 * j + p
                    pltpu.make_async_copy(_row(ys_hbm, d_ref[0, k * FIN_TM + r]),
                                          _row(gbuf.at[sl, k], r * ROW_TILES), sem.at[sl]).start(priority=p)
                return carry

            lax.fori_loop(0, FIN_TM // 2, issue, 0, unroll=DMA_UNROLL // 2)

    @pl.when(i == 0)
    def _():
        fetch(dcur_ref, 0)

    @pl.when(i + 1 < pl.num_programs(0))
    def _():
        fetch(dnxt_ref, 1 - slot)

    xa, xb = _unpack_pair(_load_rows(xp_ref))
    hgu = _dot(xa.astype(BF16), sgu_ref[:HALF, :]) + _dot(xb.astype(BF16), sgu_ref[HALF:, :])
    act = (_silu(hgu[:, :SHARED_FF]) * hgu[:, SHARED_FF:]).astype(BF16)
    shared = _dot(act, sdn_ref[...])

    def drain(s, carry):
        pltpu.make_async_copy(_row(ys_hbm, 0), _row(gbuf.at[slot, 0], 0), sem.at[slot]).wait()
        return carry

    lax.fori_loop(0, n, drain, 0, unroll=DMA_UNROLL)

    ra = rb = None
    for k in range(TOP_K):
        ya, yb = _unpack_pair(_load_rows(gbuf.at[slot, k]))
        w = wk_ref[:, k:k + 1]
        ra = w * ya if ra is None else ra + w * ya
        rb = w * yb if rb is None else rb + w * yb
    gate = mod_ref[5:6, :]
    x1 = x1_ref[...]
    ya = ALPHA * x1[:, :HALF] + gate[:, :HALF] * (ra + shared[:, :HALF])
    yb = ALPHA * x1[:, HALF:] + gate[:, HALF:] * (rb + shared[:, HALF:])
    mu = (jnp.sum(ya, -1, keepdims=True) + jnp.sum(yb, -1, keepdims=True)) / D_MODEL
    ya = ya - mu
    yb = yb - mu
    var = (jnp.sum(ya * ya, -1, keepdims=True) + jnp.sum(yb * yb, -1, keepdims=True)) / D_MODEL
    inv = lax.rsqrt(var + LN_EPS)
    o_ref[:, :HALF] = ya * inv * g_ref[:, :HALF] + b_ref[:, :HALF]
    o_ref[:, HALF:] = yb * inv * g_ref[:, HALF:] + b_ref[:, HALF:]


def _final(dest, ys, x1, xp, wk, mod, ln_g, ln_b, sgu_bf, sdn_bf, l, row_fn, tok_off):
    tp = x1.shape[0]
    tm = FIN_TM
    off = tok_off // tm
    nt = tp // tm
    row = pl.BlockSpec((tm, D_MODEL), lambda i: (i, 0))
    vec = pl.BlockSpec((None, 1, D_MODEL), lambda i: (l, 0, 0))
    return pl.pallas_call(
        _final_kernel,
        grid=(nt,),
        in_specs=[pl.BlockSpec((None, 1, TOP_K * tm), lambda i: (off + i, 0, 0), memory_space=pltpu.SMEM),
                  pl.BlockSpec((None, 1, TOP_K * tm), lambda i: (off + jnp.minimum(i + 1, nt - 1), 0, 0),
                               memory_space=pltpu.SMEM),
                  pl.BlockSpec(memory_space=pl.ANY),
                  row,
                  pl.BlockSpec((tm * ROW_TILES, 128), lambda i: (off + i, 0)),
                  pl.BlockSpec((tm, TOP_K), lambda i: (off + i, 0)),
                  pl.BlockSpec((None, None, 6, D_MODEL), lambda i: (l, row_fn(i * tm // TOK_TILE), 0, 0)),
                  vec, vec,
                  pl.BlockSpec((None, D_MODEL, 2 * SHARED_FF), lambda i: (l, 0, 0)),
                  pl.BlockSpec((None, SHARED_FF, D_MODEL), lambda i: (l, 0, 0))],
        out_specs=row,
        out_shape=jax.ShapeDtypeStruct((tp, D_MODEL), F32),
        scratch_shapes=[pltpu.VMEM((2, TOP_K, tm * ROW_TILES, 128), jnp.uint32), pltpu.SemaphoreType.DMA((2,))],
        compiler_params=_cp("arbitrary"),
        name="combine_ln2",
    )(dest, dest, ys, x1, xp, wk, mod, ln_g.reshape(DEPTH, 1, D_MODEL), ln_b.reshape(DEPTH, 1, D_MODEL),
      sgu_bf, sdn_bf)


def _dispatch_plan(idx_t, pos_t, cnt):
    t = idx_t.shape[1]
    nblk = t * TOP_K // MOE_BM + N_EXPERTS
    n_slots = nblk * MOE_BM
    e_ids = jnp.arange(N_EXPERTS, dtype=I32)
    counts = cnt[:, 0].astype(I32)
    padded = (counts + MOE_BM - 1) // MOE_BM * MOE_BM
    incl = e_ids[None, :] <= e_ids[:, None]
    pad_end = jnp.sum(jnp.where(incl, padded[None, :], 0), axis=1)
    pad_start = pad_end - padded
    start_of = jnp.sum(jnp.where(idx_t[:, :, None] == e_ids, pad_start, 0), axis=-1)
    dest = (start_of + pos_t) * ROW_TILES

    def tiles(tm):
        return dest.reshape(TOP_K, t // tm, tm).transpose(1, 0, 2).reshape(t // tm, 1, TOP_K * tm)

    blk_e = jnp.minimum(jnp.sum((pad_end[None, :] <= (jnp.arange(nblk, dtype=I32) * MOE_BM)[:, None]).astype(I32),
                                axis=1), N_EXPERTS - 1)
    n_used = (pad_end[-1] // MOE_BM).reshape(1)
    free = padded - counts
    free_end = jnp.sum(jnp.where(incl, free[None, :], 0), axis=1)
    free_start = free_end - free
    j = jnp.arange(n_slots - t * TOP_K, dtype=I32)
    owner = jnp.sum((free_end[None, :] <= j[:, None]).astype(I32), axis=1)
    own = owner[:, None] == e_ids
    in_pad = jnp.sum(jnp.where(own, (pad_start + counts - free_start)[None, :], 0), axis=1) + j
    in_tail = pad_end[-1] + j - free_end[-1]
    pad_dst = jnp.where(owner < N_EXPERTS, in_pad, in_tail) * ROW_TILES
    return tiles(TOK_TILE), tiles(FIN_TM), pad_dst, blk_e, n_used


def _rope_tables(n_tok):
    t = jnp.arange(n_tok)
    row = (t // GRID_W).astype(F32)
    col = (t % GRID_W).astype(F32)
    half = HEAD_DIM // 2
    inv = ROPE_BASE ** (-jnp.arange(0, half, 2, dtype=F32) / half)
    ar = row[:, None] * inv
    ac = col[:, None] * inv
    ang = jnp.concatenate([ar, ar, ac, ac], -1)
    cos, sin = jnp.cos(ang), jnp.sin(ang)
    quarter = (jnp.arange(HEAD_DIM) // (HEAD_DIM // 4)) % 2
    sin_a = jnp.where(quarter == 0, -sin, 0.0)
    sin_b = jnp.where(quarter == 1, sin, 0.0)
    return cos, sin_a, sin_b


def kernel(x_prompt, x_sample, cache_win_k, cache_win_v, state_ssm, state_ret, cache_na_k, cache_na_v, c, c_ctx, w_ada, b_ada, w_in, win_sink, conv_w, conv_b, dt_bias, a_log, d_skip, ssm_norm, ret_decay, na_rpb, w_branch, w_out, ln1_g, ln1_b, router_w, router_bias, exp_w_gu, exp_w_down, sh_w_gu, sh_w_down, ln2_g, ln2_b):
    bc, lc, _ = x_prompt.shape
    bl, ll, _ = x_sample.shape
    assert lc == TOK_TILE and ll % TOK_TILE == 0
    tc, tl = bc * lc, bl * ll
    lat_tiles = ll // TOK_TILE

    nr = -(-(1 + bl) // 8) * 8
    cond = jnp.zeros((nr, D_MODEL), F32).at[0].set(c_ctx).at[1:1 + bl].set(c)
    mod = _ada(cond, w_ada, b_ada).reshape(DEPTH, nr, 6, D_MODEL)
    row_ctx = lambda i: 0
    row_lat = lambda i: 1 + i // lat_tiles

    rope = _rope_tables(ll)
    perm = np.array([d * SSM_HEADS + g * 8 + hh for g in range(SSM_GROUPS) for d in range(2) for hh in range(8)])
    dsk_full = jnp.repeat(d_skip, SSM_HEAD_DIM, axis=1)
    w_in1 = w_in[:, :, :U1_COLS].astype(BF16)
    w_in2 = w_in[:, :, U2_OFF:GATE_OFF].astype(BF16)
    w_ing = w_in[:, :, GATE_OFF:].astype(BF16)
    w_dt = w_in[:, :, DT_OFF:U2_OFF][:, :, perm].astype(BF16)
    w_out_bf = w_out.astype(BF16)
    rwt_bf = jnp.swapaxes(router_w, 1, 2).astype(BF16)
    sgu_bf = sh_w_gu.astype(BF16)
    sdn_bf = sh_w_down.astype(BF16)

    xs = {"ctx": x_prompt.reshape(tc, D_MODEL), "lat": x_sample.reshape(tl, D_MODEL)}
    new_ctx = []
    for l in range(DEPTH):
        x1 = {}
        bias_tab = _na_bias_table(na_rpb[l], ll)
        for path in ("ctx", "lat"):
            latent = path == "lat"
            x = xs[path]
            seq = ll if latent else lc
            row_fn = row_lat if latent else row_ctx
            h = _modulate(x, mod, l, row_fn)
            u1 = _in_proj(h, w_in1, l, U1_COLS)
            u2 = _in_proj(h, w_in2, l, U2_COLS)
            gates = _in_proj(h, w_ing, l, GATE_COLS, BF16)
            dt, la, lat_t = _ssd_steps(h, w_dt[l], dt_bias[l].reshape(-1)[perm], a_log[l].reshape(-1)[perm])
            xc = _conv(u1, conv_w[l], conv_b[l], seq)
            ys, ssm_fin = _ssd(xc, u1, dt, la, lat_t, dsk_full[l:l + 1], ssm_norm[l:l + 1], seq, l,
                               state_ssm if latent else None)
            yr, ret_fin = _retention(u2, ret_decay, seq, l, rope if latent else None,
                                     state_ret if latent else None)
            if latent:
                ya = _win_attention(u1, cache_win_k, cache_win_v, win_sink, rope, seq, l)
                yn = _na_attention(u2, cache_na_k, cache_na_v, bias_tab, seq, l)
            else:
                ya, yn = _ctx_attention(u1, u2, win_sink, seq, l)
                kv = lambda t, nh: t.reshape(bc, lc, nh, HEAD_DIM)
                new_ctx.append((kv(u1[:, 1024:1280], WIN_KV_HEADS), kv(u1[:, 1280:1536], WIN_KV_HEADS),
                                ssm_fin, ret_fin,
                                kv(u2[:, 4096:5120], NA_HEADS), kv(u2[:, 5120:6144], NA_HEADS)))
            merged = _merge(ya, ys, yr, yn, gates, w_branch, l)
            x1[path] = _out_proj(merged, w_out_bf, x, mod, ln1_g, ln1_b, l, row_fn)

        xp, idx_t, w_t, pos_t, cnt = _router(x1["ctx"], x1["lat"], mod, rwt_bf[l], router_bias[l], l, lat_tiles)
        dest_disp, dest_fin, pad_dst, blk_e, n_used = _dispatch_plan(idx_t, pos_t, cnt)
        x_sorted = _dispatch(xp, dest_disp, pad_dst, blk_e.shape[0] * MOE_BM)
        y_sorted = _experts(x_sorted, blk_e, n_used, exp_w_gu, exp_w_down, l)
        wk = w_t.T
        for path, off, row_fn in (("ctx", 0, row_ctx), ("lat", tc, row_lat)):
            xs[path] = _final(dest_fin, y_sorted, x1[path], xp, wk, mod, ln2_g, ln2_b, sgu_bf, sdn_bf,
                              l, row_fn, off)

    stack = lambda i: jnp.stack([t[i] for t in new_ctx], axis=1)
    return (xs["ctx"].reshape(bc, lc, D_MODEL), xs["lat"].reshape(bl, ll, D_MODEL),
            stack(0), stack(1), stack(2), stack(3), stack(4), stack(5))
```

```python
import functools

import jax
import jax.numpy as jnp
import numpy as np
from jax import lax
from jax.experimental import pallas as pl
from jax.experimental.pallas import tpu as pltpu

F32 = jnp.float32
BF16 = jnp.bfloat16
I32 = jnp.int32

D_MODEL = 2048
DEPTH = 2
GRID_W = 64
HEAD_DIM = 128
ROPE_BASE = 10000.0
CHUNK = 128
WIN_HEADS = 8
WIN_KV_HEADS = 2
WIN_GROUP = WIN_HEADS // WIN_KV_HEADS
WINDOW = 128
SSM_D_INNER = D_MODEL // 2
SSM_HEAD_DIM = 64
SSM_HEADS = SSM_D_INNER // SSM_HEAD_DIM
SSM_GROUPS = 2
SSM_STATE = 128
SSM_CONV = 7
SSM_CONV_DIM = SSM_D_INNER + 2 * SSM_GROUPS * SSM_STATE
RET_HEADS = 4
RET_QK_DIM = 128
RET_V_DIM = 256
NA_HEADS = 8
NA_ROWS = 8
NA_COLS = 16
N_BRANCH = 4
BRANCH_W = D_MODEL // 2
N_EXPERTS = 64
TOP_K = 8
N_EXPERT_GROUPS = 8
TOPK_GROUPS = 4
EXPERT_FF = 512
SHARED_FF = 512
ROUTED_SCALE = 2.5
ALPHA = (2.0 * DEPTH) ** 0.25
LN_EPS = 1e-5
RMS_EPS = 1e-6

U1_COLS = 4096
DT_OFF = 4096
U2_OFF = DT_OFF + 2 * SSM_HEADS
U2_COLS = 6144
GATE_OFF = U2_OFF + U2_COLS
GATE_COLS = N_BRANCH * D_MODEL

TOK_TILE = 256
MOE_BM = 256
NA_WIN_ROWS = 10
NEG_BIG = -1e30

VMEM_LIMIT = 52 * 1024 * 1024


def _cp(*sem):
    return pltpu.CompilerParams(dimension_semantics=sem, vmem_limit_bytes=VMEM_LIMIT)


def _dot(a, b):
    return jnp.dot(a, b, preferred_element_type=F32)


def _dot_nt(a, b):
    return lax.dot_general(a, b, (((1,), (1,)), ((), ())), preferred_element_type=F32)


def _dot_exact(a, b):
    return jnp.dot(a, b, preferred_element_type=F32, precision=lax.Precision.HIGHEST)


def _silu(x):
    return x * jax.nn.sigmoid(x)


def _softplus(x):
    return jnp.maximum(x, 0.0) + jnp.log1p(jnp.exp(-jnp.abs(x)))


def _rope(x, cos, sin_a, sin_b):
    return x * cos + pltpu.roll(x, 96, 1) * sin_a + pltpu.roll(x, 32, 1) * sin_b


def _layer_norm(y, g, b):
    mu = jnp.mean(y, -1, keepdims=True)
    yc = y - mu
    var = jnp.mean(yc * yc, -1, keepdims=True)
    return yc * lax.rsqrt(var + LN_EPS) * g + b


def _ada_kernel(c_ref, w_ref, b_ref, o_ref):
    a = _silu(c_ref[...]).astype(BF16)
    o_ref[...] = _dot(a, w_ref[...].astype(BF16)) + b_ref[...]


def _ada(cond, w_ada, b_ada):
    nl, _, nout = w_ada.shape
    nr = cond.shape[0]
    tn = 1024
    return pl.pallas_call(
        _ada_kernel,
        grid=(nl, nout // tn),
        in_specs=[pl.BlockSpec((nr, D_MODEL), lambda l, j: (0, 0)),
                  pl.BlockSpec((None, D_MODEL, tn), lambda l, j: (l, 0, j)),
                  pl.BlockSpec((None, 1, tn), lambda l, j: (l, 0, j))],
        out_specs=pl.BlockSpec((None, nr, tn), lambda l, j: (l, 0, j)),
        out_shape=jax.ShapeDtypeStruct((nl, nr, nout), F32),
        compiler_params=_cp("arbitrary", "arbitrary"),
        name="ada",
    )(cond, w_ada, b_ada.reshape(nl, 1, nout))


def _mod_kernel(x_ref, m_ref, o_ref):
    o_ref[...] = (x_ref[...] * (1.0 + m_ref[1:2, :]) + m_ref[0:1, :]).astype(BF16)


def _modulate(x, mod, l, row_fn):
    tp = x.shape[0]
    return pl.pallas_call(
        _mod_kernel,
        grid=(tp // TOK_TILE,),
        in_specs=[pl.BlockSpec((TOK_TILE, D_MODEL), lambda i: (i, 0)),
                  pl.BlockSpec((None, None, 6, D_MODEL), lambda i: (l, row_fn(i), 0, 0))],
        out_specs=pl.BlockSpec((TOK_TILE, D_MODEL), lambda i: (i, 0)),
        out_shape=jax.ShapeDtypeStruct((tp, D_MODEL), BF16),
        compiler_params=_cp("arbitrary"),
        name="modulate",
    )(x, mod)


def _mm_kernel(x_ref, w_ref, o_ref):
    o_ref[...] = _dot(x_ref[...], w_ref[...]).astype(o_ref.dtype)


def _in_proj(h, w, l, ncols, out_dtype=F32):
    tp = h.shape[0]
    tm, tn = 1024, 2048
    assert w.dtype == BF16 and ncols % tn == 0 and tp % tm == 0
    return pl.pallas_call(
        _mm_kernel,
        grid=(ncols // tn, tp // tm),
        in_specs=[pl.BlockSpec((tm, D_MODEL), lambda j, i: (i, 0)),
                  pl.BlockSpec((None, D_MODEL, tn), lambda j, i: (l, 0, j))],
        out_specs=pl.BlockSpec((tm, tn), lambda j, i: (i, j)),
        out_shape=jax.ShapeDtypeStruct((tp, ncols), out_dtype),
        compiler_params=_cp("arbitrary", "arbitrary"),
        name="in_proj",
    )(h, w)


def _dt_kernel(h_ref, w_ref, wt_ref, b_ref, a_ref, bt_ref, at_ref, dt_ref, la_ref, lat_ref):
    h = h_ref[...]
    dt = _softplus(_dot(h, w_ref[...]) + b_ref[...])
    la = dt * -jnp.exp(a_ref[...])
    dtt = _softplus(_dot_nt(wt_ref[...], h) + bt_ref[...])
    lat = dtt * -jnp.exp(at_ref[...])
    for g in range(SSM_GROUPS):
        dt_ref[g] = dt[:, 16 * g:16 * (g + 1)]
        la_ref[g] = la[:, 16 * g:16 * (g + 1)]
        for q in range(h.shape[0] // CHUNK):
            lat_ref[g, q] = lat[16 * g:16 * (g + 1), q * CHUNK:(q + 1) * CHUNK]


def _ssd_steps(h, w_dt, dt_bias, a_log):
    tp = h.shape[0]
    tm = 512
    nq = tm // CHUNK
    return pl.pallas_call(
        _dt_kernel,
        grid=(tp // tm,),
        in_specs=[pl.BlockSpec((tm, D_MODEL), lambda i: (i, 0)),
                  pl.BlockSpec((D_MODEL, 32), lambda i: (0, 0)),
                  pl.BlockSpec((32, D_MODEL), lambda i: (0, 0)),
                  pl.BlockSpec((1, 32), lambda i: (0, 0)),
                  pl.BlockSpec((1, 32), lambda i: (0, 0)),
                  pl.BlockSpec((32, 1), lambda i: (0, 0)),
                  pl.BlockSpec((32, 1), lambda i: (0, 0))],
        out_specs=[pl.BlockSpec((SSM_GROUPS, tm, 16), lambda i: (0, i, 0)),
                   pl.BlockSpec((SSM_GROUPS, tm, 16), lambda i: (0, i, 0)),
                   pl.BlockSpec((SSM_GROUPS, nq, 16, CHUNK), lambda i: (0, i, 0, 0))],
        out_shape=[jax.ShapeDtypeStruct((SSM_GROUPS, tp, 16), F32),
                   jax.ShapeDtypeStruct((SSM_GROUPS, tp, 16), F32),
                   jax.ShapeDtypeStruct((SSM_GROUPS, tp // CHUNK, 16, CHUNK), F32)],
        compiler_params=_cp("arbitrary"),
        name="ssd_steps",
    )(h, w_dt, w_dt.T, dt_bias.reshape(1, 32), a_log.reshape(1, 32),
      dt_bias.reshape(32, 1), a_log.reshape(32, 1))


def _conv_kernel(x_ref, w_ref, b_ref, o_ref, pad_ref, *, seq):
    ct = x_ref.shape[1]
    pad = SSM_CONV // 2
    pad_ref[0:8, :] = jnp.zeros((8, ct), F32)
    pad_ref[8 + seq:16 + seq, :] = jnp.zeros((8, ct), F32)
    pad_ref[8:8 + seq, :] = x_ref[...]
    for r in range(seq // CHUNK):
        acc = jnp.broadcast_to(b_ref[...], (CHUNK, ct))
        for j in range(SSM_CONV):
            s = 8 - pad + j + r * CHUNK
            acc = acc + w_ref[j:j + 1, :] * pad_ref[s:s + CHUNK, :]
        o_ref[r * CHUNK:(r + 1) * CHUNK, :] = _silu(acc)


def _conv(u1, conv_w, conv_b, seq):
    tp = u1.shape[0]
    ct = 512
    c0 = 2560 // ct
    return pl.pallas_call(
        functools.partial(_conv_kernel, seq=seq),
        grid=(tp // seq, SSM_CONV_DIM // ct),
        in_specs=[pl.BlockSpec((seq, ct), lambda b, j: (b, c0 + j)),
                  pl.BlockSpec((SSM_CONV, ct), lambda b, j: (0, j)),
                  pl.BlockSpec((1, ct), lambda b, j: (0, j))],
        out_specs=pl.BlockSpec((seq, ct), lambda b, j: (b, j)),
        out_shape=jax.ShapeDtypeStruct((tp, SSM_CONV_DIM), F32),
        scratch_shapes=[pltpu.VMEM((seq + 16, ct), F32)],
        compiler_params=_cp("arbitrary", "arbitrary"),
        name="conv",
    )(u1, conv_w, conv_b.reshape(1, -1))


def _expand_heads(a, sel):
    hi = a.astype(BF16)
    lo = (a - hi.astype(F32)).astype(BF16)
    return _dot(hi, sel) + _dot(lo, sel)


def _ssd_kernel(*refs, seq, has_s0, want_fin):
    (xs_ref, bm_ref, cm_ref, z_ref, dt_ref, la_ref, lat_ref, dsk_ref, nrm_ref) = refs[:9]
    pos = 9
    s0_ref = None
    if has_s0:
        s0_ref = refs[pos]
        pos += 1
    o_ref = refs[pos]
    pos += 1
    fin_ref = None
    if want_fin:
        fin_ref = refs[pos]
        pos += 1
    st_ref = refs[pos]
    y_ref = refs[pos + 1]

    nc = seq // CHUNK
    hpg = SSM_HEADS // SSM_GROUPS
    ri = lax.broadcasted_iota(I32, (CHUNK, CHUNK), 0)
    ci = lax.broadcasted_iota(I32, (CHUNK, CHUNK), 1)
    lower = ri >= ci
    upper = ri <= ci
    lt = jnp.where(lower, 1.0, 0.0).astype(F32)
    ut = jnp.where(upper, 1.0, 0.0).astype(F32)
    half0 = lax.broadcasted_iota(I32, (CHUNK, 128), 1) < SSM_HEAD_DIM
    sel_row = lax.broadcasted_iota(I32, (2 * hpg, hpg * SSM_HEAD_DIM), 0)
    sel_head = lax.broadcasted_iota(I32, (2 * hpg, hpg * SSM_HEAD_DIM), 1) // SSM_HEAD_DIM
    sels = [jnp.where(sel_row == hpg * d + sel_head, 1.0, 0.0).astype(BF16) for d in range(2)]

    for d in range(2):
        if has_s0:
            st_ref[d] = jnp.concatenate([s0_ref[d, hh] for hh in range(hpg)], axis=1)
        else:
            st_ref[d] = jnp.zeros((SSM_STATE, hpg * SSM_HEAD_DIM), F32)

    def chunk(c, d):
        r0 = pl.multiple_of(c * CHUNK, CHUNK)
        la_c = la_ref[pl.ds(r0, CHUNK), :]
        dt_c = dt_ref[pl.ds(r0, CHUNK), :]
        lat_c = lat_ref[c]
        if d == 0:
            acs = _dot_exact(lt, la_c)
            acst = _dot_exact(lat_c, ut)
            mask = lower
            er = CHUNK - 1
        else:
            acs = _dot_exact(ut, la_c)
            acst = _dot_exact(lat_c, lt)
            mask = upper
            er = 0
        edge = acs[er:er + 1, :]
        sel = sels[d]
        xs_c = xs_ref[pl.ds(r0, CHUNK), :]
        b_c = bm_ref[pl.ds(r0, CHUNK), :]
        c_c = cm_ref[pl.ds(r0, CHUNK), :].astype(BF16)
        g = _dot_nt(c_c, b_c.astype(BF16))
        xdt = xs_c * _expand_heads(dt_c, sel)
        tiles = []
        for p in range(hpg // 2):
            xt = xdt[:, 128 * p:128 * (p + 1)]
            acc = None
            for q in range(2):
                hh = hpg * d + 2 * p + q
                dec = jnp.exp(jnp.where(mask, acs[:, hh:hh + 1] - acst[hh:hh + 1, :], -jnp.inf))
                v = jnp.where(half0 if q == 0 else jnp.logical_not(half0), xt, 0.0).astype(BF16)
                t = _dot((g * dec).astype(BF16), v)
                acc = t if acc is None else acc + t
            tiles.append(acc)
        y = jnp.concatenate(tiles, axis=1)
        st = st_ref[d]
        e_in = _expand_heads(jnp.exp(acs), sel)
        y = y + e_in * _dot(c_c, st.astype(BF16))
        y_ref[d, pl.ds(r0, CHUNK), :] = y
        wgt = jnp.exp(edge - acs)
        v = (xdt * _expand_heads(wgt, sel)).astype(BF16)
        st_ref[d] = st * e_in[er:er + 1, :] + _dot(b_c.T.astype(BF16), v)

    def step(c, carry):
        chunk(c, 0)
        chunk(nc - 1 - c, 1)
        return carry

    lax.fori_loop(0, nc, step, 0)

    for r in range(nc):
        sl = slice(r * CHUNK, (r + 1) * CHUNK)
        y = (y_ref[0, sl, :] + y_ref[1, sl, :] + xs_ref[sl, :] * dsk_ref[...]) * _silu(z_ref[sl, :])
        y = y * lax.rsqrt(jnp.mean(y * y, -1, keepdims=True) + RMS_EPS)
        o_ref[sl, :] = (y * nrm_ref[...]).astype(BF16)

    if want_fin:
        for d in range(2):
            st = st_ref[d]
            for hh in range(hpg):
                fin_ref[d, hh] = st[:, hh * SSM_HEAD_DIM:(hh + 1) * SSM_HEAD_DIM]


def _ssd(xc, u1, dt, la, lat, dsk, nrm, seq, l, s0):
    tp = xc.shape[0]
    nb = tp // seq
    nc = seq // CHUNK
    gw = SSM_D_INNER // SSM_GROUPS
    hpg = SSM_HEADS // SSM_GROUPS
    has_s0 = s0 is not None
    want_fin = not has_s0
    in_specs = [pl.BlockSpec((seq, gw), lambda b, g: (b, g)),
                pl.BlockSpec((seq, SSM_STATE), lambda b, g: (b, SSM_D_INNER // SSM_STATE + g)),
                pl.BlockSpec((seq, SSM_STATE), lambda b, g: (b, SSM_D_INNER // SSM_STATE + SSM_GROUPS + g)),
                pl.BlockSpec((seq, gw), lambda b, g: (b, 1536 // gw + g)),
                pl.BlockSpec((None, seq, 16), lambda b, g: (g, b, 0)),
                pl.BlockSpec((None, seq, 16), lambda b, g: (g, b, 0)),
                pl.BlockSpec((None, nc, 16, CHUNK), lambda b, g: (g, b, 0, 0)),
                pl.BlockSpec((1, gw), lambda b, g: (0, g)),
                pl.BlockSpec((1, gw), lambda b, g: (0, g))]
    args = [xc, xc, xc, u1, dt, la, lat, dsk, nrm]
    if has_s0:
        in_specs.append(pl.BlockSpec((None, None, 2, hpg, SSM_STATE, SSM_HEAD_DIM),
                                     lambda b, g: (b, l, 0, g, 0, 0)))
        args.append(s0)
    out_specs = [pl.BlockSpec((seq, gw), lambda b, g: (b, g))]
    out_shape = [jax.ShapeDtypeStruct((tp, SSM_D_INNER), BF16)]
    if want_fin:
        out_specs.append(pl.BlockSpec((None, 2, hpg, SSM_STATE, SSM_HEAD_DIM), lambda b, g: (b, 0, g, 0, 0)))
        out_shape.append(jax.ShapeDtypeStruct((nb, 2, SSM_HEADS, SSM_STATE, SSM_HEAD_DIM), F32))
    res = pl.pallas_call(
        functools.partial(_ssd_kernel, seq=seq, has_s0=has_s0, want_fin=want_fin),
        grid=(nb, SSM_GROUPS),
        in_specs=in_specs,
        out_specs=out_specs,
        out_shape=out_shape,
        scratch_shapes=[pltpu.VMEM((2, SSM_STATE, gw), F32), pltpu.VMEM((2, seq, gw), F32)],
        compiler_params=_cp("arbitrary", "arbitrary"),
        name="ssd_scan",
    )(*args)
    return (res[0], res[1]) if want_fin else (res[0], None)


def _ret_kernel(*refs, seq, l, latent):
    dec_ref, q_ref, k_ref, v_ref, g_ref = refs[:5]
    pos = 5
    if latent:
        cos_ref, sa_ref, sb_ref, s0_ref = refs[pos:pos + 4]
        pos += 4
    o_ref = refs[pos]
    pos += 1
    fin_ref = None
    if not latent:
        fin_ref = refs[pos]
        pos += 1
    qs_ref, ks_ref, st_ref, y_ref = refs[pos:pos + 4]

    nc = seq // CHUNK
    hd = pl.program_id(1)
    if latent:
        qs_ref[...] = _rope(q_ref[...], cos_ref[...], sa_ref[...], sb_ref[...]).astype(BF16)
        ks_ref[...] = _rope(k_ref[...] * (RET_QK_DIM ** -0.5), cos_ref[...], sa_ref[...], sb_ref[...])
        for d in range(2):
            st_ref[d] = s0_ref[d]
    else:
        qs_ref[...] = q_ref[...].astype(BF16)
        ks_ref[...] = k_ref[...] * (RET_QK_DIM ** -0.5)
        for d in range(2):
            st_ref[d] = jnp.zeros((RET_QK_DIM, RET_V_DIM), F32)

    ri = lax.broadcasted_iota(I32, (CHUNK, CHUNK), 0)
    ci = lax.broadcasted_iota(I32, (CHUNK, CHUNK), 1)
    dist = (ri - ci).astype(F32)
    rowi = lax.broadcasted_iota(I32, (CHUNK, 1), 0).astype(F32)

    consts = []
    for d in range(2):
        raw = jnp.full((1, 1), dec_ref[l * 2 * RET_HEADS + d * RET_HEADS + hd], F32)
        lg = -_softplus(-raw)
        if d == 0:
            dec = jnp.exp(jnp.where(ri >= ci, dist * lg, -jnp.inf))
            e_in = jnp.exp((rowi + 1.0) * lg)
            wgt = jnp.exp((CHUNK - 1.0 - rowi) * lg)
        else:
            dec = jnp.exp(jnp.where(ri <= ci, -dist * lg, -jnp.inf))
            e_in = jnp.exp((CHUNK - rowi) * lg)
            wgt = jnp.exp(rowi * lg)
        consts.append((dec, e_in, wgt, jnp.exp(CHUNK * lg)))

    def chunk(cc, d):
        dec, e_in, wgt, full = consts[d]
        r0 = pl.multiple_of(cc * CHUNK, CHUNK)
        q = qs_ref[pl.ds(r0, CHUNK), :]
        k = ks_ref[pl.ds(r0, CHUNK), :]
        v = v_ref[pl.ds(r0, CHUNK), :].astype(BF16)
        g = _dot_nt(q, k.astype(BF16))
        st = st_ref[d]
        y_ref[d, pl.ds(r0, CHUNK), :] = _dot((g * dec).astype(BF16), v) + e_in * _dot(q, st.astype(BF16))
        st_ref[d] = st * full + _dot((k * wgt).T.astype(BF16), v)

    def step(c, carry):
        chunk(c, 0)
        chunk(nc - 1 - c, 1)
        return carry

    lax.fori_loop(0, nc, step, 0)

    for r in range(nc):
        sl = slice(r * CHUNK, (r + 1) * CHUNK)
        y = y_ref[0, sl, :] + y_ref[1, sl, :]
        y = y * lax.rsqrt(jnp.mean(y * y, -1, keepdims=True) + RMS_EPS)
        o_ref[sl, :] = (y * _silu(g_ref[sl, :])).astype(BF16)

    if not latent:
        for d in range(2):
            fin_ref[d] = st_ref[d]


def _retention(u2, ret_decay, seq, l, rope, s0):
    tp = u2.shape[0]
    nb = tp // seq
    latent = s0 is not None
    in_specs = [pl.BlockSpec((seq, RET_QK_DIM), lambda b, h, *_: (b, h)),
                pl.BlockSpec((seq, RET_QK_DIM), lambda b, h, *_: (b, RET_HEADS + h)),
                pl.BlockSpec((seq, RET_V_DIM), lambda b, h, *_: (b, 1024 // RET_V_DIM + h)),
                pl.BlockSpec((seq, RET_V_DIM), lambda b, h, *_: (b, 2048 // RET_V_DIM + h))]
    args = [u2, u2, u2, u2]
    if latent:
        for t in rope:
            in_specs.append(pl.BlockSpec((seq, HEAD_DIM), lambda b, h, *_: (0, 0)))
            args.append(t)
        in_specs.append(pl.BlockSpec((None, None, 2, None, RET_QK_DIM, RET_V_DIM),
                                     lambda b, h, *_: (b, l, 0, h, 0, 0)))
        args.append(s0)
    out_specs = [pl.BlockSpec((seq, RET_V_DIM), lambda b, h, *_: (b, h))]
    out_shape = [jax.ShapeDtypeStruct((tp, RET_HEADS * RET_V_DIM), BF16)]
    if not latent:
        out_specs.append(pl.BlockSpec((None, 2, None, RET_QK_DIM, RET_V_DIM), lambda b, h, *_: (b, 0, h, 0, 0)))
        out_shape.append(jax.ShapeDtypeStruct((nb, 2, RET_HEADS, RET_QK_DIM, RET_V_DIM), F32))
    res = pl.pallas_call(
        functools.partial(_ret_kernel, seq=seq, l=l, latent=latent),
        grid_spec=pltpu.PrefetchScalarGridSpec(
            num_scalar_prefetch=1,
            grid=(nb, RET_HEADS),
            in_specs=in_specs,
            out_specs=out_specs,
            scratch_shapes=[pltpu.VMEM((seq, RET_QK_DIM), BF16),
                            pltpu.VMEM((seq, RET_QK_DIM), F32),
                            pltpu.VMEM((2, RET_QK_DIM, RET_V_DIM), F32),
                            pltpu.VMEM((2, seq, RET_V_DIM), F32)]),
        out_shape=out_shape,
        compiler_params=_cp("arbitrary", "arbitrary"),
        name="retention",
    )(ret_decay.reshape(-1), *args)
    return (res[0], None) if latent else (res[0], res[1])


def _softmax_pv(s, v, sink):
    m = jnp.max(s, -1, keepdims=True)
    if sink is not None:
        m = jnp.maximum(m, sink)
    e = jnp.exp(s - m)
    den = jnp.sum(e, -1, keepdims=True)
    if sink is not None:
        den = den + jnp.exp(sink - m)
    return _dot(e.astype(BF16), v) / den


def _ctx_attn_kernel(sink_ref, qa_ref, ka_ref, va_ref, qn_ref, kn_ref, vn_ref, ya_ref, yn_ref, *, l):
    scale = HEAD_DIM ** -0.5
    for h in range(WIN_HEADS):
        kv = h // WIN_GROUP
        hs = slice(h * HEAD_DIM, (h + 1) * HEAD_DIM)
        ks = slice(kv * HEAD_DIM, (kv + 1) * HEAD_DIM)
        s = _dot_nt(qa_ref[:, hs].astype(BF16), ka_ref[:, ks].astype(BF16)) * scale
        ya_ref[:, hs] = _softmax_pv(s, va_ref[:, ks].astype(BF16), sink_ref[l * WIN_HEADS + h]).astype(BF16)
    for h in range(NA_HEADS):
        hs = slice(h * HEAD_DIM, (h + 1) * HEAD_DIM)
        s = _dot_nt(qn_ref[:, hs].astype(BF16), kn_ref[:, hs].astype(BF16)) * scale
        yn_ref[:, hs] = _softmax_pv(s, vn_ref[:, hs].astype(BF16), None).astype(BF16)


def _ctx_attention(u1, u2, win_sink, seq, l):
    tp = u1.shape[0]
    nb = tp // seq
    kvw = WIN_KV_HEADS * HEAD_DIM
    hw = NA_HEADS * HEAD_DIM
    return pl.pallas_call(
        functools.partial(_ctx_attn_kernel, l=l),
        grid_spec=pltpu.PrefetchScalarGridSpec(
            num_scalar_prefetch=1,
            grid=(nb,),
            in_specs=[pl.BlockSpec((seq, hw), lambda b, *_: (b, 0)),
                      pl.BlockSpec((seq, kvw), lambda b, *_: (b, 1024 // kvw)),
                      pl.BlockSpec((seq, kvw), lambda b, *_: (b, 1280 // kvw)),
                      pl.BlockSpec((seq, hw), lambda b, *_: (b, 3072 // hw)),
                      pl.BlockSpec((seq, hw), lambda b, *_: (b, 4096 // hw)),
                      pl.BlockSpec((seq, hw), lambda b, *_: (b, 5120 // hw))],
            out_specs=[pl.BlockSpec((seq, hw), lambda b, *_: (b, 0)),
                       pl.BlockSpec((seq, hw), lambda b, *_: (b, 0))]),
        out_shape=[jax.ShapeDtypeStruct((tp, hw), BF16), jax.ShapeDtypeStruct((tp, hw), BF16)],
        compiler_params=_cp("arbitrary"),
        name="ctx_attention",
    )(win_sink.reshape(-1), u1, u1, u1, u2, u2, u2)


def _win_kernel(sink_ref, q_ref, k_ref, v_ref, kc_ref, vc_ref, cos_ref, sa_ref, sb_ref, o_ref,
                kr_ref, vb_ref, *, seq, l):
    scale = HEAD_DIM ** -0.5
    kvh = pl.program_id(1)
    nq = seq // CHUNK
    band = 3 * CHUNK
    rows = WIN_GROUP * CHUNK
    kr_ref[...] = _rope(k_ref[...], cos_ref[...], sa_ref[...], sb_ref[...]).astype(BF16)
    vb_ref[...] = v_ref[...].astype(BF16)
    kc = kc_ref[...].astype(BF16)
    vc = vc_ref[...].astype(BF16)
    rg = lax.broadcasted_iota(I32, (rows, 1), 0) // CHUNK
    sink = jnp.zeros((rows, 1), F32)
    for g in range(WIN_GROUP):
        sink = jnp.where(rg == g, sink_ref[l * WIN_HEADS + kvh * WIN_GROUP + g], sink)
    qoff = lax.broadcasted_iota(I32, (rows, band), 0) % CHUNK
    koff = lax.broadcasted_iota(I32, (rows, band), 1)

    def body(n, carry):
        r0 = pl.multiple_of(n * CHUNK, CHUNK)
        cos = cos_ref[pl.ds(r0, CHUNK), :]
        sa = sa_ref[pl.ds(r0, CHUNK), :]
        sb = sb_ref[pl.ds(r0, CHUNK), :]
        qs = jnp.concatenate(
            [_rope(q_ref[pl.ds(r0, CHUNK), g * HEAD_DIM:(g + 1) * HEAD_DIM], cos, sa, sb)
             for g in range(WIN_GROUP)], axis=0).astype(BF16)
        start = pl.multiple_of(jnp.clip((n - 1) * CHUNK, 0, seq - band), CHUNK)
        kb = kr_ref[pl.ds(start, band), :]
        vb = vb_ref[pl.ds(start, band), :]
        s_loc = _dot_nt(qs, kb) * scale
        ok = jnp.abs(r0 + qoff - (start + koff)) <= WINDOW
        s_loc = jnp.where(ok, s_loc, -jnp.inf)
        s_ctx = _dot_nt(qs, kc) * scale
        m = jnp.maximum(jnp.maximum(jnp.max(s_loc, -1, keepdims=True), jnp.max(s_ctx, -1, keepdims=True)), sink)
        e_loc = jnp.exp(s_loc - m)
        e_ctx = jnp.exp(s_ctx - m)
        den = jnp.sum(e_loc, -1, keepdims=True) + jnp.sum(e_ctx, -1, keepdims=True) + jnp.exp(sink - m)
        o = (_dot(e_ctx.astype(BF16), vc) + _dot(e_loc.astype(BF16), vb)) / den
        for g in range(WIN_GROUP):
            o_ref[pl.ds(r0, CHUNK), g * HEAD_DIM:(g + 1) * HEAD_DIM] = o[g * CHUNK:(g + 1) * CHUNK, :].astype(BF16)
        return carry

    lax.fori_loop(0, nq, body, 0)


def _win_attention(u1, cache_k, cache_v, win_sink, rope, seq, l):
    tp = u1.shape[0]
    nb = tp // seq
    past = cache_k.shape[2]
    gw = WIN_GROUP * HEAD_DIM
    ck = cache_k.reshape(nb, DEPTH, past, WIN_KV_HEADS * HEAD_DIM)
    cv = cache_v.reshape(nb, DEPTH, past, WIN_KV_HEADS * HEAD_DIM)
    rope_spec = pl.BlockSpec((seq, HEAD_DIM), lambda b, h, *_: (0, 0))
    return pl.pallas_call(
        functools.partial(_win_kernel, seq=seq, l=l),
        grid_spec=pltpu.PrefetchScalarGridSpec(
            num_scalar_prefetch=1,
            grid=(nb, WIN_KV_HEADS),
            in_specs=[pl.BlockSpec((seq, gw), lambda b, h, *_: (b, h)),
                      pl.BlockSpec((seq, HEAD_DIM), lambda b, h, *_: (b, 1024 // HEAD_DIM + h)),
                      pl.BlockSpec((seq, HEAD_DIM), lambda b, h, *_: (b, 1280 // HEAD_DIM + h)),
                      pl.BlockSpec((None, None, past, HEAD_DIM), lambda b, h, *_: (b, l, 0, h)),
                      pl.BlockSpec((None, None, past, HEAD_DIM), lambda b, h, *_: (b, l, 0, h)),
                      rope_spec, rope_spec, rope_spec],
            out_specs=pl.BlockSpec((seq, gw), lambda b, h, *_: (b, h)),
            scratch_shapes=[pltpu.VMEM((seq, HEAD_DIM), BF16), pltpu.VMEM((seq, HEAD_DIM), BF16)]),
        out_shape=jax.ShapeDtypeStruct((tp, WIN_HEADS * HEAD_DIM), BF16),
        compiler_params=_cp("arbitrary", "arbitrary"),
        name="win_attention",
    )(win_sink.reshape(-1), u1, u1, u1, ck, cv, *rope)


def _na_window_start(j, rows):
    kr = min(NA_ROWS, rows)
    rs = jnp.clip(2 * j - kr // 2, 0, rows - kr)
    return jnp.minimum(rs, rows - NA_WIN_ROWS)


def _na_kernel(q_ref, k_ref, v_ref, kc_ref, vc_ref, bias_ref, o_ref, kb_ref, vb_ref, *, seq):
    scale = HEAD_DIM ** -0.5
    rows = seq // GRID_W
    nq = seq // CHUNK
    win = NA_WIN_ROWS * GRID_W
    kb_ref[...] = k_ref[...].astype(BF16)
    vb_ref[...] = v_ref[...].astype(BF16)
    kc = kc_ref[...].astype(BF16)
    vc = vc_ref[...].astype(BF16)

    def body(j, carry):
        r0 = pl.multiple_of(j * CHUNK, CHUNK)
        q = q_ref[pl.ds(r0, CHUNK), :].astype(BF16)
        start = pl.multiple_of(_na_window_start(j, rows) * GRID_W, GRID_W)
        kw = kb_ref[pl.ds(start, win), :]
        vw = vb_ref[pl.ds(start, win), :]
        s_loc = _dot_nt(q, kw) * scale + bias_ref[j]
        s_ctx = _dot_nt(q, kc) * scale
        m = jnp.maximum(jnp.max(s_loc, -1, keepdims=True), jnp.max(s_ctx, -1, keepdims=True))
        e_loc = jnp.exp(s_loc - m)
        e_ctx = jnp.exp(s_ctx - m)
        den = jnp.sum(e_loc, -1, keepdims=True) + jnp.sum(e_ctx, -1, keepdims=True)
        o = (_dot(e_ctx.astype(BF16), vc) + _dot(e_loc.astype(BF16), vw)) / den
        o_ref[pl.ds(r0, CHUNK), :] = o.astype(BF16)
        return carry

    lax.fori_loop(0, nq, body, 0)


def _na_bias_table(rpb, seq):
    nh = rpb.shape[0]
    rows = seq // GRID_W
    kr = min(NA_ROWS, rows)
    nq = seq // CHUNK
    ndr, ndc = 2 * NA_ROWS - 1, 2 * NA_COLS - 1
    halves = CHUNK // GRID_W
    qc = np.arange(GRID_W)[:, None]
    kc = np.arange(GRID_W)[None, :]
    dc = (np.clip(kc - qc, 1 - NA_COLS, NA_COLS - 1) + (NA_COLS - 1)).reshape(-1)
    cs = np.clip(qc - NA_COLS // 2, 0, GRID_W - NA_COLS)
    col_ok = ((kc >= cs) & (kc < cs + NA_COLS)).reshape(-1)
    onehot = (dc[None, :] == np.arange(ndc)[:, None]).astype(np.float32)
    blocks = jnp.einsum('hrd,dq->hrq', rpb, onehot, precision=lax.Precision.HIGHEST)
    blocks = jnp.where(col_ok[None, None, :], blocks, NEG_BIG)
    blocks = jnp.concatenate([blocks, jnp.full((nh, 1, GRID_W * GRID_W), NEG_BIG, F32)], axis=1)
    blocks = blocks.reshape(nh, ndr + 1, GRID_W, GRID_W)
    j = np.arange(nq)[:, None, None]
    r = 2 * j + np.arange(halves)[None, :, None]
    ws = np.minimum(np.clip(2 * j - kr // 2, 0, rows - kr), rows - NA_WIN_ROWS)
    krow = ws + np.arange(NA_WIN_ROWS)[None, None, :]
    rs = np.clip(r - kr // 2, 0, rows - kr)
    blk = np.where((krow >= rs) & (krow < rs + kr), krow - r + (NA_ROWS - 1), ndr)
    t = jnp.take(blocks, jnp.asarray(blk.reshape(-1), I32), axis=1)
    t = t.reshape(nh, nq, halves, NA_WIN_ROWS, GRID_W, GRID_W).transpose(0, 1, 2, 4, 3, 5)
    return t.reshape(nh, nq, CHUNK, NA_WIN_ROWS * GRID_W)


def _na_attention(u2, cache_k, cache_v, bias, seq, l):
    tp = u2.shape[0]
    nb = tp // seq
    past = cache_k.shape[2]
    nq = seq // CHUNK
    win = NA_WIN_ROWS * GRID_W
    ck = cache_k.reshape(nb, DEPTH, past, NA_HEADS * HEAD_DIM)
    cv = cache_v.reshape(nb, DEPTH, past, NA_HEADS * HEAD_DIM)
    return pl.pallas_call(
        functools.partial(_na_kernel, seq=seq),
        grid=(nb, NA_HEADS),
        in_specs=[pl.BlockSpec((seq, HEAD_DIM), lambda b, h: (b, 3072 // HEAD_DIM + h)),
                  pl.BlockSpec((seq, HEAD_DIM), lambda b, h: (b, 4096 // HEAD_DIM + h)),
                  pl.BlockSpec((seq, HEAD_DIM), lambda b, h: (b, 5120 // HEAD_DIM + h)),
                  pl.BlockSpec((None, None, past, HEAD_DIM), lambda b, h: (b, l, 0, h)),
                  pl.BlockSpec((None, None, past, HEAD_DIM), lambda b, h: (b, l, 0, h)),
                  pl.BlockSpec((None, nq, CHUNK, win), lambda b, h: (h, 0, 0, 0))],
        out_specs=pl.BlockSpec((seq, HEAD_DIM), lambda b, h: (b, h)),
        out_shape=jax.ShapeDtypeStruct((tp, NA_HEADS * HEAD_DIM), BF16),
        scratch_shapes=[pltpu.VMEM((seq, HEAD_DIM), BF16), pltpu.VMEM((seq, HEAD_DIM), BF16)],
        compiler_params=_cp("arbitrary", "arbitrary"),
        name="na_attention",
    )(u2, u2, u2, ck, cv, bias)


def _merge_kernel(ya_ref, ys_ref, yr_ref, yn_ref, g0_ref, g1_ref, g2_ref, g3_ref, w_ref, o_ref):
    acc = None
    for n, (b_ref, g_ref) in enumerate(((ya_ref, g0_ref), (ys_ref, g1_ref), (yr_ref, g2_ref), (yn_ref, g3_ref))):
        t = jax.nn.sigmoid(g_ref[...].astype(F32)) * _dot(b_ref[...], w_ref[n])
        acc = t if acc is None else acc + t
    o_ref[...] = acc.astype(BF16)


def _merge(ya, ys, yr, yn, gates, w_branch, l):
    tp = ya.shape[0]
    tm, tn = 512, 1024
    assert w_branch.dtype == BF16
    per = D_MODEL // tn
    br = pl.BlockSpec((tm, BRANCH_W), lambda j, i: (i, 0))
    gate = lambda n: pl.BlockSpec((tm, tn), lambda j, i: (i, n * per + j))
    return pl.pallas_call(
        _merge_kernel,
        grid=(per, tp // tm),
        in_specs=[br, br, br, br, gate(0), gate(1), gate(2), gate(3),
                  pl.BlockSpec((None, N_BRANCH, BRANCH_W, tn), lambda j, i: (l, 0, 0, j))],
        out_specs=pl.BlockSpec((tm, tn), lambda j, i: (i, j)),
        out_shape=jax.ShapeDtypeStruct((tp, D_MODEL), BF16),
        compiler_params=_cp("arbitrary", "arbitrary"),
        name="merge",
    )(ya, ys, yr, yn, gates, gates, gates, gates, w_branch)


def _out_kernel(m_ref, w_ref, x_ref, mod_ref, g_ref, b_ref, o_ref):
    mix = _dot(m_ref[...], w_ref[...])
    y = ALPHA * x_ref[...] + mod_ref[2:3, :] * mix
    o_ref[...] = _layer_norm(y, g_ref[...], b_ref[...])


def _out_proj(merged, w_out_bf, x, mod, ln_g, ln_b, l, row_fn):
    tp = x.shape[0]
    tm = TOK_TILE
    row = pl.BlockSpec((tm, D_MODEL), lambda i: (i, 0))
    vec = pl.BlockSpec((None, 1, D_MODEL), lambda i: (l, 0, 0))
    return pl.pallas_call(
        _out_kernel,
        grid=(tp // tm,),
        in_specs=[row,
                  pl.BlockSpec((None, D_MODEL, D_MODEL), lambda i: (l, 0, 0)),
                  row,
                  pl.BlockSpec((None, None, 6, D_MODEL), lambda i: (l, row_fn(i), 0, 0)),
                  vec, vec],
        out_specs=row,
        out_shape=jax.ShapeDtypeStruct((tp, D_MODEL), F32),
        compiler_params=_cp("arbitrary"),
        name="out_proj_ln1",
    )(merged, w_out_bf, x, mod, ln_g.reshape(DEPTH, 1, D_MODEL), ln_b.reshape(DEPTH, 1, D_MODEL))


def _pack_pair(a, b):
    ab = lax.bitcast_convert_type(a.astype(BF16).astype(F32), jnp.uint32)
    bb = lax.bitcast_convert_type(b.astype(BF16).astype(F32), jnp.uint32)
    return (ab & jnp.uint32(0xFFFF0000)) | (bb >> 16)


def _unpack_pair(p):
    a = lax.bitcast_convert_type(p & jnp.uint32(0xFFFF0000), F32)
    b = lax.bitcast_convert_type(p << 16, F32)
    return a, b


HALF = D_MODEL // 2
ROW_TILES = HALF // 128


def _store_rows(ref, packed):
    n = packed.shape[0]
    for c in range(ROW_TILES):
        ref[pl.ds(c, n, stride=ROW_TILES), :] = packed[:, c * 128:(c + 1) * 128]


def _load_rows(ref):
    n = ref.shape[0] // ROW_TILES
    return jnp.concatenate([ref[pl.ds(c, n, stride=ROW_TILES), :] for c in range(ROW_TILES)], axis=1)


def _row(ref, start):
    return ref.at[pl.ds(pl.multiple_of(start, ROW_TILES), ROW_TILES), :]


def _router_kernel(xc_ref, xl_ref, mod_ref, rw_ref, rb_ref, xp_ref, idx_ref, wt_ref, pos_ref, cnt_ref, carry,
                   *, n_ctx_tiles):
    i = pl.program_id(0)

    @pl.when(i == 0)
    def _():
        carry[...] = jnp.zeros(carry.shape, F32)

    x = jnp.where(i < n_ctx_tiles, xc_ref[...], xl_ref[...])
    xm = x * (1.0 + mod_ref[4:5, :]) + mod_ref[3:4, :]
    _store_rows(xp_ref, _pack_pair(xm[:, :HALF], xm[:, HALF:]))
    tm = xm.shape[0]
    scores = jax.nn.sigmoid(_dot_nt(rw_ref[...], xm.astype(BF16)))
    biased = scores + rb_ref[...]
    per = N_EXPERTS // N_EXPERT_GROUPS
    sub = lax.broadcasted_iota(I32, (per, tm), 0)
    bg = [biased[per * g:per * (g + 1), :] for g in range(N_EXPERT_GROUPS)]
    sg = [scores[per * g:per * (g + 1), :] for g in range(N_EXPERT_GROUPS)]
    gs = []
    for g in range(N_EXPERT_GROUPS):
        m1 = jnp.max(bg[g], 0, keepdims=True)
        i1 = jnp.min(jnp.where(bg[g] == m1, sub, per), 0, keepdims=True)
        m2 = jnp.max(jnp.where(sub == i1, -jnp.inf, bg[g]), 0, keepdims=True)
        gs.append(m1 + m2)
    cand = []
    for g in range(N_EXPERT_GROUPS):
        rank = jnp.zeros((1, tm), I32)
        for g2 in range(N_EXPERT_GROUPS):
            if g2 == g:
                continue
            ahead = (gs[g2] > gs[g]) | ((gs[g2] == gs[g]) & (g2 < g))
            rank = rank + ahead.astype(I32)
        cand.append(jnp.where(rank < TOPK_GROUPS, bg[g], -jnp.inf))
    eidx = [sub + per * g for g in range(N_EXPERT_GROUPS)]
    idx_rows, w_rows = [], []
    for _ in range(TOP_K):
        m = None
        for g in range(N_EXPERT_GROUPS):
            t = jnp.max(cand[g], 0, keepdims=True)
            m = t if m is None else jnp.maximum(m, t)
        ik = None
        for g in range(N_EXPERT_GROUPS):
            t = jnp.min(jnp.where(cand[g] == m, eidx[g], N_EXPERTS), 0, keepdims=True)
            ik = t if ik is None else jnp.minimum(ik, t)
        wk = jnp.zeros((1, tm), F32)
        for g in range(N_EXPERT_GROUPS):
            hit = eidx[g] == ik
            wk = wk + jnp.sum(jnp.where(hit, sg[g], 0.0), 0, keepdims=True)
            cand[g] = jnp.where(hit, -jnp.inf, cand[g])
        idx_rows.append(ik)
        w_rows.append(wk)
    tot = w_rows[0]
    for wk in w_rows[1:]:
        tot = tot + wk
    idx_ref[...] = jnp.concatenate(idx_rows, axis=0)
    wt_ref[...] = jnp.concatenate([wk / tot * ROUTED_SCALE for wk in w_rows], axis=0)

    member = []
    for g in range(N_EXPERT_GROUPS):
        sel = eidx[g] == idx_rows[0]
        for k in range(1, TOP_K):
            sel = sel | (eidx[g] == idx_rows[k])
        member.append(jnp.where(sel, 1.0, 0.0))
    member = jnp.concatenate(member, axis=0)
    earlier = (lax.broadcasted_iota(I32, (tm, tm), 0) < lax.broadcasted_iota(I32, (tm, tm), 1))
    before = carry[...] + _dot(member.astype(BF16), jnp.where(earlier, 1.0, 0.0).astype(BF16))
    pos_rows = []
    for k in range(TOP_K):
        pk = jnp.zeros((1, tm), F32)
        for g in range(N_EXPERT_GROUPS):
            pk = pk + jnp.sum(jnp.where(eidx[g] == idx_rows[k], before[per * g:per * (g + 1), :], 0.0),
                              0, keepdims=True)
        pos_rows.append(pk)
    pos_ref[...] = jnp.concatenate(pos_rows, axis=0).astype(I32)
    carry[...] = carry[...] + jnp.sum(member, axis=1, keepdims=True)
    cnt_ref[...] = carry[...]


def _router(x1c, x1l, mod, rwt_bf, rbias, l, n_lat_tiles_per_seq):
    tc, tl = x1c.shape[0], x1l.shape[0]
    tm = TOK_TILE
    nct, nlt = tc // tm, tl // tm
    t = tc + tl
    row_fn = lambda i: jnp.where(i < nct, 0, 1 + (i - nct) // n_lat_tiles_per_seq)
    return pl.pallas_call(
        functools.partial(_router_kernel, n_ctx_tiles=nct),
        grid=(nct + nlt,),
        in_specs=[pl.BlockSpec((tm, D_MODEL), lambda i: (jnp.minimum(i, nct - 1), 0)),
                  pl.BlockSpec((tm, D_MODEL), lambda i: (jnp.maximum(i - nct, 0), 0)),
                  pl.BlockSpec((None, None, 6, D_MODEL), lambda i: (l, row_fn(i), 0, 0)),
                  pl.BlockSpec((N_EXPERTS, D_MODEL), lambda i: (0, 0)),
                  pl.BlockSpec((N_EXPERTS, 1), lambda i: (0, 0))],
        out_specs=[pl.BlockSpec((tm * ROW_TILES, 128), lambda i: (i, 0)),
                   pl.BlockSpec((TOP_K, tm), lambda i: (0, i)),
                   pl.BlockSpec((TOP_K, tm), lambda i: (0, i)),
                   pl.BlockSpec((TOP_K, tm), lambda i: (0, i)),
                   pl.BlockSpec((N_EXPERTS, 1), lambda i: (0, 0))],
        out_shape=[jax.ShapeDtypeStruct((t * ROW_TILES, 128), jnp.uint32),
                   jax.ShapeDtypeStruct((TOP_K, t), I32),
                   jax.ShapeDtypeStruct((TOP_K, t), F32),
                   jax.ShapeDtypeStruct((TOP_K, t), I32),
                   jax.ShapeDtypeStruct((N_EXPERTS, 1), F32)],
        scratch_shapes=[pltpu.VMEM((N_EXPERTS, 1), F32)],
        compiler_params=_cp("arbitrary"),
        name="router",
    )(x1c, x1l, mod, rwt_bf, rbias.reshape(N_EXPERTS, 1))


DMA_UNROLL = 8
N_PAD_SLOTS = N_EXPERTS * MOE_BM


def _dispatch_kernel(pad_ref, dst_ref, xp_ref, xs_hbm, zrow, sem, zsem):
    tm = xp_ref.shape[0] // ROW_TILES
    n = TOP_K * tm

    @pl.when(pl.program_id(0) == 0)
    def _():
        zrow[...] = jnp.zeros(zrow.shape, jnp.uint32)

        def z_issue(j, carry):
            for p in range(2):
                pltpu.make_async_copy(zrow, _row(xs_hbm, pad_ref[2 * j + p]), zsem).start(priority=p)
            return carry

        def z_drain(s, carry):
            pltpu.make_async_copy(zrow, _row(xs_hbm, 0), zsem).wait()
            return carry

        lax.fori_loop(0, N_PAD_SLOTS // 2, z_issue, 0, unroll=DMA_UNROLL // 2)
        lax.fori_loop(0, N_PAD_SLOTS, z_drain, 0, unroll=DMA_UNROLL)

    for k in range(TOP_K):
        def issue(j, carry, k=k):
            for p in range(2):
                r = 2 * j + p
                pltpu.make_async_copy(_row(xp_ref, r * ROW_TILES), _row(xs_hbm, dst_ref[0, k * tm + r]),
                                      sem).start(priority=p)
            return carry

        lax.fori_loop(0, tm // 2, issue, 0, unroll=DMA_UNROLL // 2)

    def drain(s, carry):
        pltpu.make_async_copy(_row(xp_ref, 0), _row(xs_hbm, 0), sem).wait()
        return carry

    lax.fori_loop(0, n, drain, 0, unroll=DMA_UNROLL)


def _dispatch(xp, dest_tiles, pad_dst, n_rows):
    t = xp.shape[0] // ROW_TILES
    tm = TOK_TILE
    return pl.pallas_call(
        _dispatch_kernel,
        grid_spec=pltpu.PrefetchScalarGridSpec(
            num_scalar_prefetch=1,
            grid=(t // tm,),
            in_specs=[pl.BlockSpec((None, 1, TOP_K * tm), lambda i, pad: (i, 0, 0), memory_space=pltpu.SMEM),
                      pl.BlockSpec((tm * ROW_TILES, 128), lambda i, pad: (i, 0))],
            out_specs=pl.BlockSpec(memory_space=pl.ANY),
            scratch_shapes=[pltpu.VMEM((ROW_TILES, 128), jnp.uint32),
                            pltpu.SemaphoreType.DMA(()), pltpu.SemaphoreType.DMA(())]),
        out_shape=jax.ShapeDtypeStruct((n_rows * ROW_TILES, 128), jnp.uint32),
        compiler_params=_cp("arbitrary"),
        name="dispatch",
    )(pad_dst, dest_tiles, xp)


def _experts_kernel(be_ref, nu_ref, x_ref, wgu_ref, wdn_ref, y_ref, wgu_bf, wdn_bf):
    i = pl.program_id(0)

    @pl.when(i < nu_ref[0])
    def _():
        prev = be_ref[jnp.maximum(i - 1, 0)]

        @pl.when((i == 0) | (be_ref[i] != prev))
        def _():
            wgu_bf[...] = wgu_ref[...].astype(BF16)
            wdn_bf[...] = wdn_ref[...].astype(BF16)

        xa, xb = _unpack_pair(_load_rows(x_ref))
        hgu = _dot(xa.astype(BF16), wgu_bf[:HALF, :]) + _dot(xb.astype(BF16), wgu_bf[HALF:, :])
        act = (_silu(hgu[:, :EXPERT_FF]) * hgu[:, EXPERT_FF:]).astype(BF16)
        y = _dot(act, wdn_bf[...])
        _store_rows(y_ref, _pack_pair(y[:, :HALF], y[:, HALF:]))

    @pl.when(i >= nu_ref[0])
    def _():
        y_ref[...] = jnp.zeros(y_ref.shape, jnp.uint32)


def _experts(xs, blk_e, n_used, w_gu, w_dn, l):
    nblk = blk_e.shape[0]
    last = lambda i, be, nu: jnp.minimum(i, nu[0] - 1)
    return pl.pallas_call(
        _experts_kernel,
        grid_spec=pltpu.PrefetchScalarGridSpec(
            num_scalar_prefetch=2,
            grid=(nblk,),
            in_specs=[pl.BlockSpec((MOE_BM * ROW_TILES, 128), lambda i, be, nu: (last(i, be, nu), 0)),
                      pl.BlockSpec((None, None, D_MODEL, 2 * EXPERT_FF),
                                   lambda i, be, nu: (l, be[last(i, be, nu)], 0, 0)),
                      pl.BlockSpec((None, None, EXPERT_FF, D_MODEL),
                                   lambda i, be, nu: (l, be[last(i, be, nu)], 0, 0))],
            out_specs=pl.BlockSpec((MOE_BM * ROW_TILES, 128), lambda i, be, nu: (i, 0)),
            scratch_shapes=[pltpu.VMEM((D_MODEL, 2 * EXPERT_FF), BF16),
                            pltpu.VMEM((EXPERT_FF, D_MODEL), BF16)]),
        out_shape=jax.ShapeDtypeStruct((nblk * MOE_BM * ROW_TILES, 128), jnp.uint32),
        compiler_params=_cp("arbitrary"),
        name="experts",
    )(blk_e, n_used, xs, w_gu, w_dn)


FIN_TM = 128


def _final_kernel(dcur_ref, dnxt_ref, ys_hbm, x1_ref, xp_ref, wk_ref, mod_ref, g_ref, b_ref, sgu_ref, sdn_ref,
                  o_ref, gbuf, sem):
    n = TOP_K * FIN_TM
    i = pl.program_id(0)
    slot = i % 2

    def fetch(d_ref, sl):
        for k in range(TOP_K):
            def issue(j, carry, k=k):
                for p in range(2):
                    r = 2 * j + p
                    pltpu.make_async_copy(_row(ys_hbm, d_ref[0, k * FIN_TM + r]),
                                          _row(gbuf.at[sl, k], r * ROW_TILES), sem.at[sl]).start(priority=p)
                return carry

            lax.fori_loop(0, FIN_TM // 2, issue, 0, unroll=DMA_UNROLL // 2)

    @pl.when(i == 0)
    def _():
        fetch(dcur_ref, 0)

    @pl.when(i + 1 < pl.num_programs(0))
    def _():
        fetch(dnxt_ref, 1 - slot)

    xa, xb = _unpack_pair(_load_rows(xp_ref))
    hgu = _dot(xa.astype(BF16), sgu_ref[:HALF, :]) + _dot(xb.astype(BF16), sgu_ref[HALF:, :])
    act = (_silu(hgu[:, :SHARED_FF]) * hgu[:, SHARED_FF:]).astype(BF16)
    shared = _dot(act, sdn_ref[...])

    def drain(s, carry):
        pltpu.make_async_copy(_row(ys_hbm, 0), _row(gbuf.at[slot, 0], 0), sem.at[slot]).wait()
        return carry

    lax.fori_loop(0, n, drain, 0, unroll=DMA_UNROLL)

    ra = rb = None
    for k in range(TOP_K):
        ya, yb = _unpack_pair(_load_rows(gbuf.at[slot, k]))
        w = wk_ref[:, k:k + 1]
        ra = w * ya if ra is None else ra + w * ya
        rb = w * yb if rb is None else rb + w * yb
    gate = mod_ref[5:6, :]
    x1 = x1_ref[...]
    ya = ALPHA * x1[:, :HALF] + gate[:, :HALF] * (ra + shared[:, :HALF])
    yb = ALPHA * x1[:, HALF:] + gate[:, HALF:] * (rb + shared[:, HALF:])
    mu = (jnp.sum(ya, -1, keepdims=True) + jnp.sum(yb, -1, keepdims=True)) / D_MODEL
    ya = ya - mu
    yb = yb - mu
    var = (jnp.sum(ya * ya, -1, keepdims=True) + jnp.sum(yb * yb, -1, keepdims=True)) / D_MODEL
    inv = lax.rsqrt(var + LN_EPS)
    o_ref[:, :HALF] = ya * inv * g_ref[:, :HALF] + b_ref[:, :HALF]
    o_ref[:, HALF:] = yb * inv * g_ref[:, HALF:] + b_ref[:, HALF:]


def _final(dest, ys, x1, xp, wk, mod, ln_g, ln_b, sgu_bf, sdn_bf, l, row_fn, tok_off):
    tp = x1.shape[0]
    tm = FIN_TM
    off = tok_off // tm
    nt = tp // tm
    row = pl.BlockSpec((tm, D_MODEL), lambda i: (i, 0))
    vec = pl.BlockSpec((None, 1, D_MODEL), lambda i: (l, 0, 0))
    return pl.pallas_call(
        _final_kernel,
        grid=(nt,),
        in_specs=[pl.BlockSpec((None, 1, TOP_K * tm), lambda i: (off + i, 0, 0), memory_space=pltpu.SMEM),
                  pl.BlockSpec((None, 1, TOP_K * tm), lambda i: (off + jnp.minimum(i + 1, nt - 1), 0, 0),
                               memory_space=pltpu.SMEM),
                  pl.BlockSpec(memory_space=pl.ANY),
                  row,
                  pl.BlockSpec((tm * ROW_TILES, 128), lambda i: (off + i, 0)),
                  pl.BlockSpec((tm, TOP_K), lambda i: (off + i, 0)),
                  pl.BlockSpec((None, None, 6, D_MODEL), lambda i: (l, row_fn(i * tm // TOK_TILE), 0, 0)),
                  vec, vec,
                  pl.BlockSpec((None, D_MODEL, 2 * SHARED_FF), lambda i: (l, 0, 0)),
                  pl.BlockSpec((None, SHARED_FF, D_MODEL), lambda i: (l, 0, 0))],
        out_specs=row,
        out_shape=jax.ShapeDtypeStruct((tp, D_MODEL), F32),
        scratch_shapes=[pltpu.VMEM((2, TOP_K, tm * ROW_TILES, 128), jnp.uint32), pltpu.SemaphoreType.DMA((2,))],
        compiler_params=_cp("arbitrary"),
        name="combine_ln2",
    )(dest, dest, ys, x1, xp, wk, mod, ln_g.reshape(DEPTH, 1, D_MODEL), ln_b.reshape(DEPTH, 1, D_MODEL),
      sgu_bf, sdn_bf)


def _dispatch_plan(idx_t, pos_t, cnt):
    t = idx_t.shape[1]
    nblk = t * TOP_K // MOE_BM + N_EXPERTS
    n_slots = nblk * MOE_BM
    e_ids = jnp.arange(N_EXPERTS, dtype=I32)
    counts = cnt[:, 0].astype(I32)
    padded = (counts + MOE_BM - 1) // MOE_BM * MOE_BM
    incl = e_ids[None, :] <= e_ids[:, None]
    pad_end = jnp.sum(jnp.where(incl, padded[None, :], 0), axis=1)
    pad_start = pad_end - padded
    start_of = jnp.sum(jnp.where(idx_t[:, :, None] == e_ids, pad_start, 0), axis=-1)
    dest = (start_of + pos_t) * ROW_TILES

    def tiles(tm):
        return dest.reshape(TOP_K, t // tm, tm).transpose(1, 0, 2).reshape(t // tm, 1, TOP_K * tm)

    blk_e = jnp.minimum(jnp.sum((pad_end[None, :] <= (jnp.arange(nblk, dtype=I32) * MOE_BM)[:, None]).astype(I32),
                                axis=1), N_EXPERTS - 1)
    n_used = (pad_end[-1] // MOE_BM).reshape(1)
    free = padded - counts
    free_end = jnp.sum(jnp.where(incl, free[None, :], 0), axis=1)
    free_start = free_end - free
    j = jnp.arange(n_slots - t * TOP_K, dtype=I32)
    owner = jnp.sum((free_end[None, :] <= j[:, None]).astype(I32), axis=1)
    own = owner[:, None] == e_ids
    in_pad = jnp.sum(jnp.where(own, (pad_start + counts - free_start)[None, :], 0), axis=1) + j
    in_tail = pad_end[-1] + j - free_end[-1]
    pad_dst = jnp.where(owner < N_EXPERTS, in_pad, in_tail) * ROW_TILES
    return tiles(TOK_TILE), tiles(FIN_TM), pad_dst, blk_e, n_used


def _rope_tables(n_tok):
    t = jnp.arange(n_tok)
    row = (t // GRID_W).astype(F32)
    col = (t % GRID_W).astype(F32)
    half = HEAD_DIM // 2
    inv = ROPE_BASE ** (-jnp.arange(0, half, 2, dtype=F32) / half)
    ar = row[:, None] * inv
    ac = col[:, None] * inv
    ang = jnp.concatenate([ar, ar, ac, ac], -1)
    cos, sin = jnp.cos(ang), jnp.sin(ang)
    quarter = (jnp.arange(HEAD_DIM) // (HEAD_DIM // 4)) % 2
    sin_a = jnp.where(quarter == 0, -sin, 0.0)
    sin_b = jnp.where(quarter == 1, sin, 0.0)
    return cos, sin_a, sin_b


def kernel(x_prompt, x_sample, cache_win_k, cache_win_v, state_ssm, state_ret, cache_na_k, cache_na_v, c, c_ctx, w_ada, b_ada, w_in, win_sink, conv_w, conv_b, dt_bias, a_log, d_skip, ssm_norm, ret_decay, na_rpb, w_branch, w_out, ln1_g, ln1_b, router_w, router_bias, exp_w_gu, exp_w_down, sh_w_gu, sh_w_down, ln2_g, ln2_b):
    bc, lc, _ = x_prompt.shape
    bl, ll, _ = x_sample.shape
    assert lc == TOK_TILE and ll % TOK_TILE == 0
    tc, tl = bc * lc, bl * ll
    lat_tiles = ll // TOK_TILE

    nr = -(-(1 + bl) // 8) * 8
    cond = jnp.zeros((nr, D_MODEL), F32).at[0].set(c_ctx).at[1:1 + bl].set(c)
    mod = _ada(cond, w_ada, b_ada).reshape(DEPTH, nr, 6, D_MODEL)
    row_ctx = lambda i: 0
    row_lat = lambda i: 1 + i // lat_tiles

    rope = _rope_tables(ll)
    perm = np.array([d * SSM_HEADS + g * 8 + hh for g in range(SSM_GROUPS) for d in range(2) for hh in range(8)])
    dsk_full = jnp.repeat(d_skip, SSM_HEAD_DIM, axis=1)
    w_in1 = w_in[:, :, :U1_COLS].astype(BF16)
    w_in2 = w_in[:, :, U2_OFF:GATE_OFF].astype(BF16)
    w_ing = w_in[:, :, GATE_OFF:].astype(BF16)
    w_dt = w_in[:, :, DT_OFF:U2_OFF][:, :, perm].astype(BF16)
    w_out_bf = w_out.astype(BF16)
    w_branch_bf = w_branch.astype(BF16)
    rwt_bf = jnp.swapaxes(router_w, 1, 2).astype(BF16)
    sgu_bf = sh_w_gu.astype(BF16)
    sdn_bf = sh_w_down.astype(BF16)

    xs = {"ctx": x_prompt.reshape(tc, D_MODEL), "lat": x_sample.reshape(tl, D_MODEL)}
    new_ctx = []
    for l in range(DEPTH):
        x1 = {}
        bias_tab = _na_bias_table(na_rpb[l], ll)
        for path in ("ctx", "lat"):
            latent = path == "lat"
            x = xs[path]
            seq = ll if latent else lc
            row_fn = row_lat if latent else row_ctx
            h = _modulate(x, mod, l, row_fn)
            u1 = _in_proj(h, w_in1, l, U1_COLS)
            u2 = _in_proj(h, w_in2, l, U2_COLS)
            gates = _in_proj(h, w_ing, l, GATE_COLS, BF16)
            dt, la, lat_t = _ssd_steps(h, w_dt[l], dt_bias[l].reshape(-1)[perm], a_log[l].reshape(-1)[perm])
            xc = _conv(u1, conv_w[l], conv_b[l], seq)
            ys, ssm_fin = _ssd(xc, u1, dt, la, lat_t, dsk_full[l:l + 1], ssm_norm[l:l + 1], seq, l,
                               state_ssm if latent else None)
            yr, ret_fin = _retention(u2, ret_decay, seq, l, rope if latent else None,
                                     state_ret if latent else None)
            if latent:
                ya = _win_attention(u1, cache_win_k, cache_win_v, win_sink, rope, seq, l)
                yn = _na_attention(u2, cache_na_k, cache_na_v, bias_tab, seq, l)
            else:
                ya, yn = _ctx_attention(u1, u2, win_sink, seq, l)
                kv = lambda t, nh: t.reshape(bc, lc, nh, HEAD_DIM)
                new_ctx.append((kv(u1[:, 1024:1280], WIN_KV_HEADS), kv(u1[:, 1280:1536], WIN_KV_HEADS),
                                ssm_fin, ret_fin,
                                kv(u2[:, 4096:5120], NA_HEADS), kv(u2[:, 5120:6144], NA_HEADS)))
            merged = _merge(ya, ys, yr, yn, gates, w_branch_bf, l)
            x1[path] = _out_proj(merged, w_out_bf, x, mod, ln1_g, ln1_b, l, row_fn)

        xp, idx_t, w_t, pos_t, cnt = _router(x1["ctx"], x1["lat"], mod, rwt_bf[l], router_bias[l], l, lat_tiles)
        dest_disp, dest_fin, pad_dst, blk_e, n_used = _dispatch_plan(idx_t, pos_t, cnt)
        x_sorted = _dispatch(xp, dest_disp, pad_dst, blk_e.shape[0] * MOE_BM)
        y_sorted = _experts(x_sorted, blk_e, n_used, exp_w_gu, exp_w_down, l)
        wk = w_t.T
        for path, off, row_fn in (("ctx", 0, row_ctx), ("lat", tc, row_lat)):
            xs[path] = _final(dest_fin, y_sorted, x1[path], xp, wk, mod, ln2_g, ln2_b, sgu_bf, sdn_bf,
                              l, row_fn, off)

    stack = lambda i: jnp.stack([t[i] for t in new_ctx], axis=1)
    return (xs["ctx"].reshape(bc, lc, D_MODEL), xs["lat"].reshape(bl, ll, D_MODEL),
            stack(0), stack(1), stack(2), stack(3), stack(4), stack(5))
```
